```python
import jax, jax.numpy as jnp
from jax import lax
import numpy as np

D_MODEL = 2048
BATCH = 2
SEQ = 4096
DEPTH = 1

CONV_DIM = 1024
CONV_GROUPS = 8
CONV_WIDTH = 3
N_HEADS = 16
QK_NOPE_DIM = 128
QK_ROPE_DIM = 64
QK_HEAD_DIM = QK_NOPE_DIM + QK_ROPE_DIM
V_HEAD_DIM = 128
Q_LORA_RANK = 768
KV_LORA_RANK = 512
ROPE_THETA = 10000.0
Q_BLOCK = 128
D_FF = 5632
NORM_EPS = 1e-6
NEG_INF = -1e30

IN_SIZES = (CONV_DIM, CONV_DIM, CONV_DIM, Q_LORA_RANK, KV_LORA_RANK, QK_ROPE_DIM, 2 * D_MODEL)
IN_WIDTH = sum(IN_SIZES)
IN_SPLIT_IDX = tuple(int(i) for i in np.cumsum(IN_SIZES)[:-1])

kernel_name = "hybrid_gated_conv_mla_convffn_block"


def rmsnorm(x, g):
    xf = x.astype(jnp.float32)
    r = lax.rsqrt(jnp.mean(xf * xf, axis=-1, keepdims=True) + NORM_EPS)
    return (xf * r).astype(x.dtype) * g


def causal_dwconv3(u, w):
    s = u.shape[1]
    up = jnp.pad(u, ((0, 0), (CONV_WIDTH - 1, 0), (0, 0)))
    y = w[0] * up[:, 0:s]
    for k in range(1, CONV_WIDTH):
        y = y + w[k] * up[:, k:k + s]
    return y


def rope_tables(positions, dtype):
    inv_freq = ROPE_THETA ** (-jnp.arange(0, QK_ROPE_DIM, 2, dtype=jnp.float32) / QK_ROPE_DIM)
    ang = positions.astype(jnp.float32)[..., None] * inv_freq
    cos = jnp.cos(ang)[:, :, None, :].astype(dtype)
    sin = jnp.sin(ang)[:, :, None, :].astype(dtype)
    return cos, sin


def apply_rope_tail(x, cos, sin):
    x_nope = x[..., :QK_NOPE_DIM]
    x_r = x[..., QK_NOPE_DIM:]
    x1, x2 = jnp.split(x_r, 2, axis=-1)
    rot = jnp.concatenate([x1 * cos - x2 * sin, x2 * cos + x1 * sin], axis=-1)
    return jnp.concatenate([x_nope, rot], axis=-1)


def causal_block_attention(q, k, v):
    b, s, h, dk = q.shape
    dv = v.shape[-1]
    nb = s // Q_BLOCK
    scale = dk ** -0.5
    qb = q.reshape(b, nb, Q_BLOCK, h, dk).transpose(1, 0, 3, 2, 4)
    kh = k.transpose(0, 2, 1, 3)
    vh = v.transpose(0, 2, 1, 3)
    kpos = jnp.arange(s)

    def one_block(args):
        q_blk, blk = args
        sc = jnp.einsum('bhqd,bhkd->bhqk', q_blk, kh).astype(jnp.float32) * scale
        qpos = blk * Q_BLOCK + jnp.arange(Q_BLOCK)
        mask = kpos[None, :] <= qpos[:, None]
        sc = jnp.where(mask, sc, NEG_INF)
        p = jax.nn.softmax(sc, axis=-1).astype(vh.dtype)
        return jnp.einsum('bhqk,bhkd->bhqd', p, vh)

    o = lax.map(one_block, (qb, jnp.arange(nb)))
    return o.transpose(1, 0, 3, 2, 4).reshape(b, s, h * dv)


def setup_inputs(seed: int = 0) -> dict:
    key = jax.random.key(seed)
    ks = jax.random.split(key, 20)

    def dense(k, fan_in, fan_out):
        return jax.random.normal(k, (DEPTH, fan_in, fan_out), jnp.float32) * fan_in ** -0.5

    def gain(k, n):
        return 1.0 + 0.02 * jax.random.normal(k, (DEPTH, n), jnp.float32)

    x = jax.random.normal(ks[0], (BATCH, SEQ, D_MODEL), jnp.float32)
    positions = jnp.broadcast_to(jnp.arange(SEQ, dtype=jnp.int32), (BATCH, SEQ))
    return {
        "x": x,
        "positions": positions,
        "ln1_g": gain(ks[1], D_MODEL),
        "w_in": dense(ks[2], D_MODEL, IN_WIDTH),
        "b_gate": 0.01 * jax.random.normal(ks[3], (DEPTH, 2 * D_MODEL), jnp.float32),
        "conv_w": jax.random.normal(ks[4], (DEPTH, CONV_WIDTH, CONV_DIM), jnp.float32) * CONV_WIDTH ** -0.5,
        "w_conv_out": dense(ks[5], CONV_DIM, D_MODEL),
        "q_a_g": gain(ks[6], Q_LORA_RANK),
        "w_q_b": dense(ks[7], Q_LORA_RANK, N_HEADS * QK_HEAD_DIM),
        "kv_a_g": gain(ks[8], KV_LORA_RANK),
        "w_kv_b": dense(ks[9], KV_LORA_RANK, N_HEADS * (QK_NOPE_DIM + V_HEAD_DIM)),
        "q_norm_g": gain(ks[10], QK_HEAD_DIM),
        "k_norm_g": gain(ks[11], QK_HEAD_DIM),
        "w_mla_out": dense(ks[12], N_HEADS * V_HEAD_DIM, D_MODEL),
        "w_o": dense(ks[13], D_MODEL, D_MODEL),
        "ln2_g": gain(ks[14], D_MODEL),
        "w_ffn_up": dense(ks[15], D_MODEL, 2 * D_FF),
        "ffn_conv_w": jax.random.normal(ks[16], (DEPTH, CONV_WIDTH, 2 * D_FF), jnp.float32) * CONV_WIDTH ** -0.5,
        "ffn_conv_b": 0.01 * jax.random.normal(ks[17], (DEPTH, 2 * D_FF), jnp.float32),
        "w_ffn_down": dense(ks[18], D_FF, D_MODEL),
    }


def reference(x, positions, ln1_g, w_in, b_gate, conv_w, w_conv_out, q_a_g, w_q_b, kv_a_g, w_kv_b,
              q_norm_g, k_norm_g, w_mla_out, w_o, ln2_g, w_ffn_up, ffn_conv_w, ffn_conv_b, w_ffn_down):
    b, s, _ = x.shape
    cos, sin = rope_tables(positions, x.dtype)
    h = x
    for l in range(DEPTH):
        u = rmsnorm(h, ln1_g[l])
        z = u @ w_in[l]
        zb, zc, zv, q_lat, kv_lat, k_rope, gates = jnp.split(z, IN_SPLIT_IDX, axis=-1)
        gates = jax.nn.sigmoid(gates + b_gate[l])
        gate_a, gate_b = jnp.split(gates, 2, axis=-1)

        y_conv = (zb * causal_dwconv3(zc * zv, conv_w[l])) @ w_conv_out[l]

        q = (rmsnorm(q_lat, q_a_g[l]) @ w_q_b[l]).reshape(b, s, N_HEADS, QK_HEAD_DIM)
        kv = (rmsnorm(kv_lat, kv_a_g[l]) @ w_kv_b[l]).reshape(b, s, N_HEADS, QK_NOPE_DIM + V_HEAD_DIM)
        k_nope, v = jnp.split(kv, [QK_NOPE_DIM], axis=-1)
        k = jnp.concatenate(
            [k_nope, jnp.broadcast_to(k_rope[:, :, None, :], (b, s, N_HEADS, QK_ROPE_DIM))], axis=-1)
        q = apply_rope_tail(rmsnorm(q, q_norm_g[l]), cos, sin)
        k = apply_rope_tail(rmsnorm(k, k_norm_g[l]), cos, sin)
        y_mla = causal_block_attention(q, k, v) @ w_mla_out[l]

        h = h + (gate_a * y_conv + gate_b * y_mla) @ w_o[l]

        u = rmsnorm(h, ln2_g[l])
        a = causal_dwconv3(u @ w_ffn_up[l], ffn_conv_w[l]) + ffn_conv_b[l]
        a_gate, a_up = jnp.split(a, 2, axis=-1)
        h = h + (jax.nn.silu(a_gate) * a_up) @ w_ffn_down[l]
    return h
```

```python
import functools
import math

import jax
import jax.numpy as jnp
import numpy as np
from jax import lax
from jax.experimental import pallas as pl
from jax.experimental.pallas import tpu as pltpu

D_MODEL = 2048
CONV_DIM = 1024
CONV_WIDTH = 3
N_HEADS = 16
QK_NOPE_DIM = 128
QK_ROPE_DIM = 64
QK_HEAD_DIM = QK_NOPE_DIM + QK_ROPE_DIM
V_HEAD_DIM = 128
Q_LORA_RANK = 768
KV_LORA_RANK = 512
ROPE_THETA = 10000.0
D_FF = 5632
NORM_EPS = 1e-6
NEG_INF = -1e30

LANES = 128
HEAD_SLOT = 2 * LANES
HALO = 16
LAT_WIDTH = Q_LORA_RANK + KV_LORA_RANK + LANES
Z_WIDTH = 2 * D_MODEL + 3 * CONV_DIM
VMEM_LIMIT = 56 * 1024 * 1024

TM_PREP = 256
TM_INPROJ = 1024
TN_INPROJ = 1024
TQ_ATTN = 512
TM_MIX = 256
TM_FFN = 512
TF_FFN = 512

F32 = jnp.float32
BF16 = jnp.bfloat16


def _dot(a, b):
    return jnp.dot(a, b, preferred_element_type=F32)


def _rms_scale(x, width):
    return lax.rsqrt(jnp.sum(x * x, axis=-1, keepdims=True) * (1.0 / width) + NORM_EPS)


def _params(*sem):
    return pltpu.CompilerParams(dimension_semantics=sem, vmem_limit_bytes=VMEM_LIMIT)


def _const_spec(shape):
    return pl.BlockSpec(shape, lambda *_: (0,) * len(shape), pipeline_mode=pl.Buffered(1))


def _mla_prep_kernel(x_ref, pos_ref, ln_g_ref, w_lat_ref, qa_g_ref, kva_g_ref, w_q_ref, w_kv_ref,
                     qg_nope_ref, qg_rope_ref, kg_nope_ref, kg_rope_ref, inv_freq_ref,
                     q_ref, k_ref, v_ref, *, q_scale):
    xf = x_ref[...]
    u = (xf * _rms_scale(xf, D_MODEL) * ln_g_ref[...]).astype(BF16)
    lat = _dot(u, w_lat_ref[...])
    q_lat = lat[:, :Q_LORA_RANK]
    kv_lat = lat[:, Q_LORA_RANK:Q_LORA_RANK + KV_LORA_RANK]
    k_rope = lat[:, Q_LORA_RANK + KV_LORA_RANK:]
    qn = (q_lat * _rms_scale(q_lat, Q_LORA_RANK) * qa_g_ref[...]).astype(BF16)
    kvn = (kv_lat * _rms_scale(kv_lat, KV_LORA_RANK) * kva_g_ref[...]).astype(BF16)
    qr = _dot(qn, w_q_ref[...])
    kvr = _dot(kvn, w_kv_ref[...])

    ang = pos_ref[...].astype(F32) * inv_freq_ref[...]
    cos = jnp.cos(ang)
    sin = jnp.sin(ang)
    lane = lax.broadcasted_iota(jnp.int32, (1, LANES), 1)
    first_half = (lane % QK_ROPE_DIM) < (QK_ROPE_DIM // 2)
    sin_signed = jnp.where(first_half, -sin, sin)

    def rope(t):
        rot = jnp.where(first_half, pltpu.roll(t, LANES - QK_ROPE_DIM // 2, axis=1),
                        pltpu.roll(t, QK_ROPE_DIM // 2, axis=1))
        return t * cos + rot * sin_signed

    qg_nope = qg_nope_ref[...] * q_scale
    qg_rope = qg_rope_ref[...] * q_scale
    kg_nope = kg_nope_ref[...]
    k_tail = rope(k_rope * kg_rope_ref[...])
    k_rope_ss = 0.5 * jnp.sum(k_rope * k_rope, axis=-1, keepdims=True)

    nope_w = N_HEADS * QK_NOPE_DIM
    for pair in range(N_HEADS // 2):
        q_pair = qr[:, nope_w + pair * LANES: nope_w + (pair + 1) * LANES]
        q_pair_roped = rope(q_pair * qg_rope)
        for sub in range(2):
            h = 2 * pair + sub
            mine = (lane // QK_ROPE_DIM) == sub
            q_nope = qr[:, h * QK_NOPE_DIM:(h + 1) * QK_NOPE_DIM]
            ss = (jnp.sum(q_nope * q_nope, axis=-1, keepdims=True)
                  + jnp.sum(jnp.where(mine, q_pair * q_pair, 0.0), axis=-1, keepdims=True))
            r = lax.rsqrt(ss * (1.0 / QK_HEAD_DIM) + NORM_EPS)
            q_ref[:, h * HEAD_SLOT: h * HEAD_SLOT + LANES] = (q_nope * r * qg_nope).astype(BF16)
            q_ref[:, h * HEAD_SLOT + LANES:(h + 1) * HEAD_SLOT] = jnp.where(
                mine, q_pair_roped * r, 0.0).astype(BF16)

            k_nope = kvr[:, h * QK_NOPE_DIM:(h + 1) * QK_NOPE_DIM]
            kss = jnp.sum(k_nope * k_nope, axis=-1, keepdims=True) + k_rope_ss
            kr = lax.rsqrt(kss * (1.0 / QK_HEAD_DIM) + NORM_EPS)
            k_ref[:, h * HEAD_SLOT: h * HEAD_SLOT + LANES] = (k_nope * kr * kg_nope).astype(BF16)
            k_ref[:, h * HEAD_SLOT + LANES:(h + 1) * HEAD_SLOT] = (k_tail * kr).astype(BF16)

    v_ref[...] = kvr[:, nope_w:].astype(BF16)


def _mla_prep(x2, pos, ln_g, w_lat, qa_g, kva_g, w_q, w_kv, qg_nope, qg_rope, kg_nope, kg_rope, inv_freq):
    t = x2.shape[0]
    tm = TM_PREP
    q_scale = QK_HEAD_DIM ** -0.5 * math.log2(math.e)
    row = lambda i: (i, 0)
    return pl.pallas_call(
        functools.partial(_mla_prep_kernel, q_scale=q_scale),
        grid=(t // tm,),
        in_specs=[
            pl.BlockSpec((tm, D_MODEL), row),
            pl.BlockSpec((tm, 1), row),
            _const_spec((1, D_MODEL)),
            _const_spec((D_MODEL, LAT_WIDTH)),
            _const_spec((1, Q_LORA_RANK)),
            _const_spec((1, KV_LORA_RANK)),
            _const_spec(w_q.shape),
            _const_spec(w_kv.shape),
            _const_spec((1, LANES)), _const_spec((1, LANES)),
            _const_spec((1, LANES)), _const_spec((1, LANES)),
            _const_spec((1, LANES)),
        ],
        out_specs=[
            pl.BlockSpec((tm, N_HEADS * HEAD_SLOT), row),
            pl.BlockSpec((tm, N_HEADS * HEAD_SLOT), row),
            pl.BlockSpec((tm, N_HEADS * V_HEAD_DIM), row),
        ],
        out_shape=[
            jax.ShapeDtypeStruct((t, N_HEADS * HEAD_SLOT), BF16),
            jax.ShapeDtypeStruct((t, N_HEADS * HEAD_SLOT), BF16),
            jax.ShapeDtypeStruct((t, N_HEADS * V_HEAD_DIM), BF16),
        ],
        compiler_params=_params("arbitrary"),
        name="mla_prep",
    )(x2, pos, ln_g, w_lat, qa_g, kva_g, w_q, w_kv, qg_nope, qg_rope, kg_nope, kg_rope, inv_freq)


def _in_proj_kernel(x_ref, ln_g_ref, w_ref, z_ref, u_ref):
    @pl.when(pl.program_id(1) == 0)
    def _():
        xf = x_ref[...]
        u_ref[...] = (xf * _rms_scale(xf, D_MODEL) * ln_g_ref[...]).astype(BF16)

    z_ref[...] = _dot(u_ref[...], w_ref[...]).astype(BF16)


def _in_proj(x2, ln_g, w_z):
    t = x2.shape[0]
    tm, tn = TM_INPROJ, TN_INPROJ
    return pl.pallas_call(
        _in_proj_kernel,
        grid=(t // tm, Z_WIDTH // tn),
        in_specs=[
            pl.BlockSpec((tm, D_MODEL), lambda i, j: (i, 0)),
            pl.BlockSpec((1, D_MODEL), lambda i, j: (0, 0)),
            pl.BlockSpec((D_MODEL, tn), lambda i, j: (0, j)),
        ],
        out_specs=pl.BlockSpec((tm, tn), lambda i, j: (i, j)),
        out_shape=jax.ShapeDtypeStruct((t, Z_WIDTH), BF16),
        scratch_shapes=[pltpu.VMEM((tm, D_MODEL), BF16)],
        compiler_params=_params("arbitrary", "arbitrary"),
        name="in_proj",
    )(x2, ln_g, w_z)


def _attn_kernel(q_ref, k_ref, v_ref, o_ref, *, tq):
    i = pl.program_id(2)
    q = q_ref[...]

    def step(j, carry, masked):
        m, l, acc = carry
        start = pl.multiple_of(j * tq, tq)
        kb = k_ref[pl.ds(start, tq), :]
        vb = v_ref[pl.ds(start, tq), :]
        s = lax.dot_general(q, kb, (((1,), (1,)), ((), ())), preferred_element_type=F32)
        if masked:
            row = lax.broadcasted_iota(jnp.int32, (tq, tq), 0)
            col = lax.broadcasted_iota(jnp.int32, (tq, tq), 1)
            s = jnp.where(col <= row, s, NEG_INF)
        m_new = jnp.maximum(m, jnp.max(s, axis=-1, keepdims=True))
        alpha = jnp.exp2(m - m_new)
        p = jnp.exp2(s - m_new)
        l = alpha * l + jnp.sum(p, axis=-1, keepdims=True)
        acc = alpha * acc + _dot(p.astype(BF16), vb)
        return m_new, l, acc

    init = (jnp.full((tq, 1), NEG_INF, F32), jnp.zeros((tq, 1), F32), jnp.zeros((tq, V_HEAD_DIM), F32))
    carry = lax.fori_loop(0, i, lambda j, c: step(j, c, False), init)
    _, l, acc = step(i, carry, True)
    o_ref[...] = (acc / l).astype(BF16)


def _attn(q, k, v, batch, seq):
    tq = TQ_ATTN
    nq = seq // tq
    return pl.pallas_call(
        functools.partial(_attn_kernel, tq=tq),
        grid=(batch, N_HEADS, nq),
        in_specs=[
            pl.BlockSpec((tq, HEAD_SLOT), lambda b, h, i: (b * nq + i, h)),
            pl.BlockSpec((seq, HEAD_SLOT), lambda b, h, i: (b, h)),
            pl.BlockSpec((seq, V_HEAD_DIM), lambda b, h, i: (b, h)),
        ],
        out_specs=pl.BlockSpec((tq, V_HEAD_DIM), lambda b, h, i: (b * nq + i, h)),
        out_shape=jax.ShapeDtypeStruct((batch * seq, N_HEADS * V_HEAD_DIM), BF16),
        compiler_params=_params("arbitrary", "arbitrary", "arbitrary"),
        name="attn",
    )(q, k, v)


def _mix_out_kernel(gates_ref, zb_ref, zc_ref, zv_ref, zc_prev_ref, zv_prev_ref, o_ref, x_ref,
                    b_gate_ref, conv_w_ref, w_conv_out_ref, w_mla_out_ref, w_o_ref, ln2_g_ref,
                    h_ref, u2_ref, cv_ref, *, tm, tiles_per_seq):
    seq_start = (pl.program_id(0) % tiles_per_seq) == 0
    prev = zc_prev_ref[...].astype(F32) * zv_prev_ref[...].astype(F32)
    cv_ref[0:HALO, :] = jnp.where(seq_start, 0.0, prev)
    cv_ref[HALO:, :] = zc_ref[...].astype(F32) * zv_ref[...].astype(F32)
    cw = conv_w_ref[...]
    conv = (cw[0:1, :] * cv_ref[HALO - 2:HALO - 2 + tm, :]
            + cw[1:2, :] * cv_ref[HALO - 1:HALO - 1 + tm, :]
            + cw[2:3, :] * cv_ref[HALO:HALO + tm, :])
    conv_in = (zb_ref[...].astype(F32) * conv).astype(BF16)
    y_conv = _dot(conv_in, w_conv_out_ref[...])
    y_mla = _dot(o_ref[...], w_mla_out_ref[...])
    gates = jax.nn.sigmoid(gates_ref[...].astype(F32) + b_gate_ref[...])
    mix = (gates[:, :D_MODEL] * y_conv + gates[:, D_MODEL:] * y_mla).astype(BF16)
    h = x_ref[...] + _dot(mix, w_o_ref[...])
    h_ref[...] = h
    u2_ref[...] = (h * _rms_scale(h, D_MODEL) * ln2_g_ref[...]).astype(BF16)


def _mix_out(z, o, x2, b_gate, conv_w, w_conv_out, w_mla_out, w_o, ln2_g, seq):
    t = x2.shape[0]
    tm = TM_MIX
    halo_per_tile = tm // HALO
    cblk = lambda c: (lambda i: (i, c))
    prev = lambda c: (lambda i: (jnp.maximum(i * halo_per_tile - 1, 0), c))
    gate_blocks = 2 * D_MODEL // CONV_DIM
    return pl.pallas_call(
        functools.partial(_mix_out_kernel, tm=tm, tiles_per_seq=seq // tm),
        grid=(t // tm,),
        in_specs=[
            pl.BlockSpec((tm, 2 * D_MODEL), cblk(0)),
            pl.BlockSpec((tm, CONV_DIM), cblk(gate_blocks)),
            pl.BlockSpec((tm, CONV_DIM), cblk(gate_blocks + 1)),
            pl.BlockSpec((tm, CONV_DIM), cblk(gate_blocks + 2)),
            pl.BlockSpec((HALO, CONV_DIM), prev(gate_blocks + 1)),
            pl.BlockSpec((HALO, CONV_DIM), prev(gate_blocks + 2)),
            pl.BlockSpec((tm, D_MODEL), cblk(0)),
            pl.BlockSpec((tm, D_MODEL), cblk(0)),
            _const_spec((1, 2 * D_MODEL)),
            _const_spec((CONV_WIDTH, CONV_DIM)),
            _const_spec((CONV_DIM, D_MODEL)),
            _const_spec((D_MODEL, D_MODEL)),
            _const_spec((D_MODEL, D_MODEL)),
            _const_spec((1, D_MODEL)),
        ],
        out_specs=[pl.BlockSpec((tm, D_MODEL), cblk(0)), pl.BlockSpec((tm, D_MODEL), cblk(0))],
        out_shape=[jax.ShapeDtypeStruct((t, D_MODEL), F32), jax.ShapeDtypeStruct((t, D_MODEL), BF16)],
        scratch_shapes=[pltpu.VMEM((tm + HALO, CONV_DIM), F32)],
        compiler_params=_params("arbitrary"),
        name="mix_out",
    )(z, z, z, z, z, z, o, x2, b_gate, conv_w, w_conv_out, w_mla_out, w_o, ln2_g)


def _ffn_kernel(u_ref, u_prev_ref, h_ref, wg_ref, wu_ref, cwg_ref, cwu_ref, bg_ref, bu_ref, wd_ref,
                out_ref, lhs_ref, pg_ref, pu_ref, *, tm, tiles_per_seq):
    @pl.when(pl.program_id(1) == 0)
    def _():
        seq_start = (pl.program_id(0) % tiles_per_seq) == 0
        lhs_ref[0:HALO, :] = jnp.where(seq_start, jnp.zeros_like(u_prev_ref), u_prev_ref[...])
        lhs_ref[HALO:, :] = u_ref[...]
        out_ref[...] = h_ref[...]

    lhs = lhs_ref[...]
    pg_ref[...] = _dot(lhs, wg_ref[...])
    pu_ref[...] = _dot(lhs, wu_ref[...])

    def conv(p_ref, cw_ref, b_ref):
        cw = cw_ref[...]
        return (cw[0:1, :] * p_ref[HALO - 2:HALO - 2 + tm, :]
                + cw[1:2, :] * p_ref[HALO - 1:HALO - 1 + tm, :]
                + cw[2:3, :] * p_ref[HALO:HALO + tm, :]) + b_ref[...]

    a_gate = conv(pg_ref, cwg_ref, bg_ref)
    a_up = conv(pu_ref, cwu_ref, bu_ref)
    act = (a_gate * jax.nn.sigmoid(a_gate) * a_up).astype(BF16)
    out_ref[...] += _dot(act, wd_ref[...])


def _ffn(u2, h1, w_up, conv_w, conv_b, w_down, seq):
    t = u2.shape[0]
    tm, tf = TM_FFN, TF_FFN
    nf = D_FF // tf
    halo_per_tile = tm // HALO
    return pl.pallas_call(
        functools.partial(_ffn_kernel, tm=tm, tiles_per_seq=seq // tm),
        grid=(t // tm, nf),
        in_specs=[
            pl.BlockSpec((tm, D_MODEL), lambda i, j: (i, 0)),
            pl.BlockSpec((HALO, D_MODEL), lambda i, j: (jnp.maximum(i * halo_per_tile - 1, 0), 0)),
            pl.BlockSpec((tm, D_MODEL), lambda i, j: (i, 0)),
            pl.BlockSpec((D_MODEL, tf), lambda i, j: (0, j)),
            pl.BlockSpec((D_MODEL, tf), lambda i, j: (0, j + nf)),
            pl.BlockSpec((CONV_WIDTH, tf), lambda i, j: (0, j)),
            pl.BlockSpec((CONV_WIDTH, tf), lambda i, j: (0, j + nf)),
            pl.BlockSpec((1, tf), lambda i, j: (0, j)),
            pl.BlockSpec((1, tf), lambda i, j: (0, j + nf)),
            pl.BlockSpec((tf, D_MODEL), lambda i, j: (j, 0)),
        ],
        out_specs=pl.BlockSpec((tm, D_MODEL), lambda i, j: (i, 0)),
        out_shape=jax.ShapeDtypeStruct((t, D_MODEL), F32),
        scratch_shapes=[
            pltpu.VMEM((tm + HALO, D_MODEL), BF16),
            pltpu.VMEM((tm + HALO, tf), F32),
            pltpu.VMEM((tm + HALO, tf), F32),
        ],
        compiler_params=_params("arbitrary", "arbitrary"),
        name="ffn",
    )(u2, u2, h1, w_up, w_up, conv_w, conv_w, conv_b, conv_b, w_down)


def _row(v):
    return v.reshape(1, -1).astype(F32)


def _layer(h2, pos, inv_freq, batch, seq, ln1_g, w_in, b_gate, conv_w, w_conv_out, q_a_g, w_q_b, kv_a_g, w_kv_b,
           q_norm_g, k_norm_g, w_mla_out, w_o, ln2_g, w_ffn_up, ffn_conv_w, ffn_conv_b, w_ffn_down):
    c0 = 3 * CONV_DIM
    c1 = c0 + Q_LORA_RANK + KV_LORA_RANK
    c2 = c1 + QK_ROPE_DIM
    w_z = jnp.concatenate([w_in[:, c2:], w_in[:, :c0]], axis=1).astype(BF16)
    w_lat = jnp.concatenate([w_in[:, c0:c1], w_in[:, c1:c2], w_in[:, c1:c2]], axis=1).astype(BF16)
    wq3 = w_q_b.reshape(Q_LORA_RANK, N_HEADS, QK_HEAD_DIM)
    w_q = jnp.concatenate([wq3[:, :, :QK_NOPE_DIM].reshape(Q_LORA_RANK, -1),
                           wq3[:, :, QK_NOPE_DIM:].reshape(Q_LORA_RANK, -1)], axis=1).astype(BF16)
    wkv3 = w_kv_b.reshape(KV_LORA_RANK, N_HEADS, QK_NOPE_DIM + V_HEAD_DIM)
    w_kv = jnp.concatenate([wkv3[:, :, :QK_NOPE_DIM].reshape(KV_LORA_RANK, -1),
                            wkv3[:, :, QK_NOPE_DIM:].reshape(KV_LORA_RANK, -1)], axis=1).astype(BF16)
    twice = lambda g: _row(jnp.concatenate([g, g]))

    q, k, v = _mla_prep(h2, pos, _row(ln1_g), w_lat, _row(q_a_g), _row(kv_a_g), w_q, w_kv,
                        _row(q_norm_g[:QK_NOPE_DIM]), twice(q_norm_g[QK_NOPE_DIM:]),
                        _row(k_norm_g[:QK_NOPE_DIM]), twice(k_norm_g[QK_NOPE_DIM:]), inv_freq)
    z = _in_proj(h2, _row(ln1_g), w_z)
    o = _attn(q, k, v, batch, seq)
    h1, u2 = _mix_out(z, o, h2, _row(b_gate), conv_w.astype(F32), w_conv_out.astype(BF16),
                      w_mla_out.astype(BF16), w_o.astype(BF16), _row(ln2_g), seq)
    return _ffn(u2, h1, w_ffn_up.astype(BF16), ffn_conv_w.astype(F32), _row(ffn_conv_b),
                w_ffn_down.astype(BF16), seq)


def kernel(x, positions, ln1_g, w_in, b_gate, conv_w, w_conv_out, q_a_g, w_q_b, kv_a_g, w_kv_b,
           q_norm_g, k_norm_g, w_mla_out, w_o, ln2_g, w_ffn_up, ffn_conv_w, ffn_conv_b, w_ffn_down):
    batch, seq, _ = x.shape
    depth = ln1_g.shape[0]
    h2 = x.reshape(batch * seq, D_MODEL)
    pos = positions.reshape(batch * seq, 1).astype(jnp.int32)
    freq = ROPE_THETA ** (-jnp.arange(0, QK_ROPE_DIM, 2, dtype=F32) / QK_ROPE_DIM)
    inv_freq = jnp.tile(freq, LANES // (QK_ROPE_DIM // 2)).reshape(1, LANES)
    for l in range(depth):
        h2 = _layer(h2, pos, inv_freq, batch, seq, ln1_g[l], w_in[l], b_gate[l], conv_w[l], w_conv_out[l],
                    q_a_g[l], w_q_b[l], kv_a_g[l], w_kv_b[l], q_norm_g[l], k_norm_g[l], w_mla_out[l],
                    w_o[l], ln2_g[l], w_ffn_up[l], ffn_conv_w[l], ffn_conv_b[l], w_ffn_down[l])
    return h2.reshape(batch, seq, D_MODEL)
```

```python
import functools
import math

import jax
import jax.numpy as jnp
from jax import lax
from jax.experimental import pallas as pl
from jax.experimental.pallas import tpu as pltpu

D_MODEL = 2048
CONV_DIM = 1024
CONV_WIDTH = 3
N_HEADS = 16
QK_NOPE_DIM = 128
QK_ROPE_DIM = 64
QK_HEAD_DIM = QK_NOPE_DIM + QK_ROPE_DIM
V_HEAD_DIM = 128
Q_LORA_RANK = 768
KV_LORA_RANK = 512
ROPE_THETA = 10000.0
D_FF = 5632
NORM_EPS = 1e-6
NEG_INF = -1e30

LANES = 128
MXU_DIM = 256
HEAD_SLOT = 2 * LANES
HALO = 16
LAT_OFFSET = 3 * CONV_DIM
LAT_WIDTH = 1536
GATE_OFFSET = LAT_OFFSET + Q_LORA_RANK + KV_LORA_RANK + QK_ROPE_DIM
Z_WIDTH = 2 * D_MODEL + 3 * CONV_DIM
VMEM_LIMIT = 56 * 1024 * 1024

TM_PREP = 256
TM_INPROJ = 1024
TN_INPROJ = 1024
TQ_ATTN = 512
HEADS_PER_STEP = 4
TM_MIX = 256
TM_FFN = 1024
TF_FFN = 512

F32 = jnp.float32
BF16 = jnp.bfloat16


def _dot(a, b):
    return jnp.dot(a, b, preferred_element_type=F32)


def _rms_scale(x, width):
    return lax.rsqrt(jnp.sum(x * x, axis=-1, keepdims=True) * (1.0 / width) + NORM_EPS)


def _params(*sem):
    return pltpu.CompilerParams(dimension_semantics=sem, vmem_limit_bytes=VMEM_LIMIT)


def _const_spec(shape):
    return pl.BlockSpec(shape, lambda *_: (0,) * len(shape), pipeline_mode=pl.Buffered(1))


def _causal_conv3(p, cw, rows):
    full = cw[2:3, :] * p + cw[1:2, :] * pltpu.roll(p, 1, axis=0) + cw[0:1, :] * pltpu.roll(p, 2, axis=0)
    return full[p.shape[0] - rows:, :]


def _mla_prep_kernel(x_ref, pos_ref, ln_g_ref, w_lat_ref, qa_g_ref, kva_g_ref, w_qn_ref, w_qr_ref,
                     w_kn_ref, w_v_ref, qg_nope_ref, qg_rope_ref, kg_nope_ref, kg_rope_ref, inv_freq_ref,
                     q_ref, k_ref, vt_ref, *, q_scale):
    xf = x_ref[...]
    u = (xf * _rms_scale(xf, D_MODEL) * ln_g_ref[...]).astype(BF16)
    lat = _dot(u, w_lat_ref[...])
    q_lat = lat[:, :Q_LORA_RANK]
    kv_lat = lat[:, Q_LORA_RANK:Q_LORA_RANK + KV_LORA_RANK]
    k_rope = lat[:, Q_LORA_RANK + KV_LORA_RANK:Q_LORA_RANK + KV_LORA_RANK + LANES]
    qn = (q_lat * _rms_scale(q_lat, Q_LORA_RANK) * qa_g_ref[...]).astype(BF16)
    kvn = (kv_lat * _rms_scale(kv_lat, KV_LORA_RANK) * kva_g_ref[...]).astype(BF16)
    q_nope_all = _dot(qn, w_qn_ref[...])
    q_rope_all = _dot(qn, w_qr_ref[...])
    k_nope_all = _dot(kvn, w_kn_ref[...])
    v_all = _dot(kvn, w_v_ref[...])

    ang = pos_ref[...].astype(F32) * inv_freq_ref[...]
    cos = jnp.cos(ang)
    sin = jnp.sin(ang)
    lane = lax.broadcasted_iota(jnp.int32, (1, LANES), 1)
    first_half = (lane % QK_ROPE_DIM) < (QK_ROPE_DIM // 2)
    sin_signed = jnp.where(first_half, -sin, sin)

    def rope(t):
        rot = jnp.where(first_half, pltpu.roll(t, LANES - QK_ROPE_DIM // 2, axis=1),
                        pltpu.roll(t, QK_ROPE_DIM // 2, axis=1))
        return t * cos + rot * sin_signed

    k_rope = jnp.where(lane < QK_ROPE_DIM, k_rope, pltpu.roll(k_rope, QK_ROPE_DIM, axis=1))
    qg_nope = qg_nope_ref[...] * q_scale
    qg_rope = qg_rope_ref[...] * q_scale
    kg_nope = kg_nope_ref[...]
    k_tail = rope(k_rope * kg_rope_ref[...])
    k_rope_ss = 0.5 * jnp.sum(k_rope * k_rope, axis=-1, keepdims=True)

    for pair in range(N_HEADS // 2):
        q_pair = q_rope_all[:, pair * LANES:(pair + 1) * LANES]
        q_pair_roped = rope(q_pair * qg_rope)
        for sub in range(2):
            h = 2 * pair + sub
            mine = (lane // QK_ROPE_DIM) == sub
            q_nope = q_nope_all[:, h * QK_NOPE_DIM:(h + 1) * QK_NOPE_DIM]
            ss = (jnp.sum(q_nope * q_nope, axis=-1, keepdims=True)
                  + jnp.sum(jnp.where(mine, q_pair * q_pair, 0.0), axis=-1, keepdims=True))
            r = lax.rsqrt(ss * (1.0 / QK_HEAD_DIM) + NORM_EPS)
            q_ref[:, h * HEAD_SLOT: h * HEAD_SLOT + LANES] = (q_nope * r * qg_nope).astype(BF16)
            q_ref[:, h * HEAD_SLOT + LANES:(h + 1) * HEAD_SLOT] = jnp.where(
                mine, q_pair_roped * r, 0.0).astype(BF16)

            k_nope = k_nope_all[:, h * QK_NOPE_DIM:(h + 1) * QK_NOPE_DIM]
            kss = jnp.sum(k_nope * k_nope, axis=-1, keepdims=True) + k_rope_ss
            kr = lax.rsqrt(kss * (1.0 / QK_HEAD_DIM) + NORM_EPS)
            k_ref[:, h * HEAD_SLOT: h * HEAD_SLOT + LANES] = (k_nope * kr * kg_nope).astype(BF16)
            k_ref[:, h * HEAD_SLOT + LANES:(h + 1) * HEAD_SLOT] = (k_tail * kr).astype(BF16)

    vt_ref[0] = v_all.astype(BF16).T


def _mla_prep(x2, pos, ln_g, w_lat, qa_g, kva_g, w_qn, w_qr, w_kn, w_v, qg_nope, qg_rope, kg_nope, kg_rope,
              inv_freq):
    t = x2.shape[0]
    tm = TM_PREP
    q_scale = QK_HEAD_DIM ** -0.5 * math.log2(math.e)
    row = lambda i: (i, 0)
    return pl.pallas_call(
        functools.partial(_mla_prep_kernel, q_scale=q_scale),
        grid=(t // tm,),
        in_specs=[
            pl.BlockSpec((tm, D_MODEL), row),
            pl.BlockSpec((tm, 1), row),
            _const_spec((1, D_MODEL)),
            _const_spec((D_MODEL, LAT_WIDTH)),
            _const_spec((1, Q_LORA_RANK)),
            _const_spec((1, KV_LORA_RANK)),
            _const_spec(w_qn.shape), _const_spec(w_qr.shape),
            _const_spec(w_kn.shape), _const_spec(w_v.shape),
            _const_spec((1, LANES)), _const_spec((1, LANES)),
            _const_spec((1, LANES)), _const_spec((1, LANES)),
            _const_spec((1, LANES)),
        ],
        out_specs=[
            pl.BlockSpec((tm, N_HEADS * HEAD_SLOT), row),
            pl.BlockSpec((tm, N_HEADS * HEAD_SLOT), row),
            pl.BlockSpec((1, N_HEADS * V_HEAD_DIM, tm), lambda i: (i, 0, 0)),
        ],
        out_shape=[
            jax.ShapeDtypeStruct((t, N_HEADS * HEAD_SLOT), BF16),
            jax.ShapeDtypeStruct((t, N_HEADS * HEAD_SLOT), BF16),
            jax.ShapeDtypeStruct((t // tm, N_HEADS * V_HEAD_DIM, tm), BF16),
        ],
        compiler_params=_params("arbitrary"),
        name="mla_prep",
    )(x2, pos, ln_g, w_lat, qa_g, kva_g, w_qn, w_qr, w_kn, w_v, qg_nope, qg_rope, kg_nope, kg_rope, inv_freq)


def _in_proj_kernel(x_ref, ln_g_ref, w_gate_ref, w_conv_ref, z_ref, u_ref, *, gate_tiles):
    j = pl.program_id(1)

    @pl.when(j == 0)
    def _():
        xf = x_ref[...]
        u_ref[...] = (xf * _rms_scale(xf, D_MODEL) * ln_g_ref[...]).astype(BF16)

    @pl.when(j < gate_tiles)
    def _():
        z_ref[...] = _dot(u_ref[...], w_gate_ref[...]).astype(BF16)

    @pl.when(j >= gate_tiles)
    def _():
        z_ref[...] = _dot(u_ref[...], w_conv_ref[...].astype(BF16)).astype(BF16)


def _in_proj(x2, ln_g, w_gate, w_in):
    t = x2.shape[0]
    tm, tn = TM_INPROJ, TN_INPROJ
    gate_tiles = 2 * D_MODEL // tn
    return pl.pallas_call(
        functools.partial(_in_proj_kernel, gate_tiles=gate_tiles),
        grid=(t // tm, Z_WIDTH // tn),
        in_specs=[
            pl.BlockSpec((tm, D_MODEL), lambda i, j: (i, 0)),
            pl.BlockSpec((1, D_MODEL), lambda i, j: (0, 0)),
            pl.BlockSpec((D_MODEL, tn), lambda i, j: (0, jnp.minimum(j, gate_tiles - 1))),
            pl.BlockSpec((D_MODEL, tn), lambda i, j: (0, jnp.maximum(j - gate_tiles, 0))),
        ],
        out_specs=pl.BlockSpec((tm, tn), lambda i, j: (i, j)),
        out_shape=jax.ShapeDtypeStruct((t, Z_WIDTH), BF16),
        scratch_shapes=[pltpu.VMEM((tm, D_MODEL), BF16)],
        compiler_params=_params("arbitrary", "arbitrary"),
        name="in_proj",
    )(x2, ln_g, w_gate, w_in)


def _attn_kernel(q_ref, k_ref, vt_ref, o_ref, *, tq, vt_cols):
    i = pl.program_id(2)
    sub_blocks = tq // vt_cols

    def step(j, carry, masked):
        start = pl.multiple_of(j * tq, tq)
        scores = []
        for hh in range(HEADS_PER_STEP):
            q = q_ref[:, hh * HEAD_SLOT:(hh + 1) * HEAD_SLOT]
            kb = k_ref[pl.ds(start, tq), hh * HEAD_SLOT:(hh + 1) * HEAD_SLOT]
            scores.append(lax.dot_general(kb, q, (((1,), (1,)), ((), ())), preferred_element_type=F32))
        out = []
        for hh in range(HEADS_PER_STEP):
            m, l, acc = carry[hh]
            s = scores[hh]
            if masked:
                kv_idx = lax.broadcasted_iota(jnp.int32, (tq, tq), 0)
                q_idx = lax.broadcasted_iota(jnp.int32, (tq, tq), 1)
                s = jnp.where(kv_idx <= q_idx, s, NEG_INF)
            m_new = jnp.maximum(m, jnp.max(s, axis=0, keepdims=True))
            alpha = jnp.exp2(m - m_new)
            p = jnp.exp2(s - m_new)
            l = alpha * l + jnp.sum(p, axis=0, keepdims=True)
            pb = p.astype(BF16)
            pv = None
            for c in range(sub_blocks):
                vt = vt_ref[j * sub_blocks + c, hh * V_HEAD_DIM:(hh + 1) * V_HEAD_DIM, :]
                part = _dot(vt, pb[c * vt_cols:(c + 1) * vt_cols, :])
                pv = part if pv is None else pv + part
            out.append((m_new, l, alpha * acc + pv))
        return tuple(out)

    init = tuple((jnp.full((1, tq), NEG_INF, F32), jnp.zeros((1, tq), F32), jnp.zeros((V_HEAD_DIM, tq), F32))
                 for _ in range(HEADS_PER_STEP))
    carry = lax.fori_loop(0, i, lambda j, c: step(j, c, False), init)
    carry = step(i, carry, True)
    for hh in range(HEADS_PER_STEP):
        _, l, acc = carry[hh]
        o_ref[:, hh * V_HEAD_DIM:(hh + 1) * V_HEAD_DIM] = (acc / l).T.astype(BF16)


def _attn(q, k, vt, batch, seq):
    tq = TQ_ATTN
    nq = seq // tq
    vt_cols = vt.shape[-1]
    hs = HEADS_PER_STEP
    return pl.pallas_call(
        functools.partial(_attn_kernel, tq=tq, vt_cols=vt_cols),
        grid=(batch, N_HEADS // hs, nq),
        in_specs=[
            pl.BlockSpec((tq, hs * HEAD_SLOT), lambda b, h, i: (b * nq + i, h)),
            pl.BlockSpec((seq, hs * HEAD_SLOT), lambda b, h, i: (b, h)),
            pl.BlockSpec((seq // vt_cols, hs * V_HEAD_DIM, vt_cols), lambda b, h, i: (b, h, 0)),
        ],
        out_specs=pl.BlockSpec((tq, hs * V_HEAD_DIM), lambda b, h, i: (b * nq + i, h)),
        out_shape=jax.ShapeDtypeStruct((batch * seq, N_HEADS * V_HEAD_DIM), BF16),
        compiler_params=_params("arbitrary", "arbitrary", "arbitrary"),
        name="attn",
    )(q, k, vt)


def _mix_out_kernel(gates_ref, zb_ref, zc_ref, zv_ref, zc_prev_ref, zv_prev_ref, o_ref, x_ref,
                    b_gate_ref, conv_w_ref, w_conv_out_ref, w_mla_out_ref, w_o_ref, ln2_g_ref,
                    h_ref, u2_ref, *, tm, tiles_per_seq):
    seq_start = (pl.program_id(0) % tiles_per_seq) == 0
    prev = zc_prev_ref[...].astype(F32) * zv_prev_ref[...].astype(F32)
    cv = jnp.concatenate([jnp.where(seq_start, 0.0, prev), zc_ref[...].astype(F32) * zv_ref[...].astype(F32)], axis=0)
    conv = _causal_conv3(cv, conv_w_ref[...], tm)
    conv_in = (zb_ref[...].astype(F32) * conv).astype(BF16)
    y_conv = _dot(conv_in, w_conv_out_ref[...])
    y_mla = _dot(o_ref[...], w_mla_out_ref[...])
    gates = jax.nn.sigmoid(gates_ref[...].astype(F32) + b_gate_ref[...])
    mix = (gates[:, :D_MODEL] * y_conv + gates[:, D_MODEL:] * y_mla).astype(BF16)
    h = x_ref[...] + _dot(mix, w_o_ref[...])
    h_ref[...] = h
    u2_ref[...] = (h * _rms_scale(h, D_MODEL) * ln2_g_ref[...]).astype(BF16)


def _mix_out(z, o, x2, b_gate, conv_w, w_conv_out, w_mla_out, w_o, ln2_g, seq):
    t = x2.shape[0]
    tm = TM_MIX
    halo_per_tile = tm // HALO
    cblk = lambda c: (lambda i: (i, c))
    prev = lambda c: (lambda i: (jnp.maximum(i * halo_per_tile - 1, 0), c))
    gate_blocks = 2 * D_MODEL // CONV_DIM
    return pl.pallas_call(
        functools.partial(_mix_out_kernel, tm=tm, tiles_per_seq=seq // tm),
        grid=(t // tm,),
        in_specs=[
            pl.BlockSpec((tm, 2 * D_MODEL), cblk(0)),
            pl.BlockSpec((tm, CONV_DIM), cblk(gate_blocks)),
            pl.BlockSpec((tm, CONV_DIM), cblk(gate_blocks + 1)),
            pl.BlockSpec((tm, CONV_DIM), cblk(gate_blocks + 2)),
            pl.BlockSpec((HALO, CONV_DIM), prev(gate_blocks + 1)),
            pl.BlockSpec((HALO, CONV_DIM), prev(gate_blocks + 2)),
            pl.BlockSpec((tm, D_MODEL), cblk(0)),
            pl.BlockSpec((tm, D_MODEL), cblk(0)),
            _const_spec((1, 2 * D_MODEL)),
            _const_spec((CONV_WIDTH, CONV_DIM)),
            _const_spec((CONV_DIM, D_MODEL)),
            _const_spec((D_MODEL, D_MODEL)),
            _const_spec((D_MODEL, D_MODEL)),
            _const_spec((1, D_MODEL)),
        ],
        out_specs=[pl.BlockSpec((tm, D_MODEL), cblk(0)), pl.BlockSpec((tm, D_MODEL), cblk(0))],
        out_shape=[jax.ShapeDtypeStruct((t, D_MODEL), F32), jax.ShapeDtypeStruct((t, D_MODEL), BF16)],
        compiler_params=_params("arbitrary"),
        name="mix_out",
    )(z, z, z, z, z, z, o, x2, b_gate, conv_w, w_conv_out, w_mla_out, w_o, ln2_g)


def _ffn_kernel(u_ref, u_prev_ref, h_ref, wg_ref, wu_ref, cwg_ref, cwu_ref, bg_ref, bu_ref, wd_ref,
                out_ref, lhs_ref, *, tm, tf, tiles_per_seq):
    @pl.when(pl.program_id(1) == 0)
    def _():
        seq_start = (pl.program_id(0) % tiles_per_seq) == 0
        lhs_ref[0:HALO, :] = jnp.where(seq_start, jnp.zeros_like(u_prev_ref), u_prev_ref[...])
        lhs_ref[HALO:, :] = u_ref[...]
        out_ref[...] = h_ref[...]

    lhs = lhs_ref[...]
    down = None
    for c in range(tf // MXU_DIM):
        cols = slice(c * MXU_DIM, (c + 1) * MXU_DIM)
        a_gate = _causal_conv3(_dot(lhs, wg_ref[:, cols]), cwg_ref[:, cols], tm) + bg_ref[:, cols]
        a_up = _causal_conv3(_dot(lhs, wu_ref[:, cols]), cwu_ref[:, cols], tm) + bu_ref[:, cols]
        act = (a_gate * jax.nn.sigmoid(a_gate) * a_up).astype(BF16)
        part = _dot(act, wd_ref[cols, :])
        down = part if down is None else down + part
    out_ref[...] += down


def _ffn(u2, h1, w_up, conv_w, conv_b, w_down, seq):
    t = u2.shape[0]
    tm, tf = TM_FFN, TF_FFN
    nf = D_FF // tf
    halo_per_tile = tm // HALO
    once = pl.Buffered(1)
    return pl.pallas_call(
        functools.partial(_ffn_kernel, tm=tm, tf=tf, tiles_per_seq=seq // tm),
        grid=(t // tm, nf),
        in_specs=[
            pl.BlockSpec((tm, D_MODEL), lambda i, j: (i, 0), pipeline_mode=once),
            pl.BlockSpec((HALO, D_MODEL), lambda i, j: (jnp.maximum(i * halo_per_tile - 1, 0), 0)),
            pl.BlockSpec((tm, D_MODEL), lambda i, j: (i, 0), pipeline_mode=once),
            pl.BlockSpec((D_MODEL, tf), lambda i, j: (0, j)),
            pl.BlockSpec((D_MODEL, tf), lambda i, j: (0, j + nf)),
            pl.BlockSpec((CONV_WIDTH, tf), lambda i, j: (0, j)),
            pl.BlockSpec((CONV_WIDTH, tf), lambda i, j: (0, j + nf)),
            pl.BlockSpec((1, tf), lambda i, j: (0, j)),
            pl.BlockSpec((1, tf), lambda i, j: (0, j + nf)),
            pl.BlockSpec((tf, D_MODEL), lambda i, j: (j, 0)),
        ],
        out_specs=pl.BlockSpec((tm, D_MODEL), lambda i, j: (i, 0)),
        out_shape=jax.ShapeDtypeStruct((t, D_MODEL), F32),
        scratch_shapes=[pltpu.VMEM((tm + HALO, D_MODEL), BF16)],
        compiler_params=_params("arbitrary", "arbitrary"),
        name="ffn",
    )(u2, u2, h1, w_up, w_up, conv_w, conv_w, conv_b, conv_b, w_down)


def _row(v):
    return v.reshape(1, -1).astype(F32)


def _layer(h2, pos, inv_freq, batch, seq, ln1_g, w_in, b_gate, conv_w, w_conv_out, q_a_g, w_q_b, kv_a_g, w_kv_b,
           q_norm_g, k_norm_g, w_mla_out, w_o, ln2_g, w_ffn_up, ffn_conv_w, ffn_conv_b, w_ffn_down):
    w_lat = w_in[:, LAT_OFFSET:LAT_OFFSET + LAT_WIDTH].astype(BF16)
    w_gate = w_in[:, GATE_OFFSET:].astype(BF16)
    wq3 = w_q_b.reshape(Q_LORA_RANK, N_HEADS, QK_HEAD_DIM)
    w_qn = wq3[:, :, :QK_NOPE_DIM].reshape(Q_LORA_RANK, -1).astype(BF16)
    w_qr = wq3[:, :, QK_NOPE_DIM:].reshape(Q_LORA_RANK, -1).astype(BF16)
    wkv3 = w_kv_b.reshape(KV_LORA_RANK, N_HEADS, QK_NOPE_DIM + V_HEAD_DIM)
    w_kn = wkv3[:, :, :QK_NOPE_DIM].reshape(KV_LORA_RANK, -1).astype(BF16)
    w_v = wkv3[:, :, QK_NOPE_DIM:].reshape(KV_LORA_RANK, -1).astype(BF16)
    twice = lambda g: _row(jnp.concatenate([g, g]))

    q, k, vt = _mla_prep(h2, pos, _row(ln1_g), w_lat, _row(q_a_g), _row(kv_a_g), w_qn, w_qr, w_kn, w_v,
                         _row(q_norm_g[:QK_NOPE_DIM]), twice(q_norm_g[QK_NOPE_DIM:]),
                         _row(k_norm_g[:QK_NOPE_DIM]), twice(k_norm_g[QK_NOPE_DIM:]), inv_freq)
    z = _in_proj(h2, _row(ln1_g), w_gate, w_in)
    o = _attn(q, k, vt, batch, seq)
    h1, u2 = _mix_out(z, o, h2, _row(b_gate), conv_w.astype(F32), w_conv_out.astype(BF16),
                      w_mla_out.astype(BF16), w_o.astype(BF16), _row(ln2_g), seq)
    return _ffn(u2, h1, w_ffn_up.astype(BF16), ffn_conv_w.astype(F32), _row(ffn_conv_b),
                w_ffn_down.astype(BF16), seq)


def kernel(x, positions, ln1_g, w_in, b_gate, conv_w, w_conv_out, q_a_g, w_q_b, kv_a_g, w_kv_b,
           q_norm_g, k_norm_g, w_mla_out, w_o, ln2_g, w_ffn_up, ffn_conv_w, ffn_conv_b, w_ffn_down):
    batch, seq, _ = x.shape
    depth = ln1_g.shape[0]
    h2 = x.reshape(batch * seq, D_MODEL)
    pos = positions.reshape(batch * seq, 1).astype(jnp.int32)
    freq = ROPE_THETA ** (-jnp.arange(0, QK_ROPE_DIM, 2, dtype=F32) / QK_ROPE_DIM)
    inv_freq = jnp.tile(freq, LANES // (QK_ROPE_DIM // 2)).reshape(1, LANES)
    for l in range(depth):
        h2 = _layer(h2, pos, inv_freq, batch, seq, ln1_g[l], w_in[l], b_gate[l], conv_w[l], w_conv_out[l],
                    q_a_g[l], w_q_b[l], kv_a_g[l], w_kv_b[l], q_norm_g[l], k_norm_g[l], w_mla_out[l],
                    w_o[l], ln2_g[l], w_ffn_up[l], ffn_conv_w[l], ffn_conv_b[l], w_ffn_down[l])
    return h2.reshape(batch, seq, D_MODEL)
```

```python
import functools
import math

import jax
import jax.numpy as jnp
from jax import lax
from jax.experimental import pallas as pl
from jax.experimental.pallas import tpu as pltpu

D_MODEL = 2048
CONV_DIM = 1024
CONV_WIDTH = 3
N_HEADS = 16
QK_NOPE_DIM = 128
QK_ROPE_DIM = 64
QK_HEAD_DIM = QK_NOPE_DIM + QK_ROPE_DIM
V_HEAD_DIM = 128
Q_LORA_RANK = 768
KV_LORA_RANK = 512
ROPE_THETA = 10000.0
D_FF = 5632
NORM_EPS = 1e-6
NEG_INF = -1e30

Q_SCALE = QK_HEAD_DIM ** -0.5 * math.log2(math.e)
FIXED_SHIFT_LIMIT = 48.0

LANES = 128
HEAD_SLOT = 2 * LANES
HALO = 16
LAT_OFFSET = 3 * CONV_DIM
LAT_WIDTH = 1536
GATE_OFFSET = LAT_OFFSET + Q_LORA_RANK + KV_LORA_RANK + QK_ROPE_DIM
Z_WIDTH = 2 * D_MODEL + 3 * CONV_DIM
VMEM_LIMIT = 56 * 1024 * 1024

TM_PREP = 256
TM_INPROJ = 1024
TN_INPROJ = 1024
TQ_ATTN = 512
HEADS_PER_STEP = 4
TM_MIX = 256
TM_FFN = 512
TF_FFN = 512

F32 = jnp.float32
BF16 = jnp.bfloat16


def _dot(a, b):
    return jnp.dot(a, b, preferred_element_type=F32)


def _dot_nt(a, b):
    return lax.dot_general(a, b, (((1,), (1,)), ((), ())), preferred_element_type=F32)


def _rms_scale(x, width):
    return lax.rsqrt(jnp.sum(x * x, axis=-1, keepdims=True) * (1.0 / width) + NORM_EPS)


def _params(*sem):
    return pltpu.CompilerParams(dimension_semantics=sem, vmem_limit_bytes=VMEM_LIMIT)


def _const_spec(shape):
    return pl.BlockSpec(shape, lambda *_: (0,) * len(shape), pipeline_mode=pl.Buffered(1))


def _causal_conv3(p, cw, rows):
    full = cw[2:3, :] * p + cw[1:2, :] * pltpu.roll(p, 1, axis=0) + cw[0:1, :] * pltpu.roll(p, 2, axis=0)
    return full[p.shape[0] - rows:, :]


def _mla_prep_kernel(x_ref, pos_ref, ln_g_ref, w_lat_ref, qa_g_ref, kva_g_ref, w_qn_ref, w_qr_ref,
                     w_kn_ref, w_v_ref, qg_nope_ref, qg_rope_ref, kg_nope_ref, kg_rope_ref, inv_freq_ref,
                     q_ref, k_ref, vt_ref, *, q_scale):
    xf = x_ref[...]
    u = (xf * _rms_scale(xf, D_MODEL) * ln_g_ref[...]).astype(BF16)
    lat = _dot_nt(u, w_lat_ref[...])
    q_lat = lat[:, :Q_LORA_RANK]
    kv_lat = lat[:, Q_LORA_RANK:Q_LORA_RANK + KV_LORA_RANK]
    k_rope = lat[:, Q_LORA_RANK + KV_LORA_RANK:Q_LORA_RANK + KV_LORA_RANK + LANES]
    qn = (q_lat * _rms_scale(q_lat, Q_LORA_RANK) * qa_g_ref[...]).astype(BF16)
    kvn = (kv_lat * _rms_scale(kv_lat, KV_LORA_RANK) * kva_g_ref[...]).astype(BF16)
    q_nope_all = _dot(qn, w_qn_ref[...])
    q_rope_all = _dot(qn, w_qr_ref[...])
    k_nope_all = _dot(kvn, w_kn_ref[...])
    v_all = _dot(kvn, w_v_ref[...])

    ang = pos_ref[...].astype(F32) * inv_freq_ref[...]
    cos = jnp.cos(ang)
    sin = jnp.sin(ang)
    lane = lax.broadcasted_iota(jnp.int32, (1, LANES), 1)
    first_half = (lane % QK_ROPE_DIM) < (QK_ROPE_DIM // 2)
    sin_signed = jnp.where(first_half, -sin, sin)

    def rope(t):
        rot = jnp.where(first_half, pltpu.roll(t, LANES - QK_ROPE_DIM // 2, axis=1),
                        pltpu.roll(t, QK_ROPE_DIM // 2, axis=1))
        return t * cos + rot * sin_signed

    k_rope = jnp.where(lane < QK_ROPE_DIM, k_rope, pltpu.roll(k_rope, QK_ROPE_DIM, axis=1))
    qg_nope = qg_nope_ref[...] * q_scale
    qg_rope = qg_rope_ref[...] * q_scale
    kg_nope = kg_nope_ref[...]
    k_tail = rope(k_rope * kg_rope_ref[...])
    k_rope_ss = 0.5 * jnp.sum(k_rope * k_rope, axis=-1, keepdims=True)

    for pair in range(N_HEADS // 2):
        q_pair = q_rope_all[:, pair * LANES:(pair + 1) * LANES]
        q_pair_roped = rope(q_pair * qg_rope)
        for sub in range(2):
            h = 2 * pair + sub
            mine = (lane // QK_ROPE_DIM) == sub
            q_nope = q_nope_all[:, h * QK_NOPE_DIM:(h + 1) * QK_NOPE_DIM]
            ss = (jnp.sum(q_nope * q_nope, axis=-1, keepdims=True)
                  + jnp.sum(jnp.where(mine, q_pair * q_pair, 0.0), axis=-1, keepdims=True))
            r = lax.rsqrt(ss * (1.0 / QK_HEAD_DIM) + NORM_EPS)
            q_ref[:, h * HEAD_SLOT: h * HEAD_SLOT + LANES] = (q_nope * r * qg_nope).astype(BF16)
            q_ref[:, h * HEAD_SLOT + LANES:(h + 1) * HEAD_SLOT] = jnp.where(
                mine, q_pair_roped * r, 0.0).astype(BF16)

            k_nope = k_nope_all[:, h * QK_NOPE_DIM:(h + 1) * QK_NOPE_DIM]
            kss = jnp.sum(k_nope * k_nope, axis=-1, keepdims=True) + k_rope_ss
            kr = lax.rsqrt(kss * (1.0 / QK_HEAD_DIM) + NORM_EPS)
            k_ref[:, h * HEAD_SLOT: h * HEAD_SLOT + LANES] = (k_nope * kr * kg_nope).astype(BF16)
            k_ref[:, h * HEAD_SLOT + LANES:(h + 1) * HEAD_SLOT] = (k_tail * kr).astype(BF16)

    vt_ref[0] = v_all.astype(BF16).T


def _mla_prep(x2, pos, ln_g, w_lat, qa_g, kva_g, w_qn, w_qr, w_kn, w_v, qg_nope, qg_rope, kg_nope, kg_rope,
              inv_freq):
    t = x2.shape[0]
    tm = TM_PREP
    row = lambda i: (i, 0)
    return pl.pallas_call(
        functools.partial(_mla_prep_kernel, q_scale=Q_SCALE),
        grid=(t // tm,),
        in_specs=[
            pl.BlockSpec((tm, D_MODEL), row),
            pl.BlockSpec((tm, 1), row),
            _const_spec((1, D_MODEL)),
            _const_spec((LAT_WIDTH, D_MODEL)),
            _const_spec((1, Q_LORA_RANK)),
            _const_spec((1, KV_LORA_RANK)),
            _const_spec(w_qn.shape), _const_spec(w_qr.shape),
            _const_spec(w_kn.shape), _const_spec(w_v.shape),
            _const_spec((1, LANES)), _const_spec((1, LANES)),
            _const_spec((1, LANES)), _const_spec((1, LANES)),
            _const_spec((1, LANES)),
        ],
        out_specs=[
            pl.BlockSpec((tm, N_HEADS * HEAD_SLOT), row),
            pl.BlockSpec((tm, N_HEADS * HEAD_SLOT), row),
            pl.BlockSpec((1, N_HEADS * V_HEAD_DIM, tm), lambda i: (i, 0, 0)),
        ],
        out_shape=[
            jax.ShapeDtypeStruct((t, N_HEADS * HEAD_SLOT), BF16),
            jax.ShapeDtypeStruct((t, N_HEADS * HEAD_SLOT), BF16),
            jax.ShapeDtypeStruct((t // tm, N_HEADS * V_HEAD_DIM, tm), BF16),
        ],
        compiler_params=_params("arbitrary"),
        name="mla_prep",
    )(x2, pos, ln_g, w_lat, qa_g, kva_g, w_qn, w_qr, w_kn, w_v, qg_nope, qg_rope, kg_nope, kg_rope, inv_freq)


def _in_proj_kernel(x_ref, ln_g_ref, w_gate_ref, w_conv_ref, z_ref, u_ref, *, gate_tiles):
    j = pl.program_id(1)

    @pl.when(j == 0)
    def _():
        xf = x_ref[...]
        u_ref[...] = (xf * _rms_scale(xf, D_MODEL) * ln_g_ref[...]).astype(BF16)

    @pl.when(j < gate_tiles)
    def _():
        z_ref[...] = _dot_nt(u_ref[...], w_gate_ref[...]).astype(BF16)

    @pl.when(j >= gate_tiles)
    def _():
        z_ref[...] = _dot_nt(u_ref[...], w_conv_ref[...]).astype(BF16)


def _in_proj(x2, ln_g, w_gate, w_conv):
    t = x2.shape[0]
    tm, tn = TM_INPROJ, TN_INPROJ
    gate_tiles = 2 * D_MODEL // tn
    return pl.pallas_call(
        functools.partial(_in_proj_kernel, gate_tiles=gate_tiles),
        grid=(t // tm, Z_WIDTH // tn),
        in_specs=[
            pl.BlockSpec((tm, D_MODEL), lambda i, j: (i, 0)),
            pl.BlockSpec((1, D_MODEL), lambda i, j: (0, 0)),
            pl.BlockSpec((tn, D_MODEL), lambda i, j: (jnp.minimum(j, gate_tiles - 1), 0)),
            pl.BlockSpec((tn, D_MODEL), lambda i, j: (jnp.maximum(j - gate_tiles, 0), 0)),
        ],
        out_specs=pl.BlockSpec((tm, tn), lambda i, j: (i, j)),
        out_shape=jax.ShapeDtypeStruct((t, Z_WIDTH), BF16),
        scratch_shapes=[pltpu.VMEM((tm, D_MODEL), BF16)],
        compiler_params=_params("arbitrary", "arbitrary"),
        name="in_proj",
    )(x2, ln_g, w_gate, w_conv)


def _attn_kernel(qg_ref, kg_ref, q_ref, k_ref, vt_ref, o_ref, *, tq, vt_cols, q_scale):
    i = pl.program_id(2)
    sub_blocks = tq // vt_cols
    gq = jnp.max(jnp.abs(qg_ref[...]), axis=-1, keepdims=True)
    gk = jnp.max(jnp.abs(kg_ref[...]), axis=-1, keepdims=True)
    bound = (gq * gk)[0, 0] * (q_scale * QK_HEAD_DIM)
    fixed_shift = bound <= FIXED_SHIFT_LIMIT

    def scores(j):
        start = pl.multiple_of(j * tq, tq)
        out = []
        for hh in range(HEADS_PER_STEP):
            q = q_ref[:, hh * HEAD_SLOT:(hh + 1) * HEAD_SLOT]
            kb = k_ref[pl.ds(start, tq), hh * HEAD_SLOT:(hh + 1) * HEAD_SLOT]
            out.append(_dot_nt(kb, q))
        return out

    def causal(s):
        kv_idx = lax.broadcasted_iota(jnp.int32, (tq, tq), 0)
        q_idx = lax.broadcasted_iota(jnp.int32, (tq, tq), 1)
        return jnp.where(kv_idx <= q_idx, s, NEG_INF)

    def p_times_v(j, hh, p):
        pb = p.astype(BF16)
        pv = None
        for c in range(sub_blocks):
            vt = vt_ref[j * sub_blocks + c, hh * V_HEAD_DIM:(hh + 1) * V_HEAD_DIM, :]
            part = _dot(vt, pb[c * vt_cols:(c + 1) * vt_cols, :])
            pv = part if pv is None else pv + part
        return pv

    def write(hh, l, acc):
        o_ref[:, hh * V_HEAD_DIM:(hh + 1) * V_HEAD_DIM] = (acc / l).T.astype(BF16)

    @pl.when(fixed_shift)
    def _():
        def step(j, carry, masked):
            sc = scores(j)
            out = []
            for hh in range(HEADS_PER_STEP):
                l, acc = carry[hh]
                p = jnp.exp2((causal(sc[hh]) if masked else sc[hh]) - bound)
                out.append((l + jnp.sum(p, axis=0, keepdims=True), acc + p_times_v(j, hh, p)))
            return tuple(out)

        init = tuple((jnp.zeros((1, tq), F32), jnp.zeros((V_HEAD_DIM, tq), F32)) for _ in range(HEADS_PER_STEP))
        carry = step(i, lax.fori_loop(0, i, lambda j, c: step(j, c, False), init), True)
        for hh in range(HEADS_PER_STEP):
            write(hh, *carry[hh])

    @pl.when(jnp.logical_not(fixed_shift))
    def _():
        def step(j, carry, masked):
            sc = scores(j)
            out = []
            for hh in range(HEADS_PER_STEP):
                m, l, acc = carry[hh]
                s = causal(sc[hh]) if masked else sc[hh]
                m_new = jnp.maximum(m, jnp.max(s, axis=0, keepdims=True))
                alpha = jnp.exp2(m - m_new)
                p = jnp.exp2(s - m_new)
                out.append((m_new, alpha * l + jnp.sum(p, axis=0, keepdims=True),
                            alpha * acc + p_times_v(j, hh, p)))
            return tuple(out)

        init = tuple((jnp.full((1, tq), NEG_INF, F32), jnp.zeros((1, tq), F32), jnp.zeros((V_HEAD_DIM, tq), F32))
                     for _ in range(HEADS_PER_STEP))
        carry = step(i, lax.fori_loop(0, i, lambda j, c: step(j, c, False), init), True)
        for hh in range(HEADS_PER_STEP):
            write(hh, *carry[hh][1:])


def _attn(q, k, vt, q_norm_g, k_norm_g, batch, seq):
    tq = TQ_ATTN
    nq = seq // tq
    vt_cols = vt.shape[-1]
    hs = HEADS_PER_STEP
    return pl.pallas_call(
        functools.partial(_attn_kernel, tq=tq, vt_cols=vt_cols, q_scale=Q_SCALE),
        grid=(batch, N_HEADS // hs, nq),
        in_specs=[
            _const_spec((1, QK_HEAD_DIM)),
            _const_spec((1, QK_HEAD_DIM)),
            pl.BlockSpec((tq, hs * HEAD_SLOT), lambda b, h, i: (b * nq + i, h)),
            pl.BlockSpec((seq, hs * HEAD_SLOT), lambda b, h, i: (b, h)),
            pl.BlockSpec((seq // vt_cols, hs * V_HEAD_DIM, vt_cols), lambda b, h, i: (b, h, 0)),
        ],
        out_specs=pl.BlockSpec((tq, hs * V_HEAD_DIM), lambda b, h, i: (b * nq + i, h)),
        out_shape=jax.ShapeDtypeStruct((batch * seq, N_HEADS * V_HEAD_DIM), BF16),
        compiler_params=_params("arbitrary", "arbitrary", "arbitrary"),
        name="attn",
    )(q_norm_g, k_norm_g, q, k, vt)


def _mix_out_kernel(gates_ref, zb_ref, zc_ref, zv_ref, zc_prev_ref, zv_prev_ref, o_ref, x_ref,
                    b_gate_ref, conv_w_ref, w_conv_out_ref, w_mla_out_ref, w_o_ref, ln2_g_ref,
                    h_ref, u2_ref, *, tm, tiles_per_seq):
    seq_start = (pl.program_id(0) % tiles_per_seq) == 0
    prev = zc_prev_ref[...].astype(F32) * zv_prev_ref[...].astype(F32)
    cv = jnp.concatenate([jnp.where(seq_start, 0.0, prev), zc_ref[...].astype(F32) * zv_ref[...].astype(F32)], axis=0)
    conv = _causal_conv3(cv, conv_w_ref[...], tm)
    conv_in = (zb_ref[...].astype(F32) * conv).astype(BF16)
    y_conv = _dot(conv_in, w_conv_out_ref[...])
    y_mla = _dot(o_ref[...], w_mla_out_ref[...])
    gates = jax.nn.sigmoid(gates_ref[...].astype(F32) + b_gate_ref[...])
    mix = (gates[:, :D_MODEL] * y_conv + gates[:, D_MODEL:] * y_mla).astype(BF16)
    h = x_ref[...] + _dot(mix, w_o_ref[...])
    h_ref[...] = h
    u2_ref[...] = (h * _rms_scale(h, D_MODEL) * ln2_g_ref[...]).astype(BF16)


def _mix_out(z, o, x2, b_gate, conv_w, w_conv_out, w_mla_out, w_o, ln2_g, seq):
    t = x2.shape[0]
    tm = TM_MIX
    halo_per_tile = tm // HALO
    cblk = lambda c: (lambda i: (i, c))
    prev = lambda c: (lambda i: (jnp.maximum(i * halo_per_tile - 1, 0), c))
    gate_blocks = 2 * D_MODEL // CONV_DIM
    return pl.pallas_call(
        functools.partial(_mix_out_kernel, tm=tm, tiles_per_seq=seq // tm),
        grid=(t // tm,),
        in_specs=[
            pl.BlockSpec((tm, 2 * D_MODEL), cblk(0)),
            pl.BlockSpec((tm, CONV_DIM), cblk(gate_blocks)),
            pl.BlockSpec((tm, CONV_DIM), cblk(gate_blocks + 1)),
            pl.BlockSpec((tm, CONV_DIM), cblk(gate_blocks + 2)),
            pl.BlockSpec((HALO, CONV_DIM), prev(gate_blocks + 1)),
            pl.BlockSpec((HALO, CONV_DIM), prev(gate_blocks + 2)),
            pl.BlockSpec((tm, D_MODEL), cblk(0)),
            pl.BlockSpec((tm, D_MODEL), cblk(0)),
            _const_spec((1, 2 * D_MODEL)),
            _const_spec((CONV_WIDTH, CONV_DIM)),
            _const_spec((CONV_DIM, D_MODEL)),
            _const_spec((D_MODEL, D_MODEL)),
            _const_spec((D_MODEL, D_MODEL)),
            _const_spec((1, D_MODEL)),
        ],
        out_specs=[pl.BlockSpec((tm, D_MODEL), cblk(0)), pl.BlockSpec((tm, D_MODEL), cblk(0))],
        out_shape=[jax.ShapeDtypeStruct((t, D_MODEL), F32), jax.ShapeDtypeStruct((t, D_MODEL), BF16)],
        compiler_params=_params("arbitrary"),
        name="mix_out",
    )(z, z, z, z, z, z, o, x2, b_gate, conv_w, w_conv_out, w_mla_out, w_o, ln2_g)


def _ffn_kernel(u_ref, u_prev_ref, h_ref, wg_ref, wu_ref, cwg_ref, cwu_ref, bg_ref, bu_ref, wd_ref,
                out_ref, lhs_ref, pg_ref, pu_ref, *, tm, tiles_per_seq):
    @pl.when(pl.program_id(1) == 0)
    def _():
        seq_start = (pl.program_id(0) % tiles_per_seq) == 0
        lhs_ref[0:HALO, :] = jnp.where(seq_start, jnp.zeros_like(u_prev_ref), u_prev_ref[...])
        lhs_ref[HALO:, :] = u_ref[...]
        out_ref[...] = h_ref[...]

    lhs = lhs_ref[...]
    pg_ref[...] = _dot(lhs, wg_ref[...])
    pu_ref[...] = _dot(lhs, wu_ref[...])

    def conv(p_ref, cw_ref, b_ref):
        cw = cw_ref[...]
        return (cw[0:1, :] * p_ref[HALO - 2:HALO - 2 + tm, :]
                + cw[1:2, :] * p_ref[HALO - 1:HALO - 1 + tm, :]
                + cw[2:3, :] * p_ref[HALO:HALO + tm, :]) + b_ref[...]

    a_gate = conv(pg_ref, cwg_ref, bg_ref)
    a_up = conv(pu_ref, cwu_ref, bu_ref)
    act = (a_gate * jax.nn.sigmoid(a_gate) * a_up).astype(BF16)
    out_ref[...] += _dot(act, wd_ref[...])


def _ffn(u2, h1, w_up, conv_w, conv_b, w_down, seq):
    t = u2.shape[0]
    tm, tf = TM_FFN, TF_FFN
    nf = D_FF // tf
    halo_per_tile = tm // HALO
    return pl.pallas_call(
        functools.partial(_ffn_kernel, tm=tm, tiles_per_seq=seq // tm),
        grid=(t // tm, nf),
        in_specs=[
            pl.BlockSpec((tm, D_MODEL), lambda i, j: (i, 0)),
            pl.BlockSpec((HALO, D_MODEL), lambda i, j: (jnp.maximum(i * halo_per_tile - 1, 0), 0)),
            pl.BlockSpec((tm, D_MODEL), lambda i, j: (i, 0)),
            pl.BlockSpec((D_MODEL, tf), lambda i, j: (0, j)),
            pl.BlockSpec((D_MODEL, tf), lambda i, j: (0, j + nf)),
            pl.BlockSpec((CONV_WIDTH, tf), lambda i, j: (0, j)),
            pl.BlockSpec((CONV_WIDTH, tf), lambda i, j: (0, j + nf)),
            pl.BlockSpec((1, tf), lambda i, j: (0, j)),
            pl.BlockSpec((1, tf), lambda i, j: (0, j + nf)),
            pl.BlockSpec((tf, D_MODEL), lambda i, j: (j, 0)),
        ],
        out_specs=pl.BlockSpec((tm, D_MODEL), lambda i, j: (i, 0)),
        out_shape=jax.ShapeDtypeStruct((t, D_MODEL), F32),
        scratch_shapes=[
            pltpu.VMEM((tm + HALO, D_MODEL), BF16),
            pltpu.VMEM((tm + HALO, tf), F32),
            pltpu.VMEM((tm + HALO, tf), F32),
        ],
        compiler_params=_params("arbitrary", "arbitrary"),
        name="ffn",
    )(u2, u2, h1, w_up, w_up, conv_w, conv_w, conv_b, conv_b, w_down)


def _row(v):
    return v.reshape(1, -1).astype(F32)


def _layer(h2, pos, inv_freq, batch, seq, ln1_g, w_in, b_gate, conv_w, w_conv_out, q_a_g, w_q_b, kv_a_g, w_kv_b,
           q_norm_g, k_norm_g, w_mla_out, w_o, ln2_g, w_ffn_up, ffn_conv_w, ffn_conv_b, w_ffn_down):
    w_in_t = w_in.T
    w_conv = w_in_t[:LAT_OFFSET].astype(BF16)
    w_lat = w_in_t[LAT_OFFSET:LAT_OFFSET + LAT_WIDTH].astype(BF16)
    w_gate = w_in_t[GATE_OFFSET:].astype(BF16)
    wq3 = w_q_b.reshape(Q_LORA_RANK, N_HEADS, QK_HEAD_DIM)
    w_qn = wq3[:, :, :QK_NOPE_DIM].reshape(Q_LORA_RANK, -1).astype(BF16)
    w_qr = wq3[:, :, QK_NOPE_DIM:].reshape(Q_LORA_RANK, -1).astype(BF16)
    wkv3 = w_kv_b.reshape(KV_LORA_RANK, N_HEADS, QK_NOPE_DIM + V_HEAD_DIM)
    w_kn = wkv3[:, :, :QK_NOPE_DIM].reshape(KV_LORA_RANK, -1).astype(BF16)
    w_v = wkv3[:, :, QK_NOPE_DIM:].reshape(KV_LORA_RANK, -1).astype(BF16)
    twice = lambda g: _row(jnp.concatenate([g, g]))

    q, k, vt = _mla_prep(h2, pos, _row(ln1_g), w_lat, _row(q_a_g), _row(kv_a_g), w_qn, w_qr, w_kn, w_v,
                         _row(q_norm_g[:QK_NOPE_DIM]), twice(q_norm_g[QK_NOPE_DIM:]),
                         _row(k_norm_g[:QK_NOPE_DIM]), twice(k_norm_g[QK_NOPE_DIM:]), inv_freq)
    z = _in_proj(h2, _row(ln1_g), w_gate, w_conv)
    o = _attn(q, k, vt, _row(q_norm_g), _row(k_norm_g), batch, seq)
    h1, u2 = _mix_out(z, o, h2, _row(b_gate), conv_w.astype(F32), w_conv_out.astype(BF16),
                      w_mla_out.astype(BF16), w_o.astype(BF16), _row(ln2_g), seq)
    return _ffn(u2, h1, w_ffn_up.astype(BF16), ffn_conv_w.astype(F32), _row(ffn_conv_b),
                w_ffn_down.astype(BF16), seq)


def kernel(x, positions, ln1_g, w_in, b_gate, conv_w, w_conv_out, q_a_g, w_q_b, kv_a_g, w_kv_b,
           q_norm_g, k_norm_g, w_mla_out, w_o, ln2_g, w_ffn_up, ffn_conv_w, ffn_conv_b, w_ffn_down):
    batch, seq, _ = x.shape
    depth = ln1_g.shape[0]
    h2 = x.reshape(batch * seq, D_MODEL)
    pos = positions.reshape(batch * seq, 1).astype(jnp.int32)
    freq = ROPE_THETA ** (-jnp.arange(0, QK_ROPE_DIM, 2, dtype=F32) / QK_ROPE_DIM)
    inv_freq = jnp.tile(freq, LANES // (QK_ROPE_DIM // 2)).reshape(1, LANES)
    for l in range(depth):
        h2 = _layer(h2, pos, inv_freq, batch, seq, ln1_g[l], w_in[l], b_gate[l], conv_w[l], w_conv_out[l],
                    q_a_g[l], w_q_b[l], kv_a_g[l], w_kv_b[l], q_norm_g[l], k_norm_g[l], w_mla_out[l],
                    w_o[l], ln2_g[l], w_ffn_up[l], ffn_conv_w[l], ffn_conv_b[l], w_ffn_down[l])
    return h2.reshape(batch, seq, D_MODEL)
```

```python
import functools
import math

import jax
import jax.numpy as jnp
from jax import lax
from jax.experimental import pallas as pl
from jax.experimental.pallas import tpu as pltpu

D_MODEL = 2048
CONV_DIM = 1024
CONV_WIDTH = 3
N_HEADS = 16
QK_NOPE_DIM = 128
QK_ROPE_DIM = 64
QK_HEAD_DIM = QK_NOPE_DIM + QK_ROPE_DIM
V_HEAD_DIM = 128
Q_LORA_RANK = 768
KV_LORA_RANK = 512
ROPE_THETA = 10000.0
D_FF = 5632
NORM_EPS = 1e-6
NEG_INF = -1e30

Q_SCALE = QK_HEAD_DIM ** -0.5 * math.log2(math.e)
FIXED_SHIFT_LIMIT = 48.0

LANES = 128
HEAD_SLOT = 2 * LANES
HALO = 16
LAT_OFFSET = 3 * CONV_DIM
LAT_WIDTH = 1536
GATE_OFFSET = LAT_OFFSET + Q_LORA_RANK + KV_LORA_RANK + QK_ROPE_DIM
Z_WIDTH = 2 * D_MODEL + 3 * CONV_DIM
VMEM_LIMIT = 56 * 1024 * 1024

TM_PREP = 256
TM_INPROJ = 2048
TN_INPROJ = 1024
TQ_ATTN = 512
HEADS_PER_STEP = 4
SCORES_AHEAD = 4
TM_MIX = 256
TM_FFN = 512
TF_FFN = 512
FFN_ROW_CHUNK = 256

F32 = jnp.float32
BF16 = jnp.bfloat16


def _dot(a, b):
    return jnp.dot(a, b, preferred_element_type=F32)


def _dot_nt(a, b):
    return lax.dot_general(a, b, (((1,), (1,)), ((), ())), preferred_element_type=F32)


def _rms_scale(x, width):
    return lax.rsqrt(jnp.sum(x * x, axis=-1, keepdims=True) * (1.0 / width) + NORM_EPS)


def _params(*sem):
    return pltpu.CompilerParams(dimension_semantics=sem, vmem_limit_bytes=VMEM_LIMIT)


def _const_spec(shape):
    return pl.BlockSpec(shape, lambda *_: (0,) * len(shape), pipeline_mode=pl.Buffered(1))


def _causal_conv3(p, cw, rows):
    full = cw[2:3, :] * p + cw[1:2, :] * pltpu.roll(p, 1, axis=0) + cw[0:1, :] * pltpu.roll(p, 2, axis=0)
    return full[p.shape[0] - rows:, :]


def _mla_prep_kernel(x_ref, pos_ref, ln_g_ref, w_lat_ref, qa_g_ref, kva_g_ref, w_qn_ref, w_qr_ref,
                     w_kn_ref, w_v_ref, qg_nope_ref, qg_rope_ref, kg_nope_ref, kg_rope_ref, inv_freq_ref,
                     q_ref, k_ref, vt_ref, u_ref, *, q_scale):
    xf = x_ref[...]
    u = (xf * _rms_scale(xf, D_MODEL) * ln_g_ref[...]).astype(BF16)
    u_ref[...] = u
    lat = _dot_nt(u, w_lat_ref[...])
    q_lat = lat[:, :Q_LORA_RANK]
    kv_lat = lat[:, Q_LORA_RANK:Q_LORA_RANK + KV_LORA_RANK]
    k_rope = lat[:, Q_LORA_RANK + KV_LORA_RANK:Q_LORA_RANK + KV_LORA_RANK + LANES]
    qn = (q_lat * _rms_scale(q_lat, Q_LORA_RANK) * qa_g_ref[...]).astype(BF16)
    kvn = (kv_lat * _rms_scale(kv_lat, KV_LORA_RANK) * kva_g_ref[...]).astype(BF16)
    q_nope_all = _dot(qn, w_qn_ref[...])
    q_rope_all = _dot(qn, w_qr_ref[...])
    k_nope_all = _dot(kvn, w_kn_ref[...])
    v_all = _dot(kvn, w_v_ref[...])

    ang = pos_ref[...].astype(F32) * inv_freq_ref[...]
    cos = jnp.cos(ang)
    sin = jnp.sin(ang)
    lane = lax.broadcasted_iota(jnp.int32, (1, LANES), 1)
    first_half = (lane % QK_ROPE_DIM) < (QK_ROPE_DIM // 2)
    sin_signed = jnp.where(first_half, -sin, sin)

    def rope(t):
        rot = jnp.where(first_half, pltpu.roll(t, LANES - QK_ROPE_DIM // 2, axis=1),
                        pltpu.roll(t, QK_ROPE_DIM // 2, axis=1))
        return t * cos + rot * sin_signed

    k_rope = jnp.where(lane < QK_ROPE_DIM, k_rope, pltpu.roll(k_rope, QK_ROPE_DIM, axis=1))
    qg_nope = qg_nope_ref[...] * q_scale
    qg_rope = qg_rope_ref[...] * q_scale
    kg_nope = kg_nope_ref[...]
    k_tail = rope(k_rope * kg_rope_ref[...])
    k_rope_ss = 0.5 * jnp.sum(k_rope * k_rope, axis=-1, keepdims=True)

    for pair in range(N_HEADS // 2):
        q_pair = q_rope_all[:, pair * LANES:(pair + 1) * LANES]
        q_pair_roped = rope(q_pair * qg_rope)
        for sub in range(2):
            h = 2 * pair + sub
            mine = (lane // QK_ROPE_DIM) == sub
            q_nope = q_nope_all[:, h * QK_NOPE_DIM:(h + 1) * QK_NOPE_DIM]
            ss = (jnp.sum(q_nope * q_nope, axis=-1, keepdims=True)
                  + jnp.sum(jnp.where(mine, q_pair * q_pair, 0.0), axis=-1, keepdims=True))
            r = lax.rsqrt(ss * (1.0 / QK_HEAD_DIM) + NORM_EPS)
            q_ref[:, h * HEAD_SLOT: h * HEAD_SLOT + LANES] = (q_nope * r * qg_nope).astype(BF16)
            q_ref[:, h * HEAD_SLOT + LANES:(h + 1) * HEAD_SLOT] = jnp.where(
                mine, q_pair_roped * r, 0.0).astype(BF16)

            k_nope = k_nope_all[:, h * QK_NOPE_DIM:(h + 1) * QK_NOPE_DIM]
            kss = jnp.sum(k_nope * k_nope, axis=-1, keepdims=True) + k_rope_ss
            kr = lax.rsqrt(kss * (1.0 / QK_HEAD_DIM) + NORM_EPS)
            k_ref[:, h * HEAD_SLOT: h * HEAD_SLOT + LANES] = (k_nope * kr * kg_nope).astype(BF16)
            k_ref[:, h * HEAD_SLOT + LANES:(h + 1) * HEAD_SLOT] = (k_tail * kr).astype(BF16)

    vt_ref[0] = v_all.astype(BF16).T


def _mla_prep(x2, pos, ln_g, w_lat, qa_g, kva_g, w_qn, w_qr, w_kn, w_v, qg_nope, qg_rope, kg_nope, kg_rope,
              inv_freq):
    t = x2.shape[0]
    tm = TM_PREP
    row = lambda i: (i, 0)
    return pl.pallas_call(
        functools.partial(_mla_prep_kernel, q_scale=Q_SCALE),
        grid=(t // tm,),
        in_specs=[
            pl.BlockSpec((tm, D_MODEL), row),
            pl.BlockSpec((tm, 1), row),
            _const_spec((1, D_MODEL)),
            pl.BlockSpec((pl.Element(LAT_WIDTH), pl.Element(D_MODEL)), lambda i: (LAT_OFFSET, 0),
                         pipeline_mode=pl.Buffered(1)),
            _const_spec((1, Q_LORA_RANK)),
            _const_spec((1, KV_LORA_RANK)),
            _const_spec(w_qn.shape), _const_spec(w_qr.shape),
            _const_spec(w_kn.shape), _const_spec(w_v.shape),
            _const_spec((1, LANES)), _const_spec((1, LANES)),
            _const_spec((1, LANES)), _const_spec((1, LANES)),
            _const_spec((1, LANES)),
        ],
        out_specs=[
            pl.BlockSpec((tm, N_HEADS * HEAD_SLOT), row),
            pl.BlockSpec((tm, N_HEADS * HEAD_SLOT), row),
            pl.BlockSpec((1, N_HEADS * V_HEAD_DIM, tm), lambda i: (i, 0, 0)),
            pl.BlockSpec((tm, D_MODEL), row),
        ],
        out_shape=[
            jax.ShapeDtypeStruct((t, N_HEADS * HEAD_SLOT), BF16),
            jax.ShapeDtypeStruct((t, N_HEADS * HEAD_SLOT), BF16),
            jax.ShapeDtypeStruct((t // tm, N_HEADS * V_HEAD_DIM, tm), BF16),
            jax.ShapeDtypeStruct((t, D_MODEL), BF16),
        ],
        compiler_params=_params("arbitrary"),
        name="mla_prep",
    )(x2, pos, ln_g, w_lat, qa_g, kva_g, w_qn, w_qr, w_kn, w_v, qg_nope, qg_rope, kg_nope, kg_rope, inv_freq)


def _in_proj_kernel(u_ref, w_ref, z_ref):
    z_ref[...] = _dot_nt(u_ref[...], w_ref[...]).astype(BF16)


def _in_proj(u, w_in_t):
    t = u.shape[0]
    tm, tn = TM_INPROJ, TN_INPROJ
    gate_tiles = 2 * D_MODEL // tn

    def w_rows(i, j):
        rows = jnp.where(j < gate_tiles, GATE_OFFSET + j * tn, (j - gate_tiles) * tn)
        return pl.multiple_of(rows, math.gcd(GATE_OFFSET, tn)), 0

    return pl.pallas_call(
        _in_proj_kernel,
        grid=(t // tm, Z_WIDTH // tn),
        in_specs=[
            pl.BlockSpec((tm, D_MODEL), lambda i, j: (i, 0)),
            pl.BlockSpec((pl.Element(tn), pl.Element(D_MODEL)), w_rows),
        ],
        out_specs=pl.BlockSpec((tm, tn), lambda i, j: (i, j)),
        out_shape=jax.ShapeDtypeStruct((t, Z_WIDTH), BF16),
        compiler_params=_params("arbitrary", "arbitrary"),
        name="in_proj",
    )(u, w_in_t)


def _attn_kernel(qg_ref, kg_ref, q_ref, k_ref, vt_ref, o_ref, *, tq, vt_cols, q_scale):
    i = pl.program_id(2)
    sub_blocks = tq // vt_cols
    gq = jnp.max(jnp.abs(qg_ref[...]), axis=-1, keepdims=True)
    gk = jnp.max(jnp.abs(kg_ref[...]), axis=-1, keepdims=True)
    bound = (gq * gk)[0, 0] * (q_scale * QK_HEAD_DIM)
    fixed_shift = bound <= FIXED_SHIFT_LIMIT

    def score(j, hh):
        start = pl.multiple_of(j * tq, tq)
        q = q_ref[:, hh * HEAD_SLOT:(hh + 1) * HEAD_SLOT]
        kb = k_ref[pl.ds(start, tq), hh * HEAD_SLOT:(hh + 1) * HEAD_SLOT]
        return _dot_nt(kb, q)

    def per_head(j, fn):
        sc = {hh: score(j, hh) for hh in range(SCORES_AHEAD)}
        out = []
        for hh in range(HEADS_PER_STEP):
            out.append(fn(hh, sc.pop(hh)))
            if hh + SCORES_AHEAD < HEADS_PER_STEP:
                sc[hh + SCORES_AHEAD] = score(j, hh + SCORES_AHEAD)
        return tuple(out)

    def causal(s):
        kv_idx = lax.broadcasted_iota(jnp.int32, (tq, tq), 0)
        q_idx = lax.broadcasted_iota(jnp.int32, (tq, tq), 1)
        return jnp.where(kv_idx <= q_idx, s, NEG_INF)

    def p_times_v(j, hh, p):
        pb = p.astype(BF16)
        pv = None
        for c in range(sub_blocks):
            vt = vt_ref[j * sub_blocks + c, hh * V_HEAD_DIM:(hh + 1) * V_HEAD_DIM, :]
            part = _dot(vt, pb[c * vt_cols:(c + 1) * vt_cols, :])
            pv = part if pv is None else pv + part
        return pv

    def write(hh, l, acc):
        o_ref[:, hh * V_HEAD_DIM:(hh + 1) * V_HEAD_DIM] = (acc / l).T.astype(BF16)

    @pl.when(fixed_shift)
    def _():
        def step(j, carry, masked):
            def head(hh, s):
                l, acc = carry[hh]
                p = jnp.exp2((causal(s) if masked else s) - bound)
                return l + jnp.sum(p, axis=0, keepdims=True), acc + p_times_v(j, hh, p)

            return per_head(j, head)

        init = tuple((jnp.zeros((1, tq), F32), jnp.zeros((V_HEAD_DIM, tq), F32)) for _ in range(HEADS_PER_STEP))
        carry = step(i, lax.fori_loop(0, i, lambda j, c: step(j, c, False), init), True)
        for hh in range(HEADS_PER_STEP):
            write(hh, *carry[hh])

    @pl.when(jnp.logical_not(fixed_shift))
    def _():
        def step(j, carry, masked):
            def head(hh, s):
                m, l, acc = carry[hh]
                s = causal(s) if masked else s
                m_new = jnp.maximum(m, jnp.max(s, axis=0, keepdims=True))
                alpha = jnp.exp2(m - m_new)
                p = jnp.exp2(s - m_new)
                return m_new, alpha * l + jnp.sum(p, axis=0, keepdims=True), alpha * acc + p_times_v(j, hh, p)

            return per_head(j, head)

        init = tuple((jnp.full((1, tq), NEG_INF, F32), jnp.zeros((1, tq), F32), jnp.zeros((V_HEAD_DIM, tq), F32))
                     for _ in range(HEADS_PER_STEP))
        carry = step(i, lax.fori_loop(0, i, lambda j, c: step(j, c, False), init), True)
        for hh in range(HEADS_PER_STEP):
            write(hh, *carry[hh][1:])


def _attn(q, k, vt, q_norm_g, k_norm_g, batch, seq):
    tq = TQ_ATTN
    nq = seq // tq
    vt_cols = vt.shape[-1]
    hs = HEADS_PER_STEP
    return pl.pallas_call(
        functools.partial(_attn_kernel, tq=tq, vt_cols=vt_cols, q_scale=Q_SCALE),
        grid=(batch, N_HEADS // hs, nq),
        in_specs=[
            _const_spec((1, QK_HEAD_DIM)),
            _const_spec((1, QK_HEAD_DIM)),
            pl.BlockSpec((tq, hs * HEAD_SLOT), lambda b, h, i: (b * nq + i, h)),
            pl.BlockSpec((seq, hs * HEAD_SLOT), lambda b, h, i: (b, h)),
            pl.BlockSpec((seq // vt_cols, hs * V_HEAD_DIM, vt_cols), lambda b, h, i: (b, h, 0)),
        ],
        out_specs=pl.BlockSpec((tq, hs * V_HEAD_DIM), lambda b, h, i: (b * nq + i, h)),
        out_shape=jax.ShapeDtypeStruct((batch * seq, N_HEADS * V_HEAD_DIM), BF16),
        compiler_params=_params("arbitrary", "arbitrary", "arbitrary"),
        name="attn",
    )(q_norm_g, k_norm_g, q, k, vt)


def _mix_out_kernel(gates_ref, zb_ref, zc_ref, zv_ref, zc_prev_ref, zv_prev_ref, o_ref, x_ref,
                    b_gate_ref, conv_w_ref, w_conv_out_ref, w_mla_out_ref, w_o_ref, ln2_g_ref,
                    h_ref, u2_ref, *, tm, tiles_per_seq):
    seq_start = (pl.program_id(0) % tiles_per_seq) == 0
    prev = zc_prev_ref[...].astype(F32) * zv_prev_ref[...].astype(F32)
    cv = jnp.concatenate([jnp.where(seq_start, 0.0, prev), zc_ref[...].astype(F32) * zv_ref[...].astype(F32)], axis=0)
    conv = _causal_conv3(cv, conv_w_ref[...], tm)
    conv_in = (zb_ref[...].astype(F32) * conv).astype(BF16)
    y_conv = _dot(conv_in, w_conv_out_ref[...])
    y_mla = _dot(o_ref[...], w_mla_out_ref[...])
    gates = jax.nn.sigmoid(gates_ref[...].astype(F32) + b_gate_ref[...])
    mix = (gates[:, :D_MODEL] * y_conv + gates[:, D_MODEL:] * y_mla).astype(BF16)
    h = x_ref[...] + _dot(mix, w_o_ref[...])
    h_ref[...] = h
    u2_ref[...] = (h * _rms_scale(h, D_MODEL) * ln2_g_ref[...]).astype(BF16)


def _mix_out(z, o, x2, b_gate, conv_w, w_conv_out, w_mla_out, w_o, ln2_g, seq):
    t = x2.shape[0]
    tm = TM_MIX
    halo_per_tile = tm // HALO
    cblk = lambda c: (lambda i: (i, c))
    prev = lambda c: (lambda i: (jnp.maximum(i * halo_per_tile - 1, 0), c))
    gate_blocks = 2 * D_MODEL // CONV_DIM
    return pl.pallas_call(
        functools.partial(_mix_out_kernel, tm=tm, tiles_per_seq=seq // tm),
        grid=(t // tm,),
        in_specs=[
            pl.BlockSpec((tm, 2 * D_MODEL), cblk(0)),
            pl.BlockSpec((tm, CONV_DIM), cblk(gate_blocks)),
            pl.BlockSpec((tm, CONV_DIM), cblk(gate_blocks + 1)),
            pl.BlockSpec((tm, CONV_DIM), cblk(gate_blocks + 2)),
            pl.BlockSpec((HALO, CONV_DIM), prev(gate_blocks + 1)),
            pl.BlockSpec((HALO, CONV_DIM), prev(gate_blocks + 2)),
            pl.BlockSpec((tm, D_MODEL), cblk(0)),
            pl.BlockSpec((tm, D_MODEL), cblk(0)),
            _const_spec((1, 2 * D_MODEL)),
            _const_spec((CONV_WIDTH, CONV_DIM)),
            _const_spec((CONV_DIM, D_MODEL)),
            _const_spec((D_MODEL, D_MODEL)),
            _const_spec((D_MODEL, D_MODEL)),
            _const_spec((1, D_MODEL)),
        ],
        out_specs=[pl.BlockSpec((tm, D_MODEL), cblk(0)), pl.BlockSpec((tm, D_MODEL), cblk(0))],
        out_shape=[jax.ShapeDtypeStruct((t, D_MODEL), F32), jax.ShapeDtypeStruct((t, D_MODEL), BF16)],
        compiler_params=_params("arbitrary"),
        name="mix_out",
    )(z, z, z, z, z, z, o, x2, b_gate, conv_w, w_conv_out, w_mla_out, w_o, ln2_g)


def _ffn_kernel(u_ref, u_prev_ref, h_ref, wg_ref, wu_ref, cwg_ref, cwu_ref, bg_ref, bu_ref, wd_ref,
                out_ref, lhs_ref, pg_ref, pu_ref, *, tm, row_chunk, tiles_per_seq):
    @pl.when(pl.program_id(1) == 0)
    def _():
        seq_start = (pl.program_id(0) % tiles_per_seq) == 0
        lhs_ref[0:HALO, :] = jnp.where(seq_start, jnp.zeros_like(u_prev_ref), u_prev_ref[...])
        lhs_ref[HALO:, :] = u_ref[...]
        out_ref[...] = h_ref[...]

    bounds = [0] + [HALO + (c + 1) * row_chunk for c in range(tm // row_chunk)]
    for lo, hi in zip(bounds[:-1], bounds[1:]):
        pg_ref[lo:hi, :] = _dot(lhs_ref[lo:hi, :], wg_ref[...])
        pu_ref[lo:hi, :] = _dot(lhs_ref[lo:hi, :], wu_ref[...])

    def conv(p_ref, cw_ref, b_ref, r0):
        cw = cw_ref[...]
        return (cw[0:1, :] * p_ref[r0 + HALO - 2:r0 + HALO - 2 + row_chunk, :]
                + cw[1:2, :] * p_ref[r0 + HALO - 1:r0 + HALO - 1 + row_chunk, :]
                + cw[2:3, :] * p_ref[r0 + HALO:r0 + HALO + row_chunk, :]) + b_ref[...]

    for r0 in range(0, tm, row_chunk):
        a_gate = conv(pg_ref, cwg_ref, bg_ref, r0)
        a_up = conv(pu_ref, cwu_ref, bu_ref, r0)
        act = (a_gate * jax.nn.sigmoid(a_gate) * a_up).astype(BF16)
        out_ref[r0:r0 + row_chunk, :] += _dot(act, wd_ref[...])


def _ffn(u2, h1, w_up, conv_w, conv_b, w_down, seq):
    t = u2.shape[0]
    tm, tf = TM_FFN, TF_FFN
    nf = D_FF // tf
    halo_per_tile = tm // HALO
    return pl.pallas_call(
        functools.partial(_ffn_kernel, tm=tm, row_chunk=FFN_ROW_CHUNK, tiles_per_seq=seq // tm),
        grid=(t // tm, nf),
        in_specs=[
            pl.BlockSpec((tm, D_MODEL), lambda i, j: (i, 0)),
            pl.BlockSpec((HALO, D_MODEL), lambda i, j: (jnp.maximum(i * halo_per_tile - 1, 0), 0)),
            pl.BlockSpec((tm, D_MODEL), lambda i, j: (i, 0)),
            pl.BlockSpec((D_MODEL, tf), lambda i, j: (0, j)),
            pl.BlockSpec((D_MODEL, tf), lambda i, j: (0, j + nf)),
            pl.BlockSpec((CONV_WIDTH, tf), lambda i, j: (0, j)),
            pl.BlockSpec((CONV_WIDTH, tf), lambda i, j: (0, j + nf)),
            pl.BlockSpec((1, tf), lambda i, j: (0, j)),
            pl.BlockSpec((1, tf), lambda i, j: (0, j + nf)),
            pl.BlockSpec((tf, D_MODEL), lambda i, j: (j, 0)),
        ],
        out_specs=pl.BlockSpec((tm, D_MODEL), lambda i, j: (i, 0)),
        out_shape=jax.ShapeDtypeStruct((t, D_MODEL), F32),
        scratch_shapes=[
            pltpu.VMEM((tm + HALO, D_MODEL), BF16),
            pltpu.VMEM((tm + HALO, tf), F32),
            pltpu.VMEM((tm + HALO, tf), F32),
        ],
        compiler_params=_params("arbitrary", "arbitrary"),
        name="ffn",
    )(u2, u2, h1, w_up, w_up, conv_w, conv_w, conv_b, conv_b, w_down)


def _row(v):
    return v.reshape(1, -1).astype(F32)


def _layer(h2, pos, inv_freq, batch, seq, ln1_g, w_in, b_gate, conv_w, w_conv_out, q_a_g, w_q_b, kv_a_g, w_kv_b,
           q_norm_g, k_norm_g, w_mla_out, w_o, ln2_g, w_ffn_up, ffn_conv_w, ffn_conv_b, w_ffn_down):
    w_in_t = w_in.T.astype(BF16)
    wq3 = w_q_b.reshape(Q_LORA_RANK, N_HEADS, QK_HEAD_DIM)
    w_qn = wq3[:, :, :QK_NOPE_DIM].reshape(Q_LORA_RANK, -1).astype(BF16)
    w_qr = wq3[:, :, QK_NOPE_DIM:].reshape(Q_LORA_RANK, -1).astype(BF16)
    wkv3 = w_kv_b.reshape(KV_LORA_RANK, N_HEADS, QK_NOPE_DIM + V_HEAD_DIM)
    w_kn = wkv3[:, :, :QK_NOPE_DIM].reshape(KV_LORA_RANK, -1).astype(BF16)
    w_v = wkv3[:, :, QK_NOPE_DIM:].reshape(KV_LORA_RANK, -1).astype(BF16)
    twice = lambda g: _row(jnp.concatenate([g, g]))

    q, k, vt, u = _mla_prep(h2, pos, _row(ln1_g), w_in_t, _row(q_a_g), _row(kv_a_g), w_qn, w_qr, w_kn, w_v,
                            _row(q_norm_g[:QK_NOPE_DIM]), twice(q_norm_g[QK_NOPE_DIM:]),
                            _row(k_norm_g[:QK_NOPE_DIM]), twice(k_norm_g[QK_NOPE_DIM:]), inv_freq)
    z = _in_proj(u, w_in_t)
    o = _attn(q, k, vt, _row(q_norm_g), _row(k_norm_g), batch, seq)
    h1, u2 = _mix_out(z, o, h2, _row(b_gate), conv_w.astype(F32), w_conv_out.astype(BF16),
                      w_mla_out.astype(BF16), w_o.astype(BF16), _row(ln2_g), seq)
    return _ffn(u2, h1, w_ffn_up.astype(BF16), ffn_conv_w.astype(F32), _row(ffn_conv_b),
                w_ffn_down.astype(BF16), seq)


def kernel(x, positions, ln1_g, w_in, b_gate, conv_w, w_conv_out, q_a_g, w_q_b, kv_a_g, w_kv_b,
           q_norm_g, k_norm_g, w_mla_out, w_o, ln2_g, w_ffn_up, ffn_conv_w, ffn_conv_b, w_ffn_down):
    batch, seq, _ = x.shape
    depth = ln1_g.shape[0]
    h2 = x.reshape(batch * seq, D_MODEL)
    pos = positions.reshape(batch * seq, 1).astype(jnp.int32)
    freq = ROPE_THETA ** (-jnp.arange(0, QK_ROPE_DIM, 2, dtype=F32) / QK_ROPE_DIM)
    inv_freq = jnp.tile(freq, LANES // (QK_ROPE_DIM // 2)).reshape(1, LANES)
    for l in range(depth):
        h2 = _layer(h2, pos, inv_freq, batch, seq, ln1_g[l], w_in[l], b_gate[l], conv_w[l], w_conv_out[l],
                    q_a_g[l], w_q_b[l], kv_a_g[l], w_kv_b[l], q_norm_g[l], k_norm_g[l], w_mla_out[l],
                    w_o[l], ln2_g[l], w_ffn_up[l], ffn_conv_w[l], ffn_conv_b[l], w_ffn_down[l])
    return h2.reshape(batch, seq, D_MODEL)
```

```python
import functools
import math

import jax
import jax.numpy as jnp
from jax import lax
from jax.experimental import pallas as pl
from jax.experimental.pallas import tpu as pltpu

D_MODEL = 2048
CONV_DIM = 1024
CONV_WIDTH = 3
N_HEADS = 16
QK_NOPE_DIM = 128
QK_ROPE_DIM = 64
QK_HEAD_DIM = QK_NOPE_DIM + QK_ROPE_DIM
V_HEAD_DIM = 128
Q_LORA_RANK = 768
KV_LORA_RANK = 512
ROPE_THETA = 10000.0
D_FF = 5632
NORM_EPS = 1e-6
NEG_INF = -1e30

Q_SCALE = QK_HEAD_DIM ** -0.5 * math.log2(math.e)
FIXED_SHIFT_LIMIT = 48.0

LANES = 128
HEAD_SLOT = 2 * LANES
HALO = 16
LAT_OFFSET = 3 * CONV_DIM
LAT_WIDTH = 1536
GATE_OFFSET = LAT_OFFSET + Q_LORA_RANK + KV_LORA_RANK + QK_ROPE_DIM
Z_WIDTH = 2 * D_MODEL + 3 * CONV_DIM
VMEM_LIMIT = 56 * 1024 * 1024

TM_PREP = 256
TM_INPROJ = 2048
TN_INPROJ = 1024
TQ_ATTN = 512
HEADS_PER_STEP = 4
SCORES_AHEAD = 4
TM_MIX = 256
TM_FFN = 512
TF_FFN = 512
FFN_ROW_CHUNK = 256

F32 = jnp.float32
BF16 = jnp.bfloat16


def _dot(a, b):
    return jnp.dot(a, b, preferred_element_type=F32)


def _dot_nt(a, b):
    return lax.dot_general(a, b, (((1,), (1,)), ((), ())), preferred_element_type=F32)


def _rms_scale(x, width):
    return lax.rsqrt(jnp.sum(x * x, axis=-1, keepdims=True) * (1.0 / width) + NORM_EPS)


def _params(*sem):
    return pltpu.CompilerParams(dimension_semantics=sem, vmem_limit_bytes=VMEM_LIMIT)


def _const_spec(shape):
    return pl.BlockSpec(shape, lambda *_: (0,) * len(shape), pipeline_mode=pl.Buffered(1))


def _causal_conv3(p, cw, rows):
    full = cw[2:3, :] * p + cw[1:2, :] * pltpu.roll(p, 1, axis=0) + cw[0:1, :] * pltpu.roll(p, 2, axis=0)
    return full[p.shape[0] - rows:, :]


def _mla_prep_kernel(x_ref, pos_ref, ln_g_ref, w_lat_ref, qa_g_ref, kva_g_ref, w_qn_ref, w_qr_ref,
                     w_kn_ref, w_v_ref, qg_nope_ref, qg_rope_ref, kg_nope_ref, kg_rope_ref, inv_freq_ref,
                     q_ref, k_ref, vt_ref, u_ref, qn_s, qr_s, kn_s, v_s, kr_s, *, q_scale):
    @pl.when(pl.program_id(0) == 0)
    def _():
        for s in (qn_s, qr_s, kn_s, v_s, kr_s):
            s[...] = jnp.zeros_like(s)

    ang = pos_ref[...].astype(F32) * inv_freq_ref[...]
    cos = jnp.cos(ang)
    sin = jnp.sin(ang)
    lane = lax.broadcasted_iota(jnp.int32, (1, LANES), 1)
    first_half = (lane % QK_ROPE_DIM) < (QK_ROPE_DIM // 2)
    sin_signed = jnp.where(first_half, -sin, sin)

    def rope(t):
        rot = jnp.where(first_half, pltpu.roll(t, LANES - QK_ROPE_DIM // 2, axis=1),
                        pltpu.roll(t, QK_ROPE_DIM // 2, axis=1))
        return t * cos + rot * sin_signed

    k_rope = kr_s[...]
    k_rope = jnp.where(lane < QK_ROPE_DIM, k_rope, pltpu.roll(k_rope, QK_ROPE_DIM, axis=1))
    qg_nope = qg_nope_ref[...] * q_scale
    qg_rope = qg_rope_ref[...] * q_scale
    kg_nope = kg_nope_ref[...]
    k_tail = rope(k_rope * kg_rope_ref[...])
    k_rope_ss = 0.5 * jnp.sum(k_rope * k_rope, axis=-1, keepdims=True)

    for pair in range(N_HEADS // 2):
        q_pair = qr_s[:, pair * LANES:(pair + 1) * LANES]
        q_pair_roped = rope(q_pair * qg_rope)
        for sub in range(2):
            h = 2 * pair + sub
            mine = (lane // QK_ROPE_DIM) == sub
            q_nope = qn_s[:, h * QK_NOPE_DIM:(h + 1) * QK_NOPE_DIM]
            ss = (jnp.sum(q_nope * q_nope, axis=-1, keepdims=True)
                  + jnp.sum(jnp.where(mine, q_pair * q_pair, 0.0), axis=-1, keepdims=True))
            r = lax.rsqrt(ss * (1.0 / QK_HEAD_DIM) + NORM_EPS)
            q_ref[:, h * HEAD_SLOT: h * HEAD_SLOT + LANES] = (q_nope * r * qg_nope).astype(BF16)
            q_ref[:, h * HEAD_SLOT + LANES:(h + 1) * HEAD_SLOT] = jnp.where(
                mine, q_pair_roped * r, 0.0).astype(BF16)

    for h in range(N_HEADS):
        k_nope = kn_s[:, h * QK_NOPE_DIM:(h + 1) * QK_NOPE_DIM]
        kss = jnp.sum(k_nope * k_nope, axis=-1, keepdims=True) + k_rope_ss
        kr = lax.rsqrt(kss * (1.0 / QK_HEAD_DIM) + NORM_EPS)
        k_ref[:, h * HEAD_SLOT: h * HEAD_SLOT + LANES] = (k_nope * kr * kg_nope).astype(BF16)
        k_ref[:, h * HEAD_SLOT + LANES:(h + 1) * HEAD_SLOT] = (k_tail * kr).astype(BF16)

    vt_ref[0] = v_s[...].T

    xf = x_ref[...]
    u = (xf * _rms_scale(xf, D_MODEL) * ln_g_ref[...]).astype(BF16)
    u_ref[...] = u
    lat = _dot_nt(u, w_lat_ref[...])
    q_lat = lat[:, :Q_LORA_RANK]
    kv_lat = lat[:, Q_LORA_RANK:Q_LORA_RANK + KV_LORA_RANK]
    kr_s[...] = lat[:, Q_LORA_RANK + KV_LORA_RANK:Q_LORA_RANK + KV_LORA_RANK + LANES]
    qn = (q_lat * _rms_scale(q_lat, Q_LORA_RANK) * qa_g_ref[...]).astype(BF16)
    kvn = (kv_lat * _rms_scale(kv_lat, KV_LORA_RANK) * kva_g_ref[...]).astype(BF16)
    qn_s[...] = _dot(qn, w_qn_ref[...])
    qr_s[...] = _dot(qn, w_qr_ref[...])
    kn_s[...] = _dot(kvn, w_kn_ref[...])
    v_s[...] = _dot(kvn, w_v_ref[...]).astype(BF16)


def _mla_prep(x2, pos, ln_g, w_lat, qa_g, kva_g, w_qn, w_qr, w_kn, w_v, qg_nope, qg_rope, kg_nope, kg_rope,
              inv_freq):
    t = x2.shape[0]
    tm = TM_PREP
    n = t // tm
    row = lambda i: (jnp.minimum(i, n - 1), 0)
    prev = lambda i: (jnp.maximum(i - 1, 0), 0)
    return pl.pallas_call(
        functools.partial(_mla_prep_kernel, q_scale=Q_SCALE),
        grid=(n + 1,),
        in_specs=[
            pl.BlockSpec((tm, D_MODEL), row),
            pl.BlockSpec((tm, 1), prev),
            _const_spec((1, D_MODEL)),
            pl.BlockSpec((pl.Element(LAT_WIDTH), pl.Element(D_MODEL)), lambda i: (LAT_OFFSET, 0),
                         pipeline_mode=pl.Buffered(1)),
            _const_spec((1, Q_LORA_RANK)),
            _const_spec((1, KV_LORA_RANK)),
            _const_spec(w_qn.shape), _const_spec(w_qr.shape),
            _const_spec(w_kn.shape), _const_spec(w_v.shape),
            _const_spec((1, LANES)), _const_spec((1, LANES)),
            _const_spec((1, LANES)), _const_spec((1, LANES)),
            _const_spec((1, LANES)),
        ],
        out_specs=[
            pl.BlockSpec((tm, N_HEADS * HEAD_SLOT), prev),
            pl.BlockSpec((tm, N_HEADS * HEAD_SLOT), prev),
            pl.BlockSpec((1, N_HEADS * V_HEAD_DIM, tm), lambda i: (jnp.maximum(i - 1, 0), 0, 0)),
            pl.BlockSpec((tm, D_MODEL), row),
        ],
        out_shape=[
            jax.ShapeDtypeStruct((t, N_HEADS * HEAD_SLOT), BF16),
            jax.ShapeDtypeStruct((t, N_HEADS * HEAD_SLOT), BF16),
            jax.ShapeDtypeStruct((n, N_HEADS * V_HEAD_DIM, tm), BF16),
            jax.ShapeDtypeStruct((t, D_MODEL), BF16),
        ],
        scratch_shapes=[
            pltpu.VMEM((tm, N_HEADS * QK_NOPE_DIM), F32),
            pltpu.VMEM((tm, N_HEADS * QK_ROPE_DIM), F32),
            pltpu.VMEM((tm, N_HEADS * QK_NOPE_DIM), F32),
            pltpu.VMEM((tm, N_HEADS * V_HEAD_DIM), BF16),
            pltpu.VMEM((tm, LANES), F32),
        ],
        compiler_params=_params("arbitrary"),
        name="mla_prep",
    )(x2, pos, ln_g, w_lat, qa_g, kva_g, w_qn, w_qr, w_kn, w_v, qg_nope, qg_rope, kg_nope, kg_rope, inv_freq)


def _in_proj_kernel(u_ref, w_ref, z_ref):
    z_ref[...] = _dot_nt(u_ref[...], w_ref[...]).astype(BF16)


def _in_proj(u, w_in_t):
    t = u.shape[0]
    tm, tn = TM_INPROJ, TN_INPROJ
    gate_tiles = 2 * D_MODEL // tn

    def w_rows(i, j):
        rows = jnp.where(j < gate_tiles, GATE_OFFSET + j * tn, (j - gate_tiles) * tn)
        return pl.multiple_of(rows, math.gcd(GATE_OFFSET, tn)), 0

    return pl.pallas_call(
        _in_proj_kernel,
        grid=(t // tm, Z_WIDTH // tn),
        in_specs=[
            pl.BlockSpec((tm, D_MODEL), lambda i, j: (i, 0)),
            pl.BlockSpec((pl.Element(tn), pl.Element(D_MODEL)), w_rows),
        ],
        out_specs=pl.BlockSpec((tm, tn), lambda i, j: (i, j)),
        out_shape=jax.ShapeDtypeStruct((t, Z_WIDTH), BF16),
        compiler_params=_params("arbitrary", "arbitrary"),
        name="in_proj",
    )(u, w_in_t)


def _attn_kernel(qg_ref, kg_ref, q_ref, k_ref, vt_ref, o_ref, *, tq, vt_cols, q_scale):
    i = pl.program_id(2)
    sub_blocks = tq // vt_cols
    gq = jnp.max(jnp.abs(qg_ref[...]), axis=-1, keepdims=True)
    gk = jnp.max(jnp.abs(kg_ref[...]), axis=-1, keepdims=True)
    bound = (gq * gk)[0, 0] * (q_scale * QK_HEAD_DIM)
    fixed_shift = bound <= FIXED_SHIFT_LIMIT

    def score(j, hh):
        start = pl.multiple_of(j * tq, tq)
        q = q_ref[:, hh * HEAD_SLOT:(hh + 1) * HEAD_SLOT]
        kb = k_ref[pl.ds(start, tq), hh * HEAD_SLOT:(hh + 1) * HEAD_SLOT]
        return _dot_nt(kb, q)

    def per_head(j, fn):
        sc = {hh: score(j, hh) for hh in range(SCORES_AHEAD)}
        out = []
        for hh in range(HEADS_PER_STEP):
            out.append(fn(hh, sc.pop(hh)))
            if hh + SCORES_AHEAD < HEADS_PER_STEP:
                sc[hh + SCORES_AHEAD] = score(j, hh + SCORES_AHEAD)
        return tuple(out)

    def causal(s):
        kv_idx = lax.broadcasted_iota(jnp.int32, (tq, tq), 0)
        q_idx = lax.broadcasted_iota(jnp.int32, (tq, tq), 1)
        return jnp.where(kv_idx <= q_idx, s, NEG_INF)

    def p_times_v(j, hh, p):
        pb = p.astype(BF16)
        pv = None
        for c in range(sub_blocks):
            vt = vt_ref[j * sub_blocks + c, hh * V_HEAD_DIM:(hh + 1) * V_HEAD_DIM, :]
            part = _dot(vt, pb[c * vt_cols:(c + 1) * vt_cols, :])
            pv = part if pv is None else pv + part
        return pv

    def write(hh, l, acc):
        o_ref[:, hh * V_HEAD_DIM:(hh + 1) * V_HEAD_DIM] = (acc / l).T.astype(BF16)

    @pl.when(fixed_shift)
    def _():
        def step(j, carry, masked):
            def head(hh, s):
                l, acc = carry[hh]
                p = jnp.exp2((causal(s) if masked else s) - bound)
                return l + jnp.sum(p, axis=0, keepdims=True), acc + p_times_v(j, hh, p)

            return per_head(j, head)

        init = tuple((jnp.zeros((1, tq), F32), jnp.zeros((V_HEAD_DIM, tq), F32)) for _ in range(HEADS_PER_STEP))
        carry = step(i, lax.fori_loop(0, i, lambda j, c: step(j, c, False), init), True)
        for hh in range(HEADS_PER_STEP):
            write(hh, *carry[hh])

    @pl.when(jnp.logical_not(fixed_shift))
    def _():
        def step(j, carry, masked):
            def head(hh, s):
                m, l, acc = carry[hh]
                s = causal(s) if masked else s
                m_new = jnp.maximum(m, jnp.max(s, axis=0, keepdims=True))
                alpha = jnp.exp2(m - m_new)
                p = jnp.exp2(s - m_new)
                return m_new, alpha * l + jnp.sum(p, axis=0, keepdims=True), alpha * acc + p_times_v(j, hh, p)

            return per_head(j, head)

        init = tuple((jnp.full((1, tq), NEG_INF, F32), jnp.zeros((1, tq), F32), jnp.zeros((V_HEAD_DIM, tq), F32))
                     for _ in range(HEADS_PER_STEP))
        carry = step(i, lax.fori_loop(0, i, lambda j, c: step(j, c, False), init), True)
        for hh in range(HEADS_PER_STEP):
            write(hh, *carry[hh][1:])


def _attn(q, k, vt, q_norm_g, k_norm_g, batch, seq):
    tq = TQ_ATTN
    nq = seq // tq
    vt_cols = vt.shape[-1]
    hs = HEADS_PER_STEP
    return pl.pallas_call(
        functools.partial(_attn_kernel, tq=tq, vt_cols=vt_cols, q_scale=Q_SCALE),
        grid=(batch, N_HEADS // hs, nq),
        in_specs=[
            _const_spec((1, QK_HEAD_DIM)),
            _const_spec((1, QK_HEAD_DIM)),
            pl.BlockSpec((tq, hs * HEAD_SLOT), lambda b, h, i: (b * nq + i, h)),
            pl.BlockSpec((seq, hs * HEAD_SLOT), lambda b, h, i: (b, h)),
            pl.BlockSpec((seq // vt_cols, hs * V_HEAD_DIM, vt_cols), lambda b, h, i: (b, h, 0)),
        ],
        out_specs=pl.BlockSpec((tq, hs * V_HEAD_DIM), lambda b, h, i: (b * nq + i, h)),
        out_shape=jax.ShapeDtypeStruct((batch * seq, N_HEADS * V_HEAD_DIM), BF16),
        compiler_params=_params("arbitrary", "arbitrary", "arbitrary"),
        name="attn",
    )(q_norm_g, k_norm_g, q, k, vt)


def _mix_out_kernel(gates_ref, zb_ref, zc_ref, zv_ref, zc_prev_ref, zv_prev_ref, o_ref, x_ref,
                    b_gate_ref, conv_w_ref, w_conv_out_ref, w_mla_out_ref, w_o_ref, ln2_g_ref,
                    h_ref, u2_ref, *, tm, tiles_per_seq):
    seq_start = (pl.program_id(0) % tiles_per_seq) == 0
    prev = zc_prev_ref[...].astype(F32) * zv_prev_ref[...].astype(F32)
    cv = jnp.concatenate([jnp.where(seq_start, 0.0, prev), zc_ref[...].astype(F32) * zv_ref[...].astype(F32)], axis=0)
    conv = _causal_conv3(cv, conv_w_ref[...], tm)
    conv_in = (zb_ref[...].astype(F32) * conv).astype(BF16)
    y_conv = _dot(conv_in, w_conv_out_ref[...])
    y_mla = _dot(o_ref[...], w_mla_out_ref[...])
    gates = jax.nn.sigmoid(gates_ref[...].astype(F32) + b_gate_ref[...])
    mix = (gates[:, :D_MODEL] * y_conv + gates[:, D_MODEL:] * y_mla).astype(BF16)
    h = x_ref[...] + _dot(mix, w_o_ref[...])
    h_ref[...] = h
    u2_ref[...] = (h * _rms_scale(h, D_MODEL) * ln2_g_ref[...]).astype(BF16)


def _mix_out(z, o, x2, b_gate, conv_w, w_conv_out, w_mla_out, w_o, ln2_g, seq):
    t = x2.shape[0]
    tm = TM_MIX
    halo_per_tile = tm // HALO
    cblk = lambda c: (lambda i: (i, c))
    prev = lambda c: (lambda i: (jnp.maximum(i * halo_per_tile - 1, 0), c))
    gate_blocks = 2 * D_MODEL // CONV_DIM
    return pl.pallas_call(
        functools.partial(_mix_out_kernel, tm=tm, tiles_per_seq=seq // tm),
        grid=(t // tm,),
        in_specs=[
            pl.BlockSpec((tm, 2 * D_MODEL), cblk(0)),
            pl.BlockSpec((tm, CONV_DIM), cblk(gate_blocks)),
            pl.BlockSpec((tm, CONV_DIM), cblk(gate_blocks + 1)),
            pl.BlockSpec((tm, CONV_DIM), cblk(gate_blocks + 2)),
            pl.BlockSpec((HALO, CONV_DIM), prev(gate_blocks + 1)),
            pl.BlockSpec((HALO, CONV_DIM), prev(gate_blocks + 2)),
            pl.BlockSpec((tm, D_MODEL), cblk(0)),
            pl.BlockSpec((tm, D_MODEL), cblk(0)),
            _const_spec((1, 2 * D_MODEL)),
            _const_spec((CONV_WIDTH, CONV_DIM)),
            _const_spec((CONV_DIM, D_MODEL)),
            _const_spec((D_MODEL, D_MODEL)),
            _const_spec((D_MODEL, D_MODEL)),
            _const_spec((1, D_MODEL)),
        ],
        out_specs=[pl.BlockSpec((tm, D_MODEL), cblk(0)), pl.BlockSpec((tm, D_MODEL), cblk(0))],
        out_shape=[jax.ShapeDtypeStruct((t, D_MODEL), F32), jax.ShapeDtypeStruct((t, D_MODEL), BF16)],
        compiler_params=_params("arbitrary"),
        name="mix_out",
    )(z, z, z, z, z, z, o, x2, b_gate, conv_w, w_conv_out, w_mla_out, w_o, ln2_g)


def _ffn_kernel(u_ref, u_prev_ref, h_ref, wg_ref, wu_ref, cwg_ref, cwu_ref, bg_ref, bu_ref, wd_ref,
                out_ref, lhs_ref, pg_ref, pu_ref, *, tm, row_chunk, tiles_per_seq):
    @pl.when(pl.program_id(1) == 0)
    def _():
        seq_start = (pl.program_id(0) % tiles_per_seq) == 0
        lhs_ref[0:HALO, :] = jnp.where(seq_start, jnp.zeros_like(u_prev_ref), u_prev_ref[...])
        lhs_ref[HALO:, :] = u_ref[...]
        out_ref[...] = h_ref[...]

    bounds = [0] + [HALO + (c + 1) * row_chunk for c in range(tm // row_chunk)]
    for lo, hi in zip(bounds[:-1], bounds[1:]):
        pg_ref[lo:hi, :] = _dot(lhs_ref[lo:hi, :], wg_ref[...])
        pu_ref[lo:hi, :] = _dot(lhs_ref[lo:hi, :], wu_ref[...])

    def conv(p_ref, cw_ref, b_ref, r0):
        cw = cw_ref[...]
        return (cw[0:1, :] * p_ref[r0 + HALO - 2:r0 + HALO - 2 + row_chunk, :]
                + cw[1:2, :] * p_ref[r0 + HALO - 1:r0 + HALO - 1 + row_chunk, :]
                + cw[2:3, :] * p_ref[r0 + HALO:r0 + HALO + row_chunk, :]) + b_ref[...]

    for r0 in range(0, tm, row_chunk):
        a_gate = conv(pg_ref, cwg_ref, bg_ref, r0)
        a_up = conv(pu_ref, cwu_ref, bu_ref, r0)
        act = (a_gate * jax.nn.sigmoid(a_gate) * a_up).astype(BF16)
        out_ref[r0:r0 + row_chunk, :] += _dot(act, wd_ref[...])


def _ffn(u2, h1, w_up, conv_w, conv_b, w_down, seq):
    t = u2.shape[0]
    tm, tf = TM_FFN, TF_FFN
    nf = D_FF // tf
    halo_per_tile = tm // HALO
    return pl.pallas_call(
        functools.partial(_ffn_kernel, tm=tm, row_chunk=FFN_ROW_CHUNK, tiles_per_seq=seq // tm),
        grid=(t // tm, nf),
        in_specs=[
            pl.BlockSpec((tm, D_MODEL), lambda i, j: (i, 0)),
            pl.BlockSpec((HALO, D_MODEL), lambda i, j: (jnp.maximum(i * halo_per_tile - 1, 0), 0)),
            pl.BlockSpec((tm, D_MODEL), lambda i, j: (i, 0)),
            pl.BlockSpec((D_MODEL, tf), lambda i, j: (0, j)),
            pl.BlockSpec((D_MODEL, tf), lambda i, j: (0, j + nf)),
            pl.BlockSpec((CONV_WIDTH, tf), lambda i, j: (0, j)),
            pl.BlockSpec((CONV_WIDTH, tf), lambda i, j: (0, j + nf)),
            pl.BlockSpec((1, tf), lambda i, j: (0, j)),
            pl.BlockSpec((1, tf), lambda i, j: (0, j + nf)),
            pl.BlockSpec((tf, D_MODEL), lambda i, j: (j, 0)),
        ],
        out_specs=pl.BlockSpec((tm, D_MODEL), lambda i, j: (i, 0)),
        out_shape=jax.ShapeDtypeStruct((t, D_MODEL), F32),
        scratch_shapes=[
            pltpu.VMEM((tm + HALO, D_MODEL), BF16),
            pltpu.VMEM((tm + HALO, tf), F32),
            pltpu.VMEM((tm + HALO, tf), F32),
        ],
        compiler_params=_params("arbitrary", "arbitrary"),
        name="ffn",
    )(u2, u2, h1, w_up, w_up, conv_w, conv_w, conv_b, conv_b, w_down)


def _row(v):
    return v.reshape(1, -1).astype(F32)


def _layer(h2, pos, inv_freq, batch, seq, ln1_g, w_in, b_gate, conv_w, w_conv_out, q_a_g, w_q_b, kv_a_g, w_kv_b,
           q_norm_g, k_norm_g, w_mla_out, w_o, ln2_g, w_ffn_up, ffn_conv_w, ffn_conv_b, w_ffn_down):
    w_in_t = w_in.T.astype(BF16)
    wq3 = w_q_b.reshape(Q_LORA_RANK, N_HEADS, QK_HEAD_DIM)
    w_qn = wq3[:, :, :QK_NOPE_DIM].reshape(Q_LORA_RANK, -1).astype(BF16)
    w_qr = wq3[:, :, QK_NOPE_DIM:].reshape(Q_LORA_RANK, -1).astype(BF16)
    wkv3 = w_kv_b.reshape(KV_LORA_RANK, N_HEADS, QK_NOPE_DIM + V_HEAD_DIM)
    w_kn = wkv3[:, :, :QK_NOPE_DIM].reshape(KV_LORA_RANK, -1).astype(BF16)
    w_v = wkv3[:, :, QK_NOPE_DIM:].reshape(KV_LORA_RANK, -1).astype(BF16)
    twice = lambda g: _row(jnp.concatenate([g, g]))

    q, k, vt, u = _mla_prep(h2, pos, _row(ln1_g), w_in_t, _row(q_a_g), _row(kv_a_g), w_qn, w_qr, w_kn, w_v,
                            _row(q_norm_g[:QK_NOPE_DIM]), twice(q_norm_g[QK_NOPE_DIM:]),
                            _row(k_norm_g[:QK_NOPE_DIM]), twice(k_norm_g[QK_NOPE_DIM:]), inv_freq)
    z = _in_proj(u, w_in_t)
    o = _attn(q, k, vt, _row(q_norm_g), _row(k_norm_g), batch, seq)
    h1, u2 = _mix_out(z, o, h2, _row(b_gate), conv_w.astype(F32), w_conv_out.astype(BF16),
                      w_mla_out.astype(BF16), w_o.astype(BF16), _row(ln2_g), seq)
    return _ffn(u2, h1, w_ffn_up.astype(BF16), ffn_conv_w.astype(F32), _row(ffn_conv_b),
                w_ffn_down.astype(BF16), seq)


def kernel(x, positions, ln1_g, w_in, b_gate, conv_w, w_conv_out, q_a_g, w_q_b, kv_a_g, w_kv_b,
           q_norm_g, k_norm_g, w_mla_out, w_o, ln2_g, w_ffn_up, ffn_conv_w, ffn_conv_b, w_ffn_down):
    batch, seq, _ = x.shape
    depth = ln1_g.shape[0]
    h2 = x.reshape(batch * seq, D_MODEL)
    pos = positions.reshape(batch * seq, 1).astype(jnp.int32)
    freq = ROPE_THETA ** (-jnp.arange(0, QK_ROPE_DIM, 2, dtype=F32) / QK_ROPE_DIM)
    inv_freq = jnp.tile(freq, LANES // (QK_ROPE_DIM // 2)).reshape(1, LANES)
    for l in range(depth):
        h2 = _layer(h2, pos, inv_freq, batch, seq, ln1_g[l], w_in[l], b_gate[l], conv_w[l], w_conv_out[l],
                    q_a_g[l], w_q_b[l], kv_a_g[l], w_kv_b[l], q_norm_g[l], k_norm_g[l], w_mla_out[l],
                    w_o[l], ln2_g[l], w_ffn_up[l], ffn_conv_w[l], ffn_conv_b[l], w_ffn_down[l])
    return h2.reshape(batch, seq, D_MODEL)
```

```python
import functools
import math

import jax
import jax.numpy as jnp
from jax import lax
from jax.experimental import pallas as pl
from jax.experimental.pallas import tpu as pltpu

D_MODEL = 2048
CONV_DIM = 1024
CONV_WIDTH = 3
N_HEADS = 16
QK_NOPE_DIM = 128
QK_ROPE_DIM = 64
QK_HEAD_DIM = QK_NOPE_DIM + QK_ROPE_DIM
V_HEAD_DIM = 128
Q_LORA_RANK = 768
KV_LORA_RANK = 512
ROPE_THETA = 10000.0
D_FF = 5632
NORM_EPS = 1e-6
NEG_INF = -1e30

Q_SCALE = QK_HEAD_DIM ** -0.5 * math.log2(math.e)
FIXED_SHIFT_LIMIT = 48.0

LANES = 128
HEAD_SLOT = 2 * LANES
HALO = 16
LAT_OFFSET = 3 * CONV_DIM
LAT_WIDTH = 1536
GATE_OFFSET = LAT_OFFSET + Q_LORA_RANK + KV_LORA_RANK + QK_ROPE_DIM
Z_WIDTH = 2 * D_MODEL + 3 * CONV_DIM
VMEM_LIMIT = 56 * 1024 * 1024

TM_PREP = 256
TM_INPROJ = 2048
TN_INPROJ = 1024
TQ_ATTN = 1024
TK_ATTN = 512
HEADS_PER_STEP = 4
SCORES_AHEAD = 4
TM_MIX = 256
TM_FFN = 512
TF_FFN = 512
FFN_ROW_CHUNK = 256

F32 = jnp.float32
BF16 = jnp.bfloat16


def _dot(a, b):
    return jnp.dot(a, b, preferred_element_type=F32)


def _dot_nt(a, b):
    return lax.dot_general(a, b, (((1,), (1,)), ((), ())), preferred_element_type=F32)


def _rms_scale(x, width):
    return lax.rsqrt(jnp.sum(x * x, axis=-1, keepdims=True) * (1.0 / width) + NORM_EPS)


def _params(*sem):
    return pltpu.CompilerParams(dimension_semantics=sem, vmem_limit_bytes=VMEM_LIMIT)


def _const_spec(shape):
    return pl.BlockSpec(shape, lambda *_: (0,) * len(shape), pipeline_mode=pl.Buffered(1))


def _causal_conv3(p, cw, rows):
    full = cw[2:3, :] * p + cw[1:2, :] * pltpu.roll(p, 1, axis=0) + cw[0:1, :] * pltpu.roll(p, 2, axis=0)
    return full[p.shape[0] - rows:, :]


def _mla_prep_kernel(x_ref, pos_ref, ln_g_ref, w_lat_ref, qa_g_ref, kva_g_ref, w_qn_ref, w_qr_ref,
                     w_kn_ref, w_v_ref, qg_nope_ref, qg_rope_ref, kg_nope_ref, kg_rope_ref, inv_freq_ref,
                     q_ref, k_ref, vt_ref, u_ref, qn_s, qr_s, kn_s, v_s, kr_s, *, q_scale):
    @pl.when(pl.program_id(0) == 0)
    def _():
        for s in (qn_s, qr_s, kn_s, v_s, kr_s):
            s[...] = jnp.zeros_like(s)

    ang = pos_ref[...].astype(F32) * inv_freq_ref[...]
    cos = jnp.cos(ang)
    sin = jnp.sin(ang)
    lane = lax.broadcasted_iota(jnp.int32, (1, LANES), 1)
    first_half = (lane % QK_ROPE_DIM) < (QK_ROPE_DIM // 2)
    sin_signed = jnp.where(first_half, -sin, sin)

    def rope(t):
        rot = jnp.where(first_half, pltpu.roll(t, LANES - QK_ROPE_DIM // 2, axis=1),
                        pltpu.roll(t, QK_ROPE_DIM // 2, axis=1))
        return t * cos + rot * sin_signed

    k_rope = kr_s[...]
    k_rope = jnp.where(lane < QK_ROPE_DIM, k_rope, pltpu.roll(k_rope, QK_ROPE_DIM, axis=1))
    qg_nope = qg_nope_ref[...] * q_scale
    qg_rope = qg_rope_ref[...] * q_scale
    kg_nope = kg_nope_ref[...]
    k_tail = rope(k_rope * kg_rope_ref[...])
    k_rope_ss = 0.5 * jnp.sum(k_rope * k_rope, axis=-1, keepdims=True)

    for pair in range(N_HEADS // 2):
        q_pair = qr_s[:, pair * LANES:(pair + 1) * LANES]
        q_pair_roped = rope(q_pair * qg_rope)
        for sub in range(2):
            h = 2 * pair + sub
            mine = (lane // QK_ROPE_DIM) == sub
            q_nope = qn_s[:, h * QK_NOPE_DIM:(h + 1) * QK_NOPE_DIM]
            ss = (jnp.sum(q_nope * q_nope, axis=-1, keepdims=True)
                  + jnp.sum(jnp.where(mine, q_pair * q_pair, 0.0), axis=-1, keepdims=True))
            r = lax.rsqrt(ss * (1.0 / QK_HEAD_DIM) + NORM_EPS)
            q_ref[:, h * HEAD_SLOT: h * HEAD_SLOT + LANES] = (q_nope * r * qg_nope).astype(BF16)
            q_ref[:, h * HEAD_SLOT + LANES:(h + 1) * HEAD_SLOT] = jnp.where(
                mine, q_pair_roped * r, 0.0).astype(BF16)

    for h in range(N_HEADS):
        k_nope = kn_s[:, h * QK_NOPE_DIM:(h + 1) * QK_NOPE_DIM]
        kss = jnp.sum(k_nope * k_nope, axis=-1, keepdims=True) + k_rope_ss
        kr = lax.rsqrt(kss * (1.0 / QK_HEAD_DIM) + NORM_EPS)
        k_ref[:, h * HEAD_SLOT: h * HEAD_SLOT + LANES] = (k_nope * kr * kg_nope).astype(BF16)
        k_ref[:, h * HEAD_SLOT + LANES:(h + 1) * HEAD_SLOT] = (k_tail * kr).astype(BF16)

    vt_ref[0] = v_s[...].T

    xf = x_ref[...]
    u = (xf * _rms_scale(xf, D_MODEL) * ln_g_ref[...]).astype(BF16)
    u_ref[...] = u
    lat = _dot_nt(u, w_lat_ref[...])
    q_lat = lat[:, :Q_LORA_RANK]
    kv_lat = lat[:, Q_LORA_RANK:Q_LORA_RANK + KV_LORA_RANK]
    kr_s[...] = lat[:, Q_LORA_RANK + KV_LORA_RANK:Q_LORA_RANK + KV_LORA_RANK + LANES]
    qn = (q_lat * _rms_scale(q_lat, Q_LORA_RANK) * qa_g_ref[...]).astype(BF16)
    kvn = (kv_lat * _rms_scale(kv_lat, KV_LORA_RANK) * kva_g_ref[...]).astype(BF16)
    qn_s[...] = _dot(qn, w_qn_ref[...])
    qr_s[...] = _dot(qn, w_qr_ref[...])
    kn_s[...] = _dot(kvn, w_kn_ref[...])
    v_s[...] = _dot(kvn, w_v_ref[...]).astype(BF16)


def _mla_prep(x2, pos, ln_g, w_lat, qa_g, kva_g, w_qn, w_qr, w_kn, w_v, qg_nope, qg_rope, kg_nope, kg_rope,
              inv_freq):
    t = x2.shape[0]
    tm = TM_PREP
    n = t // tm
    row = lambda i: (jnp.minimum(i, n - 1), 0)
    prev = lambda i: (jnp.maximum(i - 1, 0), 0)
    return pl.pallas_call(
        functools.partial(_mla_prep_kernel, q_scale=Q_SCALE),
        grid=(n + 1,),
        in_specs=[
            pl.BlockSpec((tm, D_MODEL), row),
            pl.BlockSpec((tm, 1), prev),
            _const_spec((1, D_MODEL)),
            pl.BlockSpec((pl.Element(LAT_WIDTH), pl.Element(D_MODEL)), lambda i: (LAT_OFFSET, 0),
                         pipeline_mode=pl.Buffered(1)),
            _const_spec((1, Q_LORA_RANK)),
            _const_spec((1, KV_LORA_RANK)),
            _const_spec(w_qn.shape), _const_spec(w_qr.shape),
            _const_spec(w_kn.shape), _const_spec(w_v.shape),
            _const_spec((1, LANES)), _const_spec((1, LANES)),
            _const_spec((1, LANES)), _const_spec((1, LANES)),
            _const_spec((1, LANES)),
        ],
        out_specs=[
            pl.BlockSpec((tm, N_HEADS * HEAD_SLOT), prev),
            pl.BlockSpec((tm, N_HEADS * HEAD_SLOT), prev),
            pl.BlockSpec((1, N_HEADS * V_HEAD_DIM, tm), lambda i: (jnp.maximum(i - 1, 0), 0, 0)),
            pl.BlockSpec((tm, D_MODEL), row),
        ],
        out_shape=[
            jax.ShapeDtypeStruct((t, N_HEADS * HEAD_SLOT), BF16),
            jax.ShapeDtypeStruct((t, N_HEADS * HEAD_SLOT), BF16),
            jax.ShapeDtypeStruct((n, N_HEADS * V_HEAD_DIM, tm), BF16),
            jax.ShapeDtypeStruct((t, D_MODEL), BF16),
        ],
        scratch_shapes=[
            pltpu.VMEM((tm, N_HEADS * QK_NOPE_DIM), F32),
            pltpu.VMEM((tm, N_HEADS * QK_ROPE_DIM), F32),
            pltpu.VMEM((tm, N_HEADS * QK_NOPE_DIM), F32),
            pltpu.VMEM((tm, N_HEADS * V_HEAD_DIM), BF16),
            pltpu.VMEM((tm, LANES), F32),
        ],
        compiler_params=_params("arbitrary"),
        name="mla_prep",
    )(x2, pos, ln_g, w_lat, qa_g, kva_g, w_qn, w_qr, w_kn, w_v, qg_nope, qg_rope, kg_nope, kg_rope, inv_freq)


def _in_proj_kernel(u_ref, w_ref, z_ref):
    z_ref[...] = _dot_nt(u_ref[...], w_ref[...]).astype(BF16)


def _in_proj(u, w_in_t):
    t = u.shape[0]
    tm, tn = TM_INPROJ, TN_INPROJ
    gate_tiles = 2 * D_MODEL // tn

    def w_rows(i, j):
        rows = jnp.where(j < gate_tiles, GATE_OFFSET + j * tn, (j - gate_tiles) * tn)
        return pl.multiple_of(rows, math.gcd(GATE_OFFSET, tn)), 0

    return pl.pallas_call(
        _in_proj_kernel,
        grid=(t // tm, Z_WIDTH // tn),
        in_specs=[
            pl.BlockSpec((tm, D_MODEL), lambda i, j: (i, 0)),
            pl.BlockSpec((pl.Element(tn), pl.Element(D_MODEL)), w_rows),
        ],
        out_specs=pl.BlockSpec((tm, tn), lambda i, j: (i, j)),
        out_shape=jax.ShapeDtypeStruct((t, Z_WIDTH), BF16),
        compiler_params=_params("arbitrary", "arbitrary"),
        name="in_proj",
    )(u, w_in_t)


def _attn_kernel(qg_ref, kg_ref, q_ref, k_ref, vt_ref, o_ref, *, tq, tk, vt_cols, q_scale):
    i = pl.program_id(2)
    sub_blocks = tk // vt_cols
    kv_per_q = tq // tk
    gq = jnp.max(jnp.abs(qg_ref[...]), axis=-1, keepdims=True)
    gk = jnp.max(jnp.abs(kg_ref[...]), axis=-1, keepdims=True)
    bound = (gq * gk)[0, 0] * (q_scale * QK_HEAD_DIM)
    fixed_shift = bound <= FIXED_SHIFT_LIMIT

    def score(j, hh, col0):
        start = pl.multiple_of(j * tk, tk)
        q = q_ref[col0:, hh * HEAD_SLOT:(hh + 1) * HEAD_SLOT]
        kb = k_ref[pl.ds(start, tk), hh * HEAD_SLOT:(hh + 1) * HEAD_SLOT]
        return _dot_nt(kb, q)

    def per_head(j, col0, fn):
        sc = {hh: score(j, hh, col0) for hh in range(SCORES_AHEAD)}
        out = []
        for hh in range(HEADS_PER_STEP):
            out.append(fn(hh, sc.pop(hh)))
            if hh + SCORES_AHEAD < HEADS_PER_STEP:
                sc[hh + SCORES_AHEAD] = score(j, hh + SCORES_AHEAD, col0)
        return tuple(out)

    def causal(s):
        kv_idx = lax.broadcasted_iota(jnp.int32, s.shape, 0)
        q_idx = lax.broadcasted_iota(jnp.int32, s.shape, 1)
        return jnp.where(kv_idx <= q_idx, s, NEG_INF)

    def p_times_v(j, hh, p):
        pb = p.astype(BF16)
        pv = None
        for c in range(sub_blocks):
            vt = vt_ref[j * sub_blocks + c, hh * V_HEAD_DIM:(hh + 1) * V_HEAD_DIM, :]
            part = _dot(vt, pb[c * vt_cols:(c + 1) * vt_cols, :])
            pv = part if pv is None else pv + part
        return pv

    def tail(old, new, col0):
        return new if col0 == 0 else jnp.concatenate([old[:, :col0], new], axis=1)

    def sweep(step, init):
        carry = lax.fori_loop(0, i * kv_per_q, lambda j, c: step(j, c, 0, False), init)
        for d in range(kv_per_q):
            carry = step(i * kv_per_q + d, carry, d * tk, True)
        return carry

    def write(hh, l, acc):
        o_ref[:, hh * V_HEAD_DIM:(hh + 1) * V_HEAD_DIM] = (acc / l).T.astype(BF16)

    @pl.when(fixed_shift)
    def _():
        def step(j, carry, col0, masked):
            def head(hh, s):
                l, acc = carry[hh]
                p = jnp.exp2((causal(s) if masked else s) - bound)
                return (tail(l, l[:, col0:] + jnp.sum(p, axis=0, keepdims=True), col0),
                        tail(acc, acc[:, col0:] + p_times_v(j, hh, p), col0))

            return per_head(j, col0, head)

        carry = sweep(step, tuple((jnp.zeros((1, tq), F32), jnp.zeros((V_HEAD_DIM, tq), F32))
                                  for _ in range(HEADS_PER_STEP)))
        for hh in range(HEADS_PER_STEP):
            write(hh, *carry[hh])

    @pl.when(jnp.logical_not(fixed_shift))
    def _():
        def step(j, carry, col0, masked):
            def head(hh, s):
                m, l, acc = (t[:, col0:] for t in carry[hh])
                s = causal(s) if masked else s
                m_new = jnp.maximum(m, jnp.max(s, axis=0, keepdims=True))
                alpha = jnp.exp2(m - m_new)
                p = jnp.exp2(s - m_new)
                new = (m_new, alpha * l + jnp.sum(p, axis=0, keepdims=True), alpha * acc + p_times_v(j, hh, p))
                return tuple(tail(o, n, col0) for o, n in zip(carry[hh], new))

            return per_head(j, col0, head)

        carry = sweep(step, tuple((jnp.full((1, tq), NEG_INF, F32), jnp.zeros((1, tq), F32),
                                   jnp.zeros((V_HEAD_DIM, tq), F32)) for _ in range(HEADS_PER_STEP)))
        for hh in range(HEADS_PER_STEP):
            write(hh, *carry[hh][1:])


def _attn(q, k, vt, q_norm_g, k_norm_g, batch, seq):
    tq, tk = TQ_ATTN, TK_ATTN
    nq = seq // tq
    vt_cols = vt.shape[-1]
    hs = HEADS_PER_STEP
    return pl.pallas_call(
        functools.partial(_attn_kernel, tq=tq, tk=tk, vt_cols=vt_cols, q_scale=Q_SCALE),
        grid=(batch, N_HEADS // hs, nq),
        in_specs=[
            _const_spec((1, QK_HEAD_DIM)),
            _const_spec((1, QK_HEAD_DIM)),
            pl.BlockSpec((tq, hs * HEAD_SLOT), lambda b, h, i: (b * nq + i, h)),
            pl.BlockSpec((seq, hs * HEAD_SLOT), lambda b, h, i: (b, h)),
            pl.BlockSpec((seq // vt_cols, hs * V_HEAD_DIM, vt_cols), lambda b, h, i: (b, h, 0)),
        ],
        out_specs=pl.BlockSpec((tq, hs * V_HEAD_DIM), lambda b, h, i: (b * nq + i, h)),
        out_shape=jax.ShapeDtypeStruct((batch * seq, N_HEADS * V_HEAD_DIM), BF16),
        compiler_params=_params("arbitrary", "arbitrary", "arbitrary"),
        name="attn",
    )(q_norm_g, k_norm_g, q, k, vt)


def _mix_out_kernel(gates_ref, zb_ref, zc_ref, zv_ref, zc_prev_ref, zv_prev_ref, o_ref, x_ref,
                    b_gate_ref, conv_w_ref, w_conv_out_ref, w_mla_out_ref, w_o_ref, ln2_g_ref,
                    h_ref, u2_ref, *, tm, tiles_per_seq):
    seq_start = (pl.program_id(0) % tiles_per_seq) == 0
    prev = zc_prev_ref[...].astype(F32) * zv_prev_ref[...].astype(F32)
    cv = jnp.concatenate([jnp.where(seq_start, 0.0, prev), zc_ref[...].astype(F32) * zv_ref[...].astype(F32)], axis=0)
    conv = _causal_conv3(cv, conv_w_ref[...], tm)
    conv_in = (zb_ref[...].astype(F32) * conv).astype(BF16)
    y_conv = _dot(conv_in, w_conv_out_ref[...])
    y_mla = _dot(o_ref[...], w_mla_out_ref[...])
    gates = jax.nn.sigmoid(gates_ref[...].astype(F32) + b_gate_ref[...])
    mix = (gates[:, :D_MODEL] * y_conv + gates[:, D_MODEL:] * y_mla).astype(BF16)
    h = x_ref[...] + _dot(mix, w_o_ref[...])
    h_ref[...] = h
    u2_ref[...] = (h * _rms_scale(h, D_MODEL) * ln2_g_ref[...]).astype(BF16)


def _mix_out(z, o, x2, b_gate, conv_w, w_conv_out, w_mla_out, w_o, ln2_g, seq):
    t = x2.shape[0]
    tm = TM_MIX
    halo_per_tile = tm // HALO
    cblk = lambda c: (lambda i: (i, c))
    prev = lambda c: (lambda i: (jnp.maximum(i * halo_per_tile - 1, 0), c))
    gate_blocks = 2 * D_MODEL // CONV_DIM
    return pl.pallas_call(
        functools.partial(_mix_out_kernel, tm=tm, tiles_per_seq=seq // tm),
        grid=(t // tm,),
        in_specs=[
            pl.BlockSpec((tm, 2 * D_MODEL), cblk(0)),
            pl.BlockSpec((tm, CONV_DIM), cblk(gate_blocks)),
            pl.BlockSpec((tm, CONV_DIM), cblk(gate_blocks + 1)),
            pl.BlockSpec((tm, CONV_DIM), cblk(gate_blocks + 2)),
            pl.BlockSpec((HALO, CONV_DIM), prev(gate_blocks + 1)),
            pl.BlockSpec((HALO, CONV_DIM), prev(gate_blocks + 2)),
            pl.BlockSpec((tm, D_MODEL), cblk(0)),
            pl.BlockSpec((tm, D_MODEL), cblk(0)),
            _const_spec((1, 2 * D_MODEL)),
            _const_spec((CONV_WIDTH, CONV_DIM)),
            _const_spec((CONV_DIM, D_MODEL)),
            _const_spec((D_MODEL, D_MODEL)),
            _const_spec((D_MODEL, D_MODEL)),
            _const_spec((1, D_MODEL)),
        ],
        out_specs=[pl.BlockSpec((tm, D_MODEL), cblk(0)), pl.BlockSpec((tm, D_MODEL), cblk(0))],
        out_shape=[jax.ShapeDtypeStruct((t, D_MODEL), F32), jax.ShapeDtypeStruct((t, D_MODEL), BF16)],
        compiler_params=_params("arbitrary"),
        name="mix_out",
    )(z, z, z, z, z, z, o, x2, b_gate, conv_w, w_conv_out, w_mla_out, w_o, ln2_g)


def _ffn_kernel(u_ref, u_prev_ref, h_ref, wg_ref, wu_ref, cwg_ref, cwu_ref, bg_ref, bu_ref, wd_ref,
                out_ref, lhs_ref, pg_ref, pu_ref, *, tm, row_chunk, tiles_per_seq):
    @pl.when(pl.program_id(1) == 0)
    def _():
        seq_start = (pl.program_id(0) % tiles_per_seq) == 0
        lhs_ref[0:HALO, :] = jnp.where(seq_start, jnp.zeros_like(u_prev_ref), u_prev_ref[...])
        lhs_ref[HALO:, :] = u_ref[...]
        out_ref[...] = h_ref[...]

    bounds = [0] + [HALO + (c + 1) * row_chunk for c in range(tm // row_chunk)]
    for lo, hi in zip(bounds[:-1], bounds[1:]):
        pg_ref[lo:hi, :] = _dot(lhs_ref[lo:hi, :], wg_ref[...])
        pu_ref[lo:hi, :] = _dot(lhs_ref[lo:hi, :], wu_ref[...])

    def conv(p_ref, cw_ref, b_ref, r0):
        cw = cw_ref[...]
        return (cw[0:1, :] * p_ref[r0 + HALO - 2:r0 + HALO - 2 + row_chunk, :]
                + cw[1:2, :] * p_ref[r0 + HALO - 1:r0 + HALO - 1 + row_chunk, :]
                + cw[2:3, :] * p_ref[r0 + HALO:r0 + HALO + row_chunk, :]) + b_ref[...]

    for r0 in range(0, tm, row_chunk):
        a_gate = conv(pg_ref, cwg_ref, bg_ref, r0)
        a_up = conv(pu_ref, cwu_ref, bu_ref, r0)
        act = (a_gate * jax.nn.sigmoid(a_gate) * a_up).astype(BF16)
        out_ref[r0:r0 + row_chunk, :] += _dot(act, wd_ref[...])


def _ffn(u2, h1, w_up, conv_w, conv_b, w_down, seq):
    t = u2.shape[0]
    tm, tf = TM_FFN, TF_FFN
    nf = D_FF // tf
    halo_per_tile = tm // HALO
    return pl.pallas_call(
        functools.partial(_ffn_kernel, tm=tm, row_chunk=FFN_ROW_CHUNK, tiles_per_seq=seq // tm),
        grid=(t // tm, nf),
        in_specs=[
            pl.BlockSpec((tm, D_MODEL), lambda i, j: (i, 0)),
            pl.BlockSpec((HALO, D_MODEL), lambda i, j: (jnp.maximum(i * halo_per_tile - 1, 0), 0)),
            pl.BlockSpec((tm, D_MODEL), lambda i, j: (i, 0)),
            pl.BlockSpec((D_MODEL, tf), lambda i, j: (0, j)),
            pl.BlockSpec((D_MODEL, tf), lambda i, j: (0, j + nf)),
            pl.BlockSpec((CONV_WIDTH, tf), lambda i, j: (0, j)),
            pl.BlockSpec((CONV_WIDTH, tf), lambda i, j: (0, j + nf)),
            pl.BlockSpec((1, tf), lambda i, j: (0, j)),
            pl.BlockSpec((1, tf), lambda i, j: (0, j + nf)),
            pl.BlockSpec((tf, D_MODEL), lambda i, j: (j, 0)),
        ],
        out_specs=pl.BlockSpec((tm, D_MODEL), lambda i, j: (i, 0)),
        out_shape=jax.ShapeDtypeStruct((t, D_MODEL), F32),
        scratch_shapes=[
            pltpu.VMEM((tm + HALO, D_MODEL), BF16),
            pltpu.VMEM((tm + HALO, tf), F32),
            pltpu.VMEM((tm + HALO, tf), F32),
        ],
        compiler_params=_params("arbitrary", "arbitrary"),
        name="ffn",
    )(u2, u2, h1, w_up, w_up, conv_w, conv_w, conv_b, conv_b, w_down)


def _row(v):
    return v.reshape(1, -1).astype(F32)


def _layer(h2, pos, inv_freq, batch, seq, ln1_g, w_in, b_gate, conv_w, w_conv_out, q_a_g, w_q_b, kv_a_g, w_kv_b,
           q_norm_g, k_norm_g, w_mla_out, w_o, ln2_g, w_ffn_up, ffn_conv_w, ffn_conv_b, w_ffn_down):
    w_in_t = w_in.T.astype(BF16)
    wq3 = w_q_b.reshape(Q_LORA_RANK, N_HEADS, QK_HEAD_DIM)
    w_qn = wq3[:, :, :QK_NOPE_DIM].reshape(Q_LORA_RANK, -1).astype(BF16)
    w_qr = wq3[:, :, QK_NOPE_DIM:].reshape(Q_LORA_RANK, -1).astype(BF16)
    wkv3 = w_kv_b.reshape(KV_LORA_RANK, N_HEADS, QK_NOPE_DIM + V_HEAD_DIM)
    w_kn = wkv3[:, :, :QK_NOPE_DIM].reshape(KV_LORA_RANK, -1).astype(BF16)
    w_v = wkv3[:, :, QK_NOPE_DIM:].reshape(KV_LORA_RANK, -1).astype(BF16)
    twice = lambda g: _row(jnp.concatenate([g, g]))

    q, k, vt, u = _mla_prep(h2, pos, _row(ln1_g), w_in_t, _row(q_a_g), _row(kv_a_g), w_qn, w_qr, w_kn, w_v,
                            _row(q_norm_g[:QK_NOPE_DIM]), twice(q_norm_g[QK_NOPE_DIM:]),
                            _row(k_norm_g[:QK_NOPE_DIM]), twice(k_norm_g[QK_NOPE_DIM:]), inv_freq)
    z = _in_proj(u, w_in_t)
    o = _attn(q, k, vt, _row(q_norm_g), _row(k_norm_g), batch, seq)
    h1, u2 = _mix_out(z, o, h2, _row(b_gate), conv_w.astype(F32), w_conv_out.astype(BF16),
                      w_mla_out.astype(BF16), w_o.astype(BF16), _row(ln2_g), seq)
    return _ffn(u2, h1, w_ffn_up.astype(BF16), ffn_conv_w.astype(F32), _row(ffn_conv_b),
                w_ffn_down.astype(BF16), seq)


def kernel(x, positions, ln1_g, w_in, b_gate, conv_w, w_conv_out, q_a_g, w_q_b, kv_a_g, w_kv_b,
           q_norm_g, k_norm_g, w_mla_out, w_o, ln2_g, w_ffn_up, ffn_conv_w, ffn_conv_b, w_ffn_down):
    batch, seq, _ = x.shape
    depth = ln1_g.shape[0]
    h2 = x.reshape(batch * seq, D_MODEL)
    pos = positions.reshape(batch * seq, 1).astype(jnp.int32)
    freq = ROPE_THETA ** (-jnp.arange(0, QK_ROPE_DIM, 2, dtype=F32) / QK_ROPE_DIM)
    inv_freq = jnp.tile(freq, LANES // (QK_ROPE_DIM // 2)).reshape(1, LANES)
    for l in range(depth):
        h2 = _layer(h2, pos, inv_freq, batch, seq, ln1_g[l], w_in[l], b_gate[l], conv_w[l], w_conv_out[l],
                    q_a_g[l], w_q_b[l], kv_a_g[l], w_kv_b[l], q_norm_g[l], k_norm_g[l], w_mla_out[l],
                    w_o[l], ln2_g[l], w_ffn_up[l], ffn_conv_w[l], ffn_conv_b[l], w_ffn_down[l])
    return h2.reshape(batch, seq, D_MODEL)
```

```python
import functools
import math

import jax
import jax.numpy as jnp
from jax import lax
from jax.experimental import pallas as pl
from jax.experimental.pallas import tpu as pltpu

D_MODEL = 2048
CONV_DIM = 1024
CONV_WIDTH = 3
N_HEADS = 16
QK_NOPE_DIM = 128
QK_ROPE_DIM = 64
QK_HEAD_DIM = QK_NOPE_DIM + QK_ROPE_DIM
V_HEAD_DIM = 128
Q_LORA_RANK = 768
KV_LORA_RANK = 512
ROPE_THETA = 10000.0
D_FF = 5632
NORM_EPS = 1e-6
NEG_INF = -1e30

Q_SCALE = QK_HEAD_DIM ** -0.5 * math.log2(math.e)
FIXED_SHIFT_LIMIT = 48.0

LANES = 128
HEAD_SLOT = 2 * LANES
HALO = 16
LAT_OFFSET = 3 * CONV_DIM
LAT_WIDTH = 1536
GATE_OFFSET = LAT_OFFSET + Q_LORA_RANK + KV_LORA_RANK + QK_ROPE_DIM
Z_WIDTH = 2 * D_MODEL + 3 * CONV_DIM
VMEM_LIMIT = 56 * 1024 * 1024

TM_PREP = 256
TM_INPROJ = 2048
TN_INPROJ = 1024
TQ_ATTN = 1024
TK_ATTN = 512
HEADS_PER_STEP = 4
SCORES_AHEAD = 4
TM_MIX = 256
TM_FFN = 512
TF_FFN = 512
FFN_ROW_CHUNK = 256

F32 = jnp.float32
BF16 = jnp.bfloat16


def _dot(a, b):
    return jnp.dot(a, b, preferred_element_type=F32)


def _dot_nt(a, b):
    return lax.dot_general(a, b, (((1,), (1,)), ((), ())), preferred_element_type=F32)


def _rms_scale(x, width):
    return lax.rsqrt(jnp.sum(x * x, axis=-1, keepdims=True) * (1.0 / width) + NORM_EPS)


def _params(*sem):
    return pltpu.CompilerParams(dimension_semantics=sem, vmem_limit_bytes=VMEM_LIMIT)


def _const_spec(shape):
    return pl.BlockSpec(shape, lambda *_: (0,) * len(shape), pipeline_mode=pl.Buffered(1))


def _cast_sidecar(weights, n_steps, step_index):
    in_specs, out_specs, out_shape = [], [], []
    for w in weights:
        rows, cols = w.shape
        slab = rows // n_steps
        assert slab * n_steps == rows and slab % 16 == 0, (w.shape, n_steps)
        spec = pl.BlockSpec((slab, cols), lambda *ids: (step_index(*ids), 0))
        in_specs.append(spec)
        out_specs.append(spec)
        out_shape.append(jax.ShapeDtypeStruct(w.shape, BF16))
    return in_specs, out_specs, out_shape


def _cast_slabs(src_refs, dst_refs):
    for src, dst in zip(src_refs, dst_refs):
        dst[...] = src[...].astype(BF16)


def _causal_conv3(p, cw, rows):
    full = cw[2:3, :] * p + cw[1:2, :] * pltpu.roll(p, 1, axis=0) + cw[0:1, :] * pltpu.roll(p, 2, axis=0)
    return full[p.shape[0] - rows:, :]


def _mla_prep_kernel(x_ref, pos_ref, ln_g_ref, w_lat_ref, qa_g_ref, kva_g_ref, w_qn_ref, w_qr_ref,
                     w_kn_ref, w_v_ref, qg_nope_ref, qg_rope_ref, kg_nope_ref, kg_rope_ref, inv_freq_ref,
                     w_up_f32_ref, w_down_f32_ref,
                     q_ref, k_ref, vt_ref, u_ref, w_up_bf16_ref, w_down_bf16_ref,
                     qn_s, qr_s, kn_s, v_s, kr_s, *, q_scale):
    @pl.when(pl.program_id(0) == 0)
    def _():
        for s in (qn_s, qr_s, kn_s, v_s, kr_s):
            s[...] = jnp.zeros_like(s)

    _cast_slabs((w_up_f32_ref, w_down_f32_ref), (w_up_bf16_ref, w_down_bf16_ref))

    ang = pos_ref[...].astype(F32) * inv_freq_ref[...]
    cos = jnp.cos(ang)
    sin = jnp.sin(ang)
    lane = lax.broadcasted_iota(jnp.int32, (1, LANES), 1)
    first_half = (lane % QK_ROPE_DIM) < (QK_ROPE_DIM // 2)
    sin_signed = jnp.where(first_half, -sin, sin)

    def rope(t):
        rot = jnp.where(first_half, pltpu.roll(t, LANES - QK_ROPE_DIM // 2, axis=1),
                        pltpu.roll(t, QK_ROPE_DIM // 2, axis=1))
        return t * cos + rot * sin_signed

    k_rope = kr_s[...]
    k_rope = jnp.where(lane < QK_ROPE_DIM, k_rope, pltpu.roll(k_rope, QK_ROPE_DIM, axis=1))
    qg_nope = qg_nope_ref[...] * q_scale
    qg_rope = qg_rope_ref[...] * q_scale
    kg_nope = kg_nope_ref[...]
    k_tail = rope(k_rope * kg_rope_ref[...])
    k_rope_ss = 0.5 * jnp.sum(k_rope * k_rope, axis=-1, keepdims=True)

    for pair in range(N_HEADS // 2):
        q_pair = qr_s[:, pair * LANES:(pair + 1) * LANES]
        q_pair_roped = rope(q_pair * qg_rope)
        for sub in range(2):
            h = 2 * pair + sub
            mine = (lane // QK_ROPE_DIM) == sub
            q_nope = qn_s[:, h * QK_NOPE_DIM:(h + 1) * QK_NOPE_DIM]
            ss = (jnp.sum(q_nope * q_nope, axis=-1, keepdims=True)
                  + jnp.sum(jnp.where(mine, q_pair * q_pair, 0.0), axis=-1, keepdims=True))
            r = lax.rsqrt(ss * (1.0 / QK_HEAD_DIM) + NORM_EPS)
            q_ref[:, h * HEAD_SLOT: h * HEAD_SLOT + LANES] = (q_nope * r * qg_nope).astype(BF16)
            q_ref[:, h * HEAD_SLOT + LANES:(h + 1) * HEAD_SLOT] = jnp.where(
                mine, q_pair_roped * r, 0.0).astype(BF16)

    for h in range(N_HEADS):
        k_nope = kn_s[:, h * QK_NOPE_DIM:(h + 1) * QK_NOPE_DIM]
        kss = jnp.sum(k_nope * k_nope, axis=-1, keepdims=True) + k_rope_ss
        kr = lax.rsqrt(kss * (1.0 / QK_HEAD_DIM) + NORM_EPS)
        k_ref[:, h * HEAD_SLOT: h * HEAD_SLOT + LANES] = (k_nope * kr * kg_nope).astype(BF16)
        k_ref[:, h * HEAD_SLOT + LANES:(h + 1) * HEAD_SLOT] = (k_tail * kr).astype(BF16)

    vt_ref[0] = v_s[...].T

    xf = x_ref[...]
    u = (xf * _rms_scale(xf, D_MODEL) * ln_g_ref[...]).astype(BF16)
    u_ref[...] = u
    lat = _dot_nt(u, w_lat_ref[...])
    q_lat = lat[:, :Q_LORA_RANK]
    kv_lat = lat[:, Q_LORA_RANK:Q_LORA_RANK + KV_LORA_RANK]
    kr_s[...] = lat[:, Q_LORA_RANK + KV_LORA_RANK:Q_LORA_RANK + KV_LORA_RANK + LANES]
    qn = (q_lat * _rms_scale(q_lat, Q_LORA_RANK) * qa_g_ref[...]).astype(BF16)
    kvn = (kv_lat * _rms_scale(kv_lat, KV_LORA_RANK) * kva_g_ref[...]).astype(BF16)
    qn_s[...] = _dot(qn, w_qn_ref[...])
    qr_s[...] = _dot(qn, w_qr_ref[...])
    kn_s[...] = _dot(kvn, w_kn_ref[...])
    v_s[...] = _dot(kvn, w_v_ref[...]).astype(BF16)


def _mla_prep(x2, pos, ln_g, w_lat, qa_g, kva_g, w_qn, w_qr, w_kn, w_v, qg_nope, qg_rope, kg_nope, kg_rope,
              inv_freq, ffn_weights):
    t = x2.shape[0]
    tm = TM_PREP
    n = t // tm
    row = lambda i: (jnp.minimum(i, n - 1), 0)
    prev = lambda i: (jnp.maximum(i - 1, 0), 0)
    cast_in, cast_out, cast_shape = _cast_sidecar(ffn_weights, n, lambda i: jnp.minimum(i, n - 1))
    return pl.pallas_call(
        functools.partial(_mla_prep_kernel, q_scale=Q_SCALE),
        grid=(n + 1,),
        in_specs=[
            pl.BlockSpec((tm, D_MODEL), row),
            pl.BlockSpec((tm, 1), prev),
            _const_spec((1, D_MODEL)),
            pl.BlockSpec((pl.Element(LAT_WIDTH), pl.Element(D_MODEL)), lambda i: (LAT_OFFSET, 0),
                         pipeline_mode=pl.Buffered(1)),
            _const_spec((1, Q_LORA_RANK)),
            _const_spec((1, KV_LORA_RANK)),
            _const_spec(w_qn.shape), _const_spec(w_qr.shape),
            _const_spec(w_kn.shape), _const_spec(w_v.shape),
            _const_spec((1, LANES)), _const_spec((1, LANES)),
            _const_spec((1, LANES)), _const_spec((1, LANES)),
            _const_spec((1, LANES)),
        ] + cast_in,
        out_specs=[
            pl.BlockSpec((tm, N_HEADS * HEAD_SLOT), prev),
            pl.BlockSpec((tm, N_HEADS * HEAD_SLOT), prev),
            pl.BlockSpec((1, N_HEADS * V_HEAD_DIM, tm), lambda i: (jnp.maximum(i - 1, 0), 0, 0)),
            pl.BlockSpec((tm, D_MODEL), row),
        ] + cast_out,
        out_shape=[
            jax.ShapeDtypeStruct((t, N_HEADS * HEAD_SLOT), BF16),
            jax.ShapeDtypeStruct((t, N_HEADS * HEAD_SLOT), BF16),
            jax.ShapeDtypeStruct((n, N_HEADS * V_HEAD_DIM, tm), BF16),
            jax.ShapeDtypeStruct((t, D_MODEL), BF16),
        ] + cast_shape,
        scratch_shapes=[
            pltpu.VMEM((tm, N_HEADS * QK_NOPE_DIM), F32),
            pltpu.VMEM((tm, N_HEADS * QK_ROPE_DIM), F32),
            pltpu.VMEM((tm, N_HEADS * QK_NOPE_DIM), F32),
            pltpu.VMEM((tm, N_HEADS * V_HEAD_DIM), BF16),
            pltpu.VMEM((tm, LANES), F32),
        ],
        compiler_params=_params("arbitrary"),
        name="mla_prep",
    )(x2, pos, ln_g, w_lat, qa_g, kva_g, w_qn, w_qr, w_kn, w_v, qg_nope, qg_rope, kg_nope, kg_rope, inv_freq,
      *ffn_weights)


def _in_proj_kernel(u_ref, w_ref, z_ref):
    z_ref[...] = _dot_nt(u_ref[...], w_ref[...]).astype(BF16)


def _in_proj(u, w_in_t):
    t = u.shape[0]
    tm, tn = TM_INPROJ, TN_INPROJ
    gate_tiles = 2 * D_MODEL // tn

    def w_rows(i, j):
        rows = jnp.where(j < gate_tiles, GATE_OFFSET + j * tn, (j - gate_tiles) * tn)
        return pl.multiple_of(rows, math.gcd(GATE_OFFSET, tn)), 0

    return pl.pallas_call(
        _in_proj_kernel,
        grid=(t // tm, Z_WIDTH // tn),
        in_specs=[
            pl.BlockSpec((tm, D_MODEL), lambda i, j: (i, 0)),
            pl.BlockSpec((pl.Element(tn), pl.Element(D_MODEL)), w_rows),
        ],
        out_specs=pl.BlockSpec((tm, tn), lambda i, j: (i, j)),
        out_shape=jax.ShapeDtypeStruct((t, Z_WIDTH), BF16),
        compiler_params=_params("arbitrary", "arbitrary"),
        name="in_proj",
    )(u, w_in_t)


def _attn_kernel(qg_ref, kg_ref, q_ref, k_ref, vt_ref, wc_f32_ref, wm_f32_ref, wo_f32_ref,
                 o_ref, wc_bf16_ref, wm_bf16_ref, wo_bf16_ref, *, tq, tk, vt_cols, q_scale):
    i = pl.program_id(2)
    sub_blocks = tk // vt_cols
    kv_per_q = tq // tk
    gq = jnp.max(jnp.abs(qg_ref[...]), axis=-1, keepdims=True)
    gk = jnp.max(jnp.abs(kg_ref[...]), axis=-1, keepdims=True)
    bound = (gq * gk)[0, 0] * (q_scale * QK_HEAD_DIM)
    fixed_shift = bound <= FIXED_SHIFT_LIMIT
    _cast_slabs((wc_f32_ref, wm_f32_ref, wo_f32_ref), (wc_bf16_ref, wm_bf16_ref, wo_bf16_ref))

    def score(j, hh, col0):
        start = pl.multiple_of(j * tk, tk)
        q = q_ref[col0:, hh * HEAD_SLOT:(hh + 1) * HEAD_SLOT]
        kb = k_ref[pl.ds(start, tk), hh * HEAD_SLOT:(hh + 1) * HEAD_SLOT]
        return _dot_nt(kb, q)

    def per_head(j, col0, fn):
        sc = {hh: score(j, hh, col0) for hh in range(SCORES_AHEAD)}
        out = []
        for hh in range(HEADS_PER_STEP):
            out.append(fn(hh, sc.pop(hh)))
            if hh + SCORES_AHEAD < HEADS_PER_STEP:
                sc[hh + SCORES_AHEAD] = score(j, hh + SCORES_AHEAD, col0)
        return tuple(out)

    def causal(s):
        kv_idx = lax.broadcasted_iota(jnp.int32, s.shape, 0)
        q_idx = lax.broadcasted_iota(jnp.int32, s.shape, 1)
        return jnp.where(kv_idx <= q_idx, s, NEG_INF)

    def p_times_v(j, hh, p):
        pb = p.astype(BF16)
        pv = None
        for c in range(sub_blocks):
            vt = vt_ref[j * sub_blocks + c, hh * V_HEAD_DIM:(hh + 1) * V_HEAD_DIM, :]
            part = _dot(vt, pb[c * vt_cols:(c + 1) * vt_cols, :])
            pv = part if pv is None else pv + part
        return pv

    def tail(old, new, col0):
        return new if col0 == 0 else jnp.concatenate([old[:, :col0], new], axis=1)

    def sweep(step, init):
        carry = lax.fori_loop(0, i * kv_per_q, lambda j, c: step(j, c, 0, False), init)
        for d in range(kv_per_q):
            carry = step(i * kv_per_q + d, carry, d * tk, True)
        return carry

    def write(hh, l, acc):
        o_ref[:, hh * V_HEAD_DIM:(hh + 1) * V_HEAD_DIM] = (acc / l).T.astype(BF16)

    @pl.when(fixed_shift)
    def _():
        def step(j, carry, col0, masked):
            def head(hh, s):
                l, acc = carry[hh]
                p = jnp.exp2((causal(s) if masked else s) - bound)
                return (tail(l, l[:, col0:] + jnp.sum(p, axis=0, keepdims=True), col0),
                        tail(acc, acc[:, col0:] + p_times_v(j, hh, p), col0))

            return per_head(j, col0, head)

        carry = sweep(step, tuple((jnp.zeros((1, tq), F32), jnp.zeros((V_HEAD_DIM, tq), F32))
                                  for _ in range(HEADS_PER_STEP)))
        for hh in range(HEADS_PER_STEP):
            write(hh, *carry[hh])

    @pl.when(jnp.logical_not(fixed_shift))
    def _():
        def step(j, carry, col0, masked):
            def head(hh, s):
                m, l, acc = (t[:, col0:] for t in carry[hh])
                s = causal(s) if masked else s
                m_new = jnp.maximum(m, jnp.max(s, axis=0, keepdims=True))
                alpha = jnp.exp2(m - m_new)
                p = jnp.exp2(s - m_new)
                new = (m_new, alpha * l + jnp.sum(p, axis=0, keepdims=True), alpha * acc + p_times_v(j, hh, p))
                return tuple(tail(o, n, col0) for o, n in zip(carry[hh], new))

            return per_head(j, col0, head)

        carry = sweep(step, tuple((jnp.full((1, tq), NEG_INF, F32), jnp.zeros((1, tq), F32),
                                   jnp.zeros((V_HEAD_DIM, tq), F32)) for _ in range(HEADS_PER_STEP)))
        for hh in range(HEADS_PER_STEP):
            write(hh, *carry[hh][1:])


def _attn(q, k, vt, q_norm_g, k_norm_g, mix_weights, batch, seq):
    tq, tk = TQ_ATTN, TK_ATTN
    nq = seq // tq
    vt_cols = vt.shape[-1]
    hs = HEADS_PER_STEP
    groups = N_HEADS // hs
    cast_in, cast_out, cast_shape = _cast_sidecar(mix_weights, batch * groups * nq,
                                                  lambda b, h, i: (b * groups + h) * nq + i)
    return pl.pallas_call(
        functools.partial(_attn_kernel, tq=tq, tk=tk, vt_cols=vt_cols, q_scale=Q_SCALE),
        grid=(batch, groups, nq),
        in_specs=[
            _const_spec((1, QK_HEAD_DIM)),
            _const_spec((1, QK_HEAD_DIM)),
            pl.BlockSpec((tq, hs * HEAD_SLOT), lambda b, h, i: (b * nq + i, h)),
            pl.BlockSpec((seq, hs * HEAD_SLOT), lambda b, h, i: (b, h)),
            pl.BlockSpec((seq // vt_cols, hs * V_HEAD_DIM, vt_cols), lambda b, h, i: (b, h, 0)),
        ] + cast_in,
        out_specs=[pl.BlockSpec((tq, hs * V_HEAD_DIM), lambda b, h, i: (b * nq + i, h))] + cast_out,
        out_shape=[jax.ShapeDtypeStruct((batch * seq, N_HEADS * V_HEAD_DIM), BF16)] + cast_shape,
        compiler_params=_params("arbitrary", "arbitrary", "arbitrary"),
        name="attn",
    )(q_norm_g, k_norm_g, q, k, vt, *mix_weights)


def _mix_out_kernel(gates_ref, zb_ref, zc_ref, zv_ref, zc_prev_ref, zv_prev_ref, o_ref, x_ref,
                    b_gate_ref, conv_w_ref, w_conv_out_ref, w_mla_out_ref, w_o_ref, ln2_g_ref,
                    h_ref, u2_ref, *, tm, tiles_per_seq):
    seq_start = (pl.program_id(0) % tiles_per_seq) == 0
    prev = zc_prev_ref[...].astype(F32) * zv_prev_ref[...].astype(F32)
    cv = jnp.concatenate([jnp.where(seq_start, 0.0, prev), zc_ref[...].astype(F32) * zv_ref[...].astype(F32)], axis=0)
    conv = _causal_conv3(cv, conv_w_ref[...], tm)
    conv_in = (zb_ref[...].astype(F32) * conv).astype(BF16)
    y_conv = _dot(conv_in, w_conv_out_ref[...])
    y_mla = _dot(o_ref[...], w_mla_out_ref[...])
    gates = jax.nn.sigmoid(gates_ref[...].astype(F32) + b_gate_ref[...])
    mix = (gates[:, :D_MODEL] * y_conv + gates[:, D_MODEL:] * y_mla).astype(BF16)
    h = x_ref[...] + _dot(mix, w_o_ref[...])
    h_ref[...] = h
    u2_ref[...] = (h * _rms_scale(h, D_MODEL) * ln2_g_ref[...]).astype(BF16)


def _mix_out(z, o, x2, b_gate, conv_w, w_conv_out, w_mla_out, w_o, ln2_g, seq):
    t = x2.shape[0]
    tm = TM_MIX
    halo_per_tile = tm // HALO
    cblk = lambda c: (lambda i: (i, c))
    prev = lambda c: (lambda i: (jnp.maximum(i * halo_per_tile - 1, 0), c))
    gate_blocks = 2 * D_MODEL // CONV_DIM
    return pl.pallas_call(
        functools.partial(_mix_out_kernel, tm=tm, tiles_per_seq=seq // tm),
        grid=(t // tm,),
        in_specs=[
            pl.BlockSpec((tm, 2 * D_MODEL), cblk(0)),
            pl.BlockSpec((tm, CONV_DIM), cblk(gate_blocks)),
            pl.BlockSpec((tm, CONV_DIM), cblk(gate_blocks + 1)),
            pl.BlockSpec((tm, CONV_DIM), cblk(gate_blocks + 2)),
            pl.BlockSpec((HALO, CONV_DIM), prev(gate_blocks + 1)),
            pl.BlockSpec((HALO, CONV_DIM), prev(gate_blocks + 2)),
            pl.BlockSpec((tm, D_MODEL), cblk(0)),
            pl.BlockSpec((tm, D_MODEL), cblk(0)),
            _const_spec((1, 2 * D_MODEL)),
            _const_spec((CONV_WIDTH, CONV_DIM)),
            _const_spec((CONV_DIM, D_MODEL)),
            _const_spec((D_MODEL, D_MODEL)),
            _const_spec((D_MODEL, D_MODEL)),
            _const_spec((1, D_MODEL)),
        ],
        out_specs=[pl.BlockSpec((tm, D_MODEL), cblk(0)), pl.BlockSpec((tm, D_MODEL), cblk(0))],
        out_shape=[jax.ShapeDtypeStruct((t, D_MODEL), F32), jax.ShapeDtypeStruct((t, D_MODEL), BF16)],
        compiler_params=_params("arbitrary"),
        name="mix_out",
    )(z, z, z, z, z, z, o, x2, b_gate, conv_w, w_conv_out, w_mla_out, w_o, ln2_g)


def _ffn_kernel(u_ref, u_prev_ref, h_ref, wg_ref, wu_ref, cwg_ref, cwu_ref, bg_ref, bu_ref, wd_ref,
                out_ref, lhs_ref, pg_ref, pu_ref, *, tm, row_chunk, tiles_per_seq):
    @pl.when(pl.program_id(1) == 0)
    def _():
        seq_start = (pl.program_id(0) % tiles_per_seq) == 0
        lhs_ref[0:HALO, :] = jnp.where(seq_start, jnp.zeros_like(u_prev_ref), u_prev_ref[...])
        lhs_ref[HALO:, :] = u_ref[...]
        out_ref[...] = h_ref[...]

    bounds = [0] + [HALO + (c + 1) * row_chunk for c in range(tm // row_chunk)]
    for lo, hi in zip(bounds[:-1], bounds[1:]):
        pg_ref[lo:hi, :] = _dot(lhs_ref[lo:hi, :], wg_ref[...])
        pu_ref[lo:hi, :] = _dot(lhs_ref[lo:hi, :], wu_ref[...])

    def conv(p_ref, cw_ref, b_ref, r0):
        cw = cw_ref[...]
        return (cw[0:1, :] * p_ref[r0 + HALO - 2:r0 + HALO - 2 + row_chunk, :]
                + cw[1:2, :] * p_ref[r0 + HALO - 1:r0 + HALO - 1 + row_chunk, :]
                + cw[2:3, :] * p_ref[r0 + HALO:r0 + HALO + row_chunk, :]) + b_ref[...]

    for r0 in range(0, tm, row_chunk):
        a_gate = conv(pg_ref, cwg_ref, bg_ref, r0)
        a_up = conv(pu_ref, cwu_ref, bu_ref, r0)
        act = (a_gate * jax.nn.sigmoid(a_gate) * a_up).astype(BF16)
        out_ref[r0:r0 + row_chunk, :] += _dot(act, wd_ref[...])


def _ffn(u2, h1, w_up, conv_w, conv_b, w_down, seq):
    t = u2.shape[0]
    tm, tf = TM_FFN, TF_FFN
    nf = D_FF // tf
    halo_per_tile = tm // HALO
    return pl.pallas_call(
        functools.partial(_ffn_kernel, tm=tm, row_chunk=FFN_ROW_CHUNK, tiles_per_seq=seq // tm),
        grid=(t // tm, nf),
        in_specs=[
            pl.BlockSpec((tm, D_MODEL), lambda i, j: (i, 0)),
            pl.BlockSpec((HALO, D_MODEL), lambda i, j: (jnp.maximum(i * halo_per_tile - 1, 0), 0)),
            pl.BlockSpec((tm, D_MODEL), lambda i, j: (i, 0)),
            pl.BlockSpec((D_MODEL, tf), lambda i, j: (0, j)),
            pl.BlockSpec((D_MODEL, tf), lambda i, j: (0, j + nf)),
            pl.BlockSpec((CONV_WIDTH, tf), lambda i, j: (0, j)),
            pl.BlockSpec((CONV_WIDTH, tf), lambda i, j: (0, j + nf)),
            pl.BlockSpec((1, tf), lambda i, j: (0, j)),
            pl.BlockSpec((1, tf), lambda i, j: (0, j + nf)),
            pl.BlockSpec((tf, D_MODEL), lambda i, j: (j, 0)),
        ],
        out_specs=pl.BlockSpec((tm, D_MODEL), lambda i, j: (i, 0)),
        out_shape=jax.ShapeDtypeStruct((t, D_MODEL), F32),
        scratch_shapes=[
            pltpu.VMEM((tm + HALO, D_MODEL), BF16),
            pltpu.VMEM((tm + HALO, tf), F32),
            pltpu.VMEM((tm + HALO, tf), F32),
        ],
        compiler_params=_params("arbitrary", "arbitrary"),
        name="ffn",
    )(u2, u2, h1, w_up, w_up, conv_w, conv_w, conv_b, conv_b, w_down)


def _row(v):
    return v.reshape(1, -1).astype(F32)


def _layer(h2, pos, inv_freq, batch, seq, ln1_g, w_in, b_gate, conv_w, w_conv_out, q_a_g, w_q_b, kv_a_g, w_kv_b,
           q_norm_g, k_norm_g, w_mla_out, w_o, ln2_g, w_ffn_up, ffn_conv_w, ffn_conv_b, w_ffn_down):
    w_in_t = w_in.T.astype(BF16)
    wq3 = w_q_b.reshape(Q_LORA_RANK, N_HEADS, QK_HEAD_DIM)
    w_qn = wq3[:, :, :QK_NOPE_DIM].reshape(Q_LORA_RANK, -1).astype(BF16)
    w_qr = wq3[:, :, QK_NOPE_DIM:].reshape(Q_LORA_RANK, -1).astype(BF16)
    wkv3 = w_kv_b.reshape(KV_LORA_RANK, N_HEADS, QK_NOPE_DIM + V_HEAD_DIM)
    w_kn = wkv3[:, :, :QK_NOPE_DIM].reshape(KV_LORA_RANK, -1).astype(BF16)
    w_v = wkv3[:, :, QK_NOPE_DIM:].reshape(KV_LORA_RANK, -1).astype(BF16)
    twice = lambda g: _row(jnp.concatenate([g, g]))

    q, k, vt, u, w_up, w_down = _mla_prep(
        h2, pos, _row(ln1_g), w_in_t, _row(q_a_g), _row(kv_a_g), w_qn, w_qr, w_kn, w_v,
        _row(q_norm_g[:QK_NOPE_DIM]), twice(q_norm_g[QK_NOPE_DIM:]),
        _row(k_norm_g[:QK_NOPE_DIM]), twice(k_norm_g[QK_NOPE_DIM:]), inv_freq, (w_ffn_up, w_ffn_down))
    z = _in_proj(u, w_in_t)
    o, w_conv_out, w_mla_out, w_o = _attn(q, k, vt, _row(q_norm_g), _row(k_norm_g),
                                          (w_conv_out, w_mla_out, w_o), batch, seq)
    h1, u2 = _mix_out(z, o, h2, _row(b_gate), conv_w.astype(F32), w_conv_out, w_mla_out, w_o, _row(ln2_g), seq)
    return _ffn(u2, h1, w_up, ffn_conv_w.astype(F32), _row(ffn_conv_b), w_down, seq)


def kernel(x, positions, ln1_g, w_in, b_gate, conv_w, w_conv_out, q_a_g, w_q_b, kv_a_g, w_kv_b,
           q_norm_g, k_norm_g, w_mla_out, w_o, ln2_g, w_ffn_up, ffn_conv_w, ffn_conv_b, w_ffn_down):
    batch, seq, _ = x.shape
    depth = ln1_g.shape[0]
    h2 = x.reshape(batch * seq, D_MODEL)
    pos = positions.reshape(batch * seq, 1).astype(jnp.int32)
    freq = ROPE_THETA ** (-jnp.arange(0, QK_ROPE_DIM, 2, dtype=F32) / QK_ROPE_DIM)
    inv_freq = jnp.tile(freq, LANES // (QK_ROPE_DIM // 2)).reshape(1, LANES)
    for l in range(depth):
        h2 = _layer(h2, pos, inv_freq, batch, seq, ln1_g[l], w_in[l], b_gate[l], conv_w[l], w_conv_out[l],
                    q_a_g[l], w_q_b[l], kv_a_g[l], w_kv_b[l], q_norm_g[l], k_norm_g[l], w_mla_out[l],
                    w_o[l], ln2_g[l], w_ffn_up[l], ffn_conv_w[l], ffn_conv_b[l], w_ffn_down[l])
    return h2.reshape(batch, seq, D_MODEL)
```

```python
import functools
import math

import jax
import jax.numpy as jnp
from jax import lax
from jax.experimental import pallas as pl
from jax.experimental.pallas import tpu as pltpu

D_MODEL = 2048
CONV_DIM = 1024
CONV_WIDTH = 3
N_HEADS = 16
QK_NOPE_DIM = 128
QK_ROPE_DIM = 64
QK_HEAD_DIM = QK_NOPE_DIM + QK_ROPE_DIM
V_HEAD_DIM = 128
Q_LORA_RANK = 768
KV_LORA_RANK = 512
ROPE_THETA = 10000.0
D_FF = 5632
NORM_EPS = 1e-6
NEG_INF = -1e30

Q_SCALE = QK_HEAD_DIM ** -0.5 * math.log2(math.e)
FIXED_SHIFT_LIMIT = 48.0

LANES = 128
HEAD_SLOT = 2 * LANES
HALO = 16
LAT_OFFSET = 3 * CONV_DIM
LAT_WIDTH = 1536
GATE_OFFSET = LAT_OFFSET + Q_LORA_RANK + KV_LORA_RANK + QK_ROPE_DIM
Z_WIDTH = 2 * D_MODEL + 3 * CONV_DIM
VMEM_LIMIT = 56 * 1024 * 1024

TM_PREP = 256
TM_INPROJ = 2048
TN_INPROJ = 1024
TQ_ATTN = 1024
TK_ATTN = 512
HEADS_PER_STEP = 4
SCORES_AHEAD = 4
TM_MIX = 256
TM_FFN = 512
TF_FFN = 512
FFN_ROW_CHUNK = 256

F32 = jnp.float32
BF16 = jnp.bfloat16


def _dot(a, b):
    return jnp.dot(a, b, preferred_element_type=F32)


def _dot_nt(a, b):
    return lax.dot_general(a, b, (((1,), (1,)), ((), ())), preferred_element_type=F32)


def _rms_scale(x, width):
    return lax.rsqrt(jnp.sum(x * x, axis=-1, keepdims=True) * (1.0 / width) + NORM_EPS)


def _params(*sem):
    return pltpu.CompilerParams(dimension_semantics=sem, vmem_limit_bytes=VMEM_LIMIT)


def _const_spec(shape):
    return pl.BlockSpec(shape, lambda *_: (0,) * len(shape), pipeline_mode=pl.Buffered(1))


def _cast_sidecar(weights, n_steps, step_index):
    in_specs, out_specs, out_shape = [], [], []
    for w in weights:
        rows, cols = w.shape
        slab = rows // n_steps
        assert slab * n_steps == rows and slab % 16 == 0, (w.shape, n_steps)
        spec = pl.BlockSpec((slab, cols), lambda *ids: (step_index(*ids), 0))
        in_specs.append(spec)
        out_specs.append(spec)
        out_shape.append(jax.ShapeDtypeStruct(w.shape, BF16))
    return in_specs, out_specs, out_shape


def _cast_slabs(src_refs, dst_refs):
    for src, dst in zip(src_refs, dst_refs):
        dst[...] = src[...].astype(BF16)


def _causal_conv3(p, cw, rows):
    full = cw[2:3, :] * p + cw[1:2, :] * pltpu.roll(p, 1, axis=0) + cw[0:1, :] * pltpu.roll(p, 2, axis=0)
    return full[p.shape[0] - rows:, :]


def _mla_prep_kernel(x_ref, pos_ref, ln_g_ref, w_lat_ref, qa_g_ref, kva_g_ref, w_qn_ref, w_qr_ref,
                     w_kn_ref, w_v_ref, qg_nope_ref, qg_rope_ref, kg_nope_ref, kg_rope_ref, inv_freq_ref,
                     w_up_f32_ref, w_down_f32_ref,
                     q_ref, k_ref, vt_ref, u_ref, w_up_bf16_ref, w_down_bf16_ref,
                     qn_s, qr_s, kn_s, v_s, kr_s, *, q_scale):
    @pl.when(pl.program_id(0) == 0)
    def _():
        for s in (qn_s, qr_s, kn_s, v_s, kr_s):
            s[...] = jnp.zeros_like(s)

    _cast_slabs((w_up_f32_ref, w_down_f32_ref), (w_up_bf16_ref, w_down_bf16_ref))

    ang = pos_ref[...].astype(F32) * inv_freq_ref[...]
    cos = jnp.cos(ang)
    sin = jnp.sin(ang)
    lane = lax.broadcasted_iota(jnp.int32, (1, LANES), 1)
    first_half = (lane % QK_ROPE_DIM) < (QK_ROPE_DIM // 2)
    sin_signed = jnp.where(first_half, -sin, sin)

    def rope(t):
        rot = jnp.where(first_half, pltpu.roll(t, LANES - QK_ROPE_DIM // 2, axis=1),
                        pltpu.roll(t, QK_ROPE_DIM // 2, axis=1))
        return t * cos + rot * sin_signed

    k_rope = kr_s[...]
    k_rope = jnp.where(lane < QK_ROPE_DIM, k_rope, pltpu.roll(k_rope, QK_ROPE_DIM, axis=1))
    qg_nope = qg_nope_ref[...] * q_scale
    qg_rope = qg_rope_ref[...] * q_scale
    kg_nope = kg_nope_ref[...]
    k_tail = rope(k_rope * kg_rope_ref[...])
    k_rope_ss = 0.5 * jnp.sum(k_rope * k_rope, axis=-1, keepdims=True)

    for pair in range(N_HEADS // 2):
        q_pair = qr_s[:, pair * LANES:(pair + 1) * LANES]
        q_pair_roped = rope(q_pair * qg_rope)
        for sub in range(2):
            h = 2 * pair + sub
            mine = (lane // QK_ROPE_DIM) == sub
            q_nope = qn_s[:, h * QK_NOPE_DIM:(h + 1) * QK_NOPE_DIM]
            ss = (jnp.sum(q_nope * q_nope, axis=-1, keepdims=True)
                  + jnp.sum(jnp.where(mine, q_pair * q_pair, 0.0), axis=-1, keepdims=True))
            r = lax.rsqrt(ss * (1.0 / QK_HEAD_DIM) + NORM_EPS)
            q_ref[:, h * HEAD_SLOT: h * HEAD_SLOT + LANES] = (q_nope * r * qg_nope).astype(BF16)
            q_ref[:, h * HEAD_SLOT + LANES:(h + 1) * HEAD_SLOT] = jnp.where(
                mine, q_pair_roped * r, 0.0).astype(BF16)

    for h in range(N_HEADS):
        k_nope = kn_s[:, h * QK_NOPE_DIM:(h + 1) * QK_NOPE_DIM]
        kss = jnp.sum(k_nope * k_nope, axis=-1, keepdims=True) + k_rope_ss
        kr = lax.rsqrt(kss * (1.0 / QK_HEAD_DIM) + NORM_EPS)
        k_ref[:, h * HEAD_SLOT: h * HEAD_SLOT + LANES] = (k_nope * kr * kg_nope).astype(BF16)
        k_ref[:, h * HEAD_SLOT + LANES:(h + 1) * HEAD_SLOT] = (k_tail * kr).astype(BF16)

    vt_ref[0] = v_s[...].T

    xf = x_ref[...]
    u = (xf * _rms_scale(xf, D_MODEL) * ln_g_ref[...]).astype(BF16)
    u_ref[...] = u
    lat = _dot_nt(u, w_lat_ref[...])
    q_lat = lat[:, :Q_LORA_RANK]
    kv_lat = lat[:, Q_LORA_RANK:Q_LORA_RANK + KV_LORA_RANK]
    kr_s[...] = lat[:, Q_LORA_RANK + KV_LORA_RANK:Q_LORA_RANK + KV_LORA_RANK + LANES]
    qn = (q_lat * _rms_scale(q_lat, Q_LORA_RANK) * qa_g_ref[...]).astype(BF16)
    kvn = (kv_lat * _rms_scale(kv_lat, KV_LORA_RANK) * kva_g_ref[...]).astype(BF16)
    qn_s[...] = _dot(qn, w_qn_ref[...])
    qr_s[...] = _dot(qn, w_qr_ref[...])
    kn_s[...] = _dot(kvn, w_kn_ref[...])
    v_s[...] = _dot(kvn, w_v_ref[...]).astype(BF16)


def _mla_prep(x2, pos, ln_g, w_lat, qa_g, kva_g, w_qn, w_qr, w_kn, w_v, qg_nope, qg_rope, kg_nope, kg_rope,
              inv_freq, ffn_weights):
    t = x2.shape[0]
    tm = TM_PREP
    n = t // tm
    row = lambda i: (jnp.minimum(i, n - 1), 0)
    prev = lambda i: (jnp.maximum(i - 1, 0), 0)
    cast_in, cast_out, cast_shape = _cast_sidecar(ffn_weights, n, lambda i: jnp.minimum(i, n - 1))
    return pl.pallas_call(
        functools.partial(_mla_prep_kernel, q_scale=Q_SCALE),
        grid=(n + 1,),
        in_specs=[
            pl.BlockSpec((tm, D_MODEL), row),
            pl.BlockSpec((tm, 1), prev),
            _const_spec((1, D_MODEL)),
            pl.BlockSpec((pl.Element(LAT_WIDTH), pl.Element(D_MODEL)), lambda i: (LAT_OFFSET, 0),
                         pipeline_mode=pl.Buffered(1)),
            _const_spec((1, Q_LORA_RANK)),
            _const_spec((1, KV_LORA_RANK)),
            _const_spec(w_qn.shape), _const_spec(w_qr.shape),
            _const_spec(w_kn.shape), _const_spec(w_v.shape),
            _const_spec((1, LANES)), _const_spec((1, LANES)),
            _const_spec((1, LANES)), _const_spec((1, LANES)),
            _const_spec((1, LANES)),
        ] + cast_in,
        out_specs=[
            pl.BlockSpec((tm, N_HEADS * HEAD_SLOT), prev),
            pl.BlockSpec((tm, N_HEADS * HEAD_SLOT), prev),
            pl.BlockSpec((1, N_HEADS * V_HEAD_DIM, tm), lambda i: (jnp.maximum(i - 1, 0), 0, 0)),
            pl.BlockSpec((tm, D_MODEL), row),
        ] + cast_out,
        out_shape=[
            jax.ShapeDtypeStruct((t, N_HEADS * HEAD_SLOT), BF16),
            jax.ShapeDtypeStruct((t, N_HEADS * HEAD_SLOT), BF16),
            jax.ShapeDtypeStruct((n, N_HEADS * V_HEAD_DIM, tm), BF16),
            jax.ShapeDtypeStruct((t, D_MODEL), BF16),
        ] + cast_shape,
        scratch_shapes=[
            pltpu.VMEM((tm, N_HEADS * QK_NOPE_DIM), F32),
            pltpu.VMEM((tm, N_HEADS * QK_ROPE_DIM), F32),
            pltpu.VMEM((tm, N_HEADS * QK_NOPE_DIM), F32),
            pltpu.VMEM((tm, N_HEADS * V_HEAD_DIM), BF16),
            pltpu.VMEM((tm, LANES), F32),
        ],
        compiler_params=_params("arbitrary"),
        name="mla_prep",
    )(x2, pos, ln_g, w_lat, qa_g, kva_g, w_qn, w_qr, w_kn, w_v, qg_nope, qg_rope, kg_nope, kg_rope, inv_freq,
      *ffn_weights)


def _in_proj_kernel(u_ref, w_ref, z_ref):
    z_ref[...] = _dot_nt(u_ref[...], w_ref[...]).astype(BF16)


def _in_proj(u, w_in_t):
    t = u.shape[0]
    tm, tn = TM_INPROJ, TN_INPROJ
    gate_tiles = 2 * D_MODEL // tn

    def w_rows(i, j):
        rows = jnp.where(j < gate_tiles, GATE_OFFSET + j * tn, (j - gate_tiles) * tn)
        return pl.multiple_of(rows, math.gcd(GATE_OFFSET, tn)), 0

    return pl.pallas_call(
        _in_proj_kernel,
        grid=(t // tm, Z_WIDTH // tn),
        in_specs=[
            pl.BlockSpec((tm, D_MODEL), lambda i, j: (i, 0)),
            pl.BlockSpec((pl.Element(tn), pl.Element(D_MODEL)), w_rows),
        ],
        out_specs=pl.BlockSpec((tm, tn), lambda i, j: (i, j)),
        out_shape=jax.ShapeDtypeStruct((t, Z_WIDTH), BF16),
        compiler_params=_params("arbitrary", "arbitrary"),
        name="in_proj",
    )(u, w_in_t)


def _attn_kernel(qg_ref, kg_ref, q_ref, k_ref, vt_ref, wc_f32_ref, wm_f32_ref, wo_f32_ref,
                 o_ref, wc_bf16_ref, wm_bf16_ref, wo_bf16_ref, m_ref, l_ref, acc_ref, *, tq, tk, vt_cols, q_scale):
    i = pl.program_id(2)
    sub_blocks = tk // vt_cols
    kv_per_q = tq // tk
    gq = jnp.max(jnp.abs(qg_ref[...]), axis=-1, keepdims=True)
    gk = jnp.max(jnp.abs(kg_ref[...]), axis=-1, keepdims=True)
    bound = (gq * gk)[0, 0] * (q_scale * QK_HEAD_DIM)
    fixed_shift = bound <= FIXED_SHIFT_LIMIT
    _cast_slabs((wc_f32_ref, wm_f32_ref, wo_f32_ref), (wc_bf16_ref, wm_bf16_ref, wo_bf16_ref))

    def score(j, hh, col0):
        start = pl.multiple_of(j * tk, tk)
        q = q_ref[col0:, hh * HEAD_SLOT:(hh + 1) * HEAD_SLOT]
        kb = k_ref[pl.ds(start, tk), hh * HEAD_SLOT:(hh + 1) * HEAD_SLOT]
        return _dot_nt(kb, q)

    def per_head(j, col0, fn):
        sc = {hh: score(j, hh, col0) for hh in range(SCORES_AHEAD)}
        for hh in range(HEADS_PER_STEP):
            fn(hh, sc.pop(hh))
            if hh + SCORES_AHEAD < HEADS_PER_STEP:
                sc[hh + SCORES_AHEAD] = score(j, hh + SCORES_AHEAD, col0)

    def causal(s):
        kv_idx = lax.broadcasted_iota(jnp.int32, s.shape, 0)
        q_idx = lax.broadcasted_iota(jnp.int32, s.shape, 1)
        return jnp.where(kv_idx <= q_idx, s, NEG_INF)

    def p_times_v(j, hh, p):
        pb = p.astype(BF16)
        pv = None
        for c in range(sub_blocks):
            vt = vt_ref[j * sub_blocks + c, hh * V_HEAD_DIM:(hh + 1) * V_HEAD_DIM, :]
            part = _dot(vt, pb[c * vt_cols:(c + 1) * vt_cols, :])
            pv = part if pv is None else pv + part
        return pv

    def sweep(step):
        def body(j, c):
            step(j, 0, False)
            return c

        lax.fori_loop(0, i * kv_per_q, body, 0)
        for d in range(kv_per_q):
            step(i * kv_per_q + d, d * tk, True)
        for hh in range(HEADS_PER_STEP):
            o_ref[:, hh * V_HEAD_DIM:(hh + 1) * V_HEAD_DIM] = (acc_ref[hh] / l_ref[hh]).T.astype(BF16)

    l_ref[...] = jnp.zeros_like(l_ref)
    acc_ref[...] = jnp.zeros_like(acc_ref)

    @pl.when(fixed_shift)
    def _():
        def step(j, col0, masked):
            def head(hh, s):
                p = jnp.exp2((causal(s) if masked else s) - bound)
                l_ref[hh, :, col0:] += jnp.sum(p, axis=0, keepdims=True)
                acc_ref[hh, :, col0:] += p_times_v(j, hh, p)

            per_head(j, col0, head)

        sweep(step)

    @pl.when(jnp.logical_not(fixed_shift))
    def _():
        m_ref[...] = jnp.full_like(m_ref, NEG_INF)

        def step(j, col0, masked):
            def head(hh, s):
                s = causal(s) if masked else s
                m = m_ref[hh, :, col0:]
                m_new = jnp.maximum(m, jnp.max(s, axis=0, keepdims=True))
                alpha = jnp.exp2(m - m_new)
                p = jnp.exp2(s - m_new)
                m_ref[hh, :, col0:] = m_new
                l_ref[hh, :, col0:] = alpha * l_ref[hh, :, col0:] + jnp.sum(p, axis=0, keepdims=True)
                acc_ref[hh, :, col0:] = alpha * acc_ref[hh, :, col0:] + p_times_v(j, hh, p)

            per_head(j, col0, head)

        sweep(step)


def _attn(q, k, vt, q_norm_g, k_norm_g, mix_weights, batch, seq):
    tq, tk = TQ_ATTN, TK_ATTN
    nq = seq // tq
    vt_cols = vt.shape[-1]
    hs = HEADS_PER_STEP
    groups = N_HEADS // hs
    cast_in, cast_out, cast_shape = _cast_sidecar(mix_weights, batch * groups * nq,
                                                  lambda b, h, i: (b * groups + h) * nq + i)
    return pl.pallas_call(
        functools.partial(_attn_kernel, tq=tq, tk=tk, vt_cols=vt_cols, q_scale=Q_SCALE),
        grid=(batch, groups, nq),
        in_specs=[
            _const_spec((1, QK_HEAD_DIM)),
            _const_spec((1, QK_HEAD_DIM)),
            pl.BlockSpec((tq, hs * HEAD_SLOT), lambda b, h, i: (b * nq + i, h)),
            pl.BlockSpec((seq, hs * HEAD_SLOT), lambda b, h, i: (b, h)),
            pl.BlockSpec((seq // vt_cols, hs * V_HEAD_DIM, vt_cols), lambda b, h, i: (b, h, 0)),
        ] + cast_in,
        out_specs=[pl.BlockSpec((tq, hs * V_HEAD_DIM), lambda b, h, i: (b * nq + i, h))] + cast_out,
        out_shape=[jax.ShapeDtypeStruct((batch * seq, N_HEADS * V_HEAD_DIM), BF16)] + cast_shape,
        scratch_shapes=[
            pltpu.VMEM((hs, 1, tq), F32),
            pltpu.VMEM((hs, 1, tq), F32),
            pltpu.VMEM((hs, V_HEAD_DIM, tq), F32),
        ],
        compiler_params=_params("arbitrary", "arbitrary", "arbitrary"),
        name="attn",
    )(q_norm_g, k_norm_g, q, k, vt, *mix_weights)


def _mix_out_kernel(gates_ref, zb_ref, zc_ref, zv_ref, zc_prev_ref, zv_prev_ref, o_ref, x_ref,
                    b_gate_ref, conv_w_ref, w_conv_out_ref, w_mla_out_ref, w_o_ref, ln2_g_ref,
                    h_ref, u2_ref, *, tm, tiles_per_seq):
    seq_start = (pl.program_id(0) % tiles_per_seq) == 0
    prev = zc_prev_ref[...].astype(F32) * zv_prev_ref[...].astype(F32)
    cv = jnp.concatenate([jnp.where(seq_start, 0.0, prev), zc_ref[...].astype(F32) * zv_ref[...].astype(F32)], axis=0)
    conv = _causal_conv3(cv, conv_w_ref[...], tm)
    conv_in = (zb_ref[...].astype(F32) * conv).astype(BF16)
    y_conv = _dot(conv_in, w_conv_out_ref[...])
    y_mla = _dot(o_ref[...], w_mla_out_ref[...])
    gates = jax.nn.sigmoid(gates_ref[...].astype(F32) + b_gate_ref[...])
    mix = (gates[:, :D_MODEL] * y_conv + gates[:, D_MODEL:] * y_mla).astype(BF16)
    h = x_ref[...] + _dot(mix, w_o_ref[...])
    h_ref[...] = h
    u2_ref[...] = (h * _rms_scale(h, D_MODEL) * ln2_g_ref[...]).astype(BF16)


def _mix_out(z, o, x2, b_gate, conv_w, w_conv_out, w_mla_out, w_o, ln2_g, seq):
    t = x2.shape[0]
    tm = TM_MIX
    halo_per_tile = tm // HALO
    cblk = lambda c: (lambda i: (i, c))
    prev = lambda c: (lambda i: (jnp.maximum(i * halo_per_tile - 1, 0), c))
    gate_blocks = 2 * D_MODEL // CONV_DIM
    return pl.pallas_call(
        functools.partial(_mix_out_kernel, tm=tm, tiles_per_seq=seq // tm),
        grid=(t // tm,),
        in_specs=[
            pl.BlockSpec((tm, 2 * D_MODEL), cblk(0)),
            pl.BlockSpec((tm, CONV_DIM), cblk(gate_blocks)),
            pl.BlockSpec((tm, CONV_DIM), cblk(gate_blocks + 1)),
            pl.BlockSpec((tm, CONV_DIM), cblk(gate_blocks + 2)),
            pl.BlockSpec((HALO, CONV_DIM), prev(gate_blocks + 1)),
            pl.BlockSpec((HALO, CONV_DIM), prev(gate_blocks + 2)),
            pl.BlockSpec((tm, D_MODEL), cblk(0)),
            pl.BlockSpec((tm, D_MODEL), cblk(0)),
            _const_spec((1, 2 * D_MODEL)),
            _const_spec((CONV_WIDTH, CONV_DIM)),
            _const_spec((CONV_DIM, D_MODEL)),
            _const_spec((D_MODEL, D_MODEL)),
            _const_spec((D_MODEL, D_MODEL)),
            _const_spec((1, D_MODEL)),
        ],
        out_specs=[pl.BlockSpec((tm, D_MODEL), cblk(0)), pl.BlockSpec((tm, D_MODEL), cblk(0))],
        out_shape=[jax.ShapeDtypeStruct((t, D_MODEL), F32), jax.ShapeDtypeStruct((t, D_MODEL), BF16)],
        compiler_params=_params("arbitrary"),
        name="mix_out",
    )(z, z, z, z, z, z, o, x2, b_gate, conv_w, w_conv_out, w_mla_out, w_o, ln2_g)


def _ffn_kernel(u_ref, u_prev_ref, h_ref, wg_ref, wu_ref, cwg_ref, cwu_ref, bg_ref, bu_ref, wd_ref,
                out_ref, lhs_ref, pg_ref, pu_ref, *, tm, row_chunk, tiles_per_seq):
    @pl.when(pl.program_id(1) == 0)
    def _():
        seq_start = (pl.program_id(0) % tiles_per_seq) == 0
        lhs_ref[0:HALO, :] = jnp.where(seq_start, jnp.zeros_like(u_prev_ref), u_prev_ref[...])
        lhs_ref[HALO:, :] = u_ref[...]
        out_ref[...] = h_ref[...]

    bounds = [0] + [HALO + (c + 1) * row_chunk for c in range(tm // row_chunk)]
    for lo, hi in zip(bounds[:-1], bounds[1:]):
        pg_ref[lo:hi, :] = _dot(lhs_ref[lo:hi, :], wg_ref[...])
        pu_ref[lo:hi, :] = _dot(lhs_ref[lo:hi, :], wu_ref[...])

    def conv(p_ref, cw_ref, b_ref, r0):
        cw = cw_ref[...]
        return (cw[0:1, :] * p_ref[r0 + HALO - 2:r0 + HALO - 2 + row_chunk, :]
                + cw[1:2, :] * p_ref[r0 + HALO - 1:r0 + HALO - 1 + row_chunk, :]
                + cw[2:3, :] * p_ref[r0 + HALO:r0 + HALO + row_chunk, :]) + b_ref[...]

    for r0 in range(0, tm, row_chunk):
        a_gate = conv(pg_ref, cwg_ref, bg_ref, r0)
        a_up = conv(pu_ref, cwu_ref, bu_ref, r0)
        act = (a_gate * jax.nn.sigmoid(a_gate) * a_up).astype(BF16)
        out_ref[r0:r0 + row_chunk, :] += _dot(act, wd_ref[...])


def _ffn(u2, h1, w_up, conv_w, conv_b, w_down, seq):
    t = u2.shape[0]
    tm, tf = TM_FFN, TF_FFN
    nf = D_FF // tf
    halo_per_tile = tm // HALO
    return pl.pallas_call(
        functools.partial(_ffn_kernel, tm=tm, row_chunk=FFN_ROW_CHUNK, tiles_per_seq=seq // tm),
        grid=(t // tm, nf),
        in_specs=[
            pl.BlockSpec((tm, D_MODEL), lambda i, j: (i, 0)),
            pl.BlockSpec((HALO, D_MODEL), lambda i, j: (jnp.maximum(i * halo_per_tile - 1, 0), 0)),
            pl.BlockSpec((tm, D_MODEL), lambda i, j: (i, 0)),
            pl.BlockSpec((D_MODEL, tf), lambda i, j: (0, j)),
            pl.BlockSpec((D_MODEL, tf), lambda i, j: (0, j + nf)),
            pl.BlockSpec((CONV_WIDTH, tf), lambda i, j: (0, j)),
            pl.BlockSpec((CONV_WIDTH, tf), lambda i, j: (0, j + nf)),
            pl.BlockSpec((1, tf), lambda i, j: (0, j)),
            pl.BlockSpec((1, tf), lambda i, j: (0, j + nf)),
            pl.BlockSpec((tf, D_MODEL), lambda i, j: (j, 0)),
        ],
        out_specs=pl.BlockSpec((tm, D_MODEL), lambda i, j: (i, 0)),
        out_shape=jax.ShapeDtypeStruct((t, D_MODEL), F32),
        scratch_shapes=[
            pltpu.VMEM((tm + HALO, D_MODEL), BF16),
            pltpu.VMEM((tm + HALO, tf), F32),
            pltpu.VMEM((tm + HALO, tf), F32),
        ],
        compiler_params=_params("arbitrary", "arbitrary"),
        name="ffn",
    )(u2, u2, h1, w_up, w_up, conv_w, conv_w, conv_b, conv_b, w_down)


def _row(v):
    return v.reshape(1, -1).astype(F32)


def _layer(h2, pos, inv_freq, batch, seq, ln1_g, w_in, b_gate, conv_w, w_conv_out, q_a_g, w_q_b, kv_a_g, w_kv_b,
           q_norm_g, k_norm_g, w_mla_out, w_o, ln2_g, w_ffn_up, ffn_conv_w, ffn_conv_b, w_ffn_down):
    w_in_t = w_in.T.astype(BF16)
    wq3 = w_q_b.reshape(Q_LORA_RANK, N_HEADS, QK_HEAD_DIM)
    w_qn = wq3[:, :, :QK_NOPE_DIM].reshape(Q_LORA_RANK, -1).astype(BF16)
    w_qr = wq3[:, :, QK_NOPE_DIM:].reshape(Q_LORA_RANK, -1).astype(BF16)
    wkv3 = w_kv_b.reshape(KV_LORA_RANK, N_HEADS, QK_NOPE_DIM + V_HEAD_DIM)
    w_kn = wkv3[:, :, :QK_NOPE_DIM].reshape(KV_LORA_RANK, -1).astype(BF16)
    w_v = wkv3[:, :, QK_NOPE_DIM:].reshape(KV_LORA_RANK, -1).astype(BF16)
    twice = lambda g: _row(jnp.concatenate([g, g]))

    q, k, vt, u, w_up, w_down = _mla_prep(
        h2, pos, _row(ln1_g), w_in_t, _row(q_a_g), _row(kv_a_g), w_qn, w_qr, w_kn, w_v,
        _row(q_norm_g[:QK_NOPE_DIM]), twice(q_norm_g[QK_NOPE_DIM:]),
        _row(k_norm_g[:QK_NOPE_DIM]), twice(k_norm_g[QK_NOPE_DIM:]), inv_freq, (w_ffn_up, w_ffn_down))
    z = _in_proj(u, w_in_t)
    o, w_conv_out, w_mla_out, w_o = _attn(q, k, vt, _row(q_norm_g), _row(k_norm_g),
                                          (w_conv_out, w_mla_out, w_o), batch, seq)
    h1, u2 = _mix_out(z, o, h2, _row(b_gate), conv_w.astype(F32), w_conv_out, w_mla_out, w_o, _row(ln2_g), seq)
    return _ffn(u2, h1, w_up, ffn_conv_w.astype(F32), _row(ffn_conv_b), w_down, seq)


def kernel(x, positions, ln1_g, w_in, b_gate, conv_w, w_conv_out, q_a_g, w_q_b, kv_a_g, w_kv_b,
           q_norm_g, k_norm_g, w_mla_out, w_o, ln2_g, w_ffn_up, ffn_conv_w, ffn_conv_b, w_ffn_down):
    batch, seq, _ = x.shape
    depth = ln1_g.shape[0]
    h2 = x.reshape(batch * seq, D_MODEL)
    pos = positions.reshape(batch * seq, 1).astype(jnp.int32)
    freq = ROPE_THETA ** (-jnp.arange(0, QK_ROPE_DIM, 2, dtype=F32) / QK_ROPE_DIM)
    inv_freq = jnp.tile(freq, LANES // (QK_ROPE_DIM // 2)).reshape(1, LANES)
    for l in range(depth):
        h2 = _layer(h2, pos, inv_freq, batch, seq, ln1_g[l], w_in[l], b_gate[l], conv_w[l], w_conv_out[l],
                    q_a_g[l], w_q_b[l], kv_a_g[l], w_kv_b[l], q_norm_g[l], k_norm_g[l], w_mla_out[l],
                    w_o[l], ln2_g[l], w_ffn_up[l], ffn_conv_w[l], ffn_conv_b[l], w_ffn_down[l])
    return h2.reshape(batch, seq, D_MODEL)
```

```python
import functools
import math

import jax
import jax.numpy as jnp
from jax import lax
from jax.experimental import pallas as pl
from jax.experimental.pallas import tpu as pltpu

D_MODEL = 2048
CONV_DIM = 1024
CONV_WIDTH = 3
N_HEADS = 16
QK_NOPE_DIM = 128
QK_ROPE_DIM = 64
QK_HEAD_DIM = QK_NOPE_DIM + QK_ROPE_DIM
V_HEAD_DIM = 128
Q_LORA_RANK = 768
KV_LORA_RANK = 512
ROPE_THETA = 10000.0
D_FF = 5632
NORM_EPS = 1e-6
NEG_INF = -1e30

Q_SCALE = QK_HEAD_DIM ** -0.5 * math.log2(math.e)
FIXED_SHIFT_LIMIT = 48.0

LANES = 128
HEAD_SLOT = 2 * LANES
HALO = 16
LAT_OFFSET = 3 * CONV_DIM
LAT_WIDTH = 1536
GATE_OFFSET = LAT_OFFSET + Q_LORA_RANK + KV_LORA_RANK + QK_ROPE_DIM
Z_WIDTH = 2 * D_MODEL + 3 * CONV_DIM
VMEM_LIMIT = 56 * 1024 * 1024

TM_PREP = 256
TM_INPROJ = 2048
TN_INPROJ = 1024
TQ_ATTN = 1024
TK_ATTN = 512
HEADS_PER_STEP = 4
SCORES_AHEAD = 4
TM_MIX = 256
TM_FFN = 512
TF_FFN = 512
FFN_ROW_CHUNK = 256

F32 = jnp.float32
BF16 = jnp.bfloat16


def _dot(a, b):
    return jnp.dot(a, b, preferred_element_type=F32)


def _dot_nt(a, b):
    return lax.dot_general(a, b, (((1,), (1,)), ((), ())), preferred_element_type=F32)


def _rms_scale(x, width):
    return lax.rsqrt(jnp.sum(x * x, axis=-1, keepdims=True) * (1.0 / width) + NORM_EPS)


def _params(*sem):
    return pltpu.CompilerParams(dimension_semantics=sem, vmem_limit_bytes=VMEM_LIMIT)


def _const_spec(shape):
    return pl.BlockSpec(shape, lambda *_: (0,) * len(shape), pipeline_mode=pl.Buffered(1))


def _cast_sidecar(weights, n_steps, step_index):
    in_specs, out_specs, out_shape = [], [], []
    for w in weights:
        rows, cols = w.shape
        slab = rows // n_steps
        assert slab * n_steps == rows and slab % 16 == 0, (w.shape, n_steps)
        spec = pl.BlockSpec((slab, cols), lambda *ids: (step_index(*ids), 0))
        in_specs.append(spec)
        out_specs.append(spec)
        out_shape.append(jax.ShapeDtypeStruct(w.shape, BF16))
    return in_specs, out_specs, out_shape


def _cast_slabs(src_refs, dst_refs):
    for src, dst in zip(src_refs, dst_refs):
        dst[...] = src[...].astype(BF16)


def _causal_conv3(p, cw, rows):
    full = cw[2:3, :] * p + cw[1:2, :] * pltpu.roll(p, 1, axis=0) + cw[0:1, :] * pltpu.roll(p, 2, axis=0)
    return full[p.shape[0] - rows:, :]


def _mla_prep_kernel(x_ref, pos_ref, ln_g_ref, w_lat_ref, qa_g_ref, kva_g_ref, w_qn_ref, w_qr_ref,
                     w_kn_ref, w_v_ref, qg_nope_ref, qg_rope_ref, kg_nope_ref, kg_rope_ref, inv_freq_ref,
                     w_up_f32_ref, w_down_f32_ref,
                     q_ref, k_ref, vt_ref, u_ref, w_up_bf16_ref, w_down_bf16_ref,
                     qn_s, qr_s, kn_s, v_s, kr_s, *, q_scale):
    @pl.when(pl.program_id(0) == 0)
    def _():
        for s in (qn_s, qr_s, kn_s, v_s, kr_s):
            s[...] = jnp.zeros_like(s)

    _cast_slabs((w_up_f32_ref, w_down_f32_ref), (w_up_bf16_ref, w_down_bf16_ref))

    ang = pos_ref[...].astype(F32) * inv_freq_ref[...]
    cos = jnp.cos(ang)
    sin = jnp.sin(ang)
    lane = lax.broadcasted_iota(jnp.int32, (1, LANES), 1)
    first_half = (lane % QK_ROPE_DIM) < (QK_ROPE_DIM // 2)
    sin_signed = jnp.where(first_half, -sin, sin)

    def rope(t):
        rot = jnp.where(first_half, pltpu.roll(t, LANES - QK_ROPE_DIM // 2, axis=1),
                        pltpu.roll(t, QK_ROPE_DIM // 2, axis=1))
        return t * cos + rot * sin_signed

    k_rope = kr_s[...]
    k_rope = jnp.where(lane < QK_ROPE_DIM, k_rope, pltpu.roll(k_rope, QK_ROPE_DIM, axis=1))
    qg_nope = qg_nope_ref[...] * q_scale
    qg_rope = qg_rope_ref[...] * q_scale
    kg_nope = kg_nope_ref[...]
    k_tail = rope(k_rope * kg_rope_ref[...])
    k_rope_ss = 0.5 * jnp.sum(k_rope * k_rope, axis=-1, keepdims=True)

    for pair in range(N_HEADS // 2):
        q_pair = qr_s[:, pair * LANES:(pair + 1) * LANES]
        q_pair_roped = rope(q_pair * qg_rope)
        for sub in range(2):
            h = 2 * pair + sub
            mine = (lane // QK_ROPE_DIM) == sub
            q_nope = qn_s[:, h * QK_NOPE_DIM:(h + 1) * QK_NOPE_DIM]
            ss = (jnp.sum(q_nope * q_nope, axis=-1, keepdims=True)
                  + jnp.sum(jnp.where(mine, q_pair * q_pair, 0.0), axis=-1, keepdims=True))
            r = lax.rsqrt(ss * (1.0 / QK_HEAD_DIM) + NORM_EPS)
            q_ref[:, h * HEAD_SLOT: h * HEAD_SLOT + LANES] = (q_nope * r * qg_nope).astype(BF16)
            q_ref[:, h * HEAD_SLOT + LANES:(h + 1) * HEAD_SLOT] = jnp.where(
                mine, q_pair_roped * r, 0.0).astype(BF16)

    for h in range(N_HEADS):
        k_nope = kn_s[:, h * QK_NOPE_DIM:(h + 1) * QK_NOPE_DIM]
        kss = jnp.sum(k_nope * k_nope, axis=-1, keepdims=True) + k_rope_ss
        kr = lax.rsqrt(kss * (1.0 / QK_HEAD_DIM) + NORM_EPS)
        k_ref[:, h * HEAD_SLOT: h * HEAD_SLOT + LANES] = (k_nope * kr * kg_nope).astype(BF16)
        k_ref[:, h * HEAD_SLOT + LANES:(h + 1) * HEAD_SLOT] = (k_tail * kr).astype(BF16)

    vt_ref[0] = v_s[...].T

    xf = x_ref[...]
    u = (xf * _rms_scale(xf, D_MODEL) * ln_g_ref[...]).astype(BF16)
    u_ref[...] = u
    lat = _dot_nt(u, w_lat_ref[...])
    q_lat = lat[:, :Q_LORA_RANK]
    kv_lat = lat[:, Q_LORA_RANK:Q_LORA_RANK + KV_LORA_RANK]
    kr_s[...] = lat[:, Q_LORA_RANK + KV_LORA_RANK:Q_LORA_RANK + KV_LORA_RANK + LANES]
    qn = (q_lat * _rms_scale(q_lat, Q_LORA_RANK) * qa_g_ref[...]).astype(BF16)
    kvn = (kv_lat * _rms_scale(kv_lat, KV_LORA_RANK) * kva_g_ref[...]).astype(BF16)
    qn_s[...] = _dot(qn, w_qn_ref[...])
    qr_s[...] = _dot(qn, w_qr_ref[...])
    kn_s[...] = _dot(kvn, w_kn_ref[...])
    v_s[...] = _dot(kvn, w_v_ref[...]).astype(BF16)


def _mla_prep(x2, pos, ln_g, w_lat, qa_g, kva_g, w_qn, w_qr, w_kn, w_v, qg_nope, qg_rope, kg_nope, kg_rope,
              inv_freq, ffn_weights):
    t = x2.shape[0]
    tm = TM_PREP
    n = t // tm
    row = lambda i: (jnp.minimum(i, n - 1), 0)
    prev = lambda i: (jnp.maximum(i - 1, 0), 0)
    cast_in, cast_out, cast_shape = _cast_sidecar(ffn_weights, n, lambda i: jnp.minimum(i, n - 1))
    return pl.pallas_call(
        functools.partial(_mla_prep_kernel, q_scale=Q_SCALE),
        grid=(n + 1,),
        in_specs=[
            pl.BlockSpec((tm, D_MODEL), row),
            pl.BlockSpec((tm, 1), prev),
            _const_spec((1, D_MODEL)),
            _const_spec((LAT_WIDTH, D_MODEL)),
            _const_spec((1, Q_LORA_RANK)),
            _const_spec((1, KV_LORA_RANK)),
            _const_spec(w_qn.shape), _const_spec(w_qr.shape),
            _const_spec(w_kn.shape), _const_spec(w_v.shape),
            _const_spec((1, LANES)), _const_spec((1, LANES)),
            _const_spec((1, LANES)), _const_spec((1, LANES)),
            _const_spec((1, LANES)),
        ] + cast_in,
        out_specs=[
            pl.BlockSpec((tm, N_HEADS * HEAD_SLOT), prev),
            pl.BlockSpec((tm, N_HEADS * HEAD_SLOT), prev),
            pl.BlockSpec((1, N_HEADS * V_HEAD_DIM, tm), lambda i: (jnp.maximum(i - 1, 0), 0, 0)),
            pl.BlockSpec((tm, D_MODEL), row),
        ] + cast_out,
        out_shape=[
            jax.ShapeDtypeStruct((t, N_HEADS * HEAD_SLOT), BF16),
            jax.ShapeDtypeStruct((t, N_HEADS * HEAD_SLOT), BF16),
            jax.ShapeDtypeStruct((n, N_HEADS * V_HEAD_DIM, tm), BF16),
            jax.ShapeDtypeStruct((t, D_MODEL), BF16),
        ] + cast_shape,
        scratch_shapes=[
            pltpu.VMEM((tm, N_HEADS * QK_NOPE_DIM), F32),
            pltpu.VMEM((tm, N_HEADS * QK_ROPE_DIM), F32),
            pltpu.VMEM((tm, N_HEADS * QK_NOPE_DIM), F32),
            pltpu.VMEM((tm, N_HEADS * V_HEAD_DIM), BF16),
            pltpu.VMEM((tm, LANES), F32),
        ],
        compiler_params=_params("arbitrary"),
        name="mla_prep",
    )(x2, pos, ln_g, w_lat, qa_g, kva_g, w_qn, w_qr, w_kn, w_v, qg_nope, qg_rope, kg_nope, kg_rope, inv_freq,
      *ffn_weights)


def _in_proj_kernel(u_ref, w_ref, z_ref):
    z_ref[...] = _dot_nt(u_ref[...], w_ref[...].astype(BF16)).astype(BF16)


def _in_proj(u, w_in_t):
    t = u.shape[0]
    tm, tn = TM_INPROJ, TN_INPROJ
    gate_tiles = 2 * D_MODEL // tn

    def w_rows(i, j):
        rows = jnp.where(j < gate_tiles, GATE_OFFSET + j * tn, (j - gate_tiles) * tn)
        return pl.multiple_of(rows, math.gcd(GATE_OFFSET, tn)), 0

    return pl.pallas_call(
        _in_proj_kernel,
        grid=(t // tm, Z_WIDTH // tn),
        in_specs=[
            pl.BlockSpec((tm, D_MODEL), lambda i, j: (i, 0)),
            pl.BlockSpec((pl.Element(tn), pl.Element(D_MODEL)), w_rows),
        ],
        out_specs=pl.BlockSpec((tm, tn), lambda i, j: (i, j)),
        out_shape=jax.ShapeDtypeStruct((t, Z_WIDTH), BF16),
        compiler_params=_params("arbitrary", "arbitrary"),
        name="in_proj",
    )(u, w_in_t)


def _attn_kernel(qg_ref, kg_ref, q_ref, k_ref, vt_ref, wc_f32_ref, wm_f32_ref, wo_f32_ref,
                 o_ref, wc_bf16_ref, wm_bf16_ref, wo_bf16_ref, m_ref, l_ref, acc_ref, *, tq, tk, vt_cols, q_scale):
    i = pl.program_id(2)
    sub_blocks = tk // vt_cols
    kv_per_q = tq // tk
    gq = jnp.max(jnp.abs(qg_ref[...]), axis=-1, keepdims=True)
    gk = jnp.max(jnp.abs(kg_ref[...]), axis=-1, keepdims=True)
    bound = (gq * gk)[0, 0] * (q_scale * QK_HEAD_DIM)
    fixed_shift = bound <= FIXED_SHIFT_LIMIT
    _cast_slabs((wc_f32_ref, wm_f32_ref, wo_f32_ref), (wc_bf16_ref, wm_bf16_ref, wo_bf16_ref))

    def score(j, hh, col0):
        start = pl.multiple_of(j * tk, tk)
        q = q_ref[col0:, hh * HEAD_SLOT:(hh + 1) * HEAD_SLOT]
        kb = k_ref[pl.ds(start, tk), hh * HEAD_SLOT:(hh + 1) * HEAD_SLOT]
        return _dot_nt(kb, q)

    def per_head(j, col0, fn):
        sc = {hh: score(j, hh, col0) for hh in range(SCORES_AHEAD)}
        for hh in range(HEADS_PER_STEP):
            fn(hh, sc.pop(hh))
            if hh + SCORES_AHEAD < HEADS_PER_STEP:
                sc[hh + SCORES_AHEAD] = score(j, hh + SCORES_AHEAD, col0)

    def causal(s):
        kv_idx = lax.broadcasted_iota(jnp.int32, s.shape, 0)
        q_idx = lax.broadcasted_iota(jnp.int32, s.shape, 1)
        return jnp.where(kv_idx <= q_idx, s, NEG_INF)

    def p_times_v(j, hh, p):
        pb = p.astype(BF16)
        pv = None
        for c in range(sub_blocks):
            vt = vt_ref[j * sub_blocks + c, hh * V_HEAD_DIM:(hh + 1) * V_HEAD_DIM, :]
            part = _dot(vt, pb[c * vt_cols:(c + 1) * vt_cols, :])
            pv = part if pv is None else pv + part
        return pv

    def sweep(step):
        def body(j, c):
            step(j, 0, False)
            return c

        lax.fori_loop(0, i * kv_per_q, body, 0)
        for d in range(kv_per_q):
            step(i * kv_per_q + d, d * tk, True)
        for hh in range(HEADS_PER_STEP):
            o_ref[:, hh * V_HEAD_DIM:(hh + 1) * V_HEAD_DIM] = (acc_ref[hh] / l_ref[hh]).T.astype(BF16)

    l_ref[...] = jnp.zeros_like(l_ref)
    acc_ref[...] = jnp.zeros_like(acc_ref)

    @pl.when(fixed_shift)
    def _():
        def step(j, col0, masked):
            def head(hh, s):
                p = jnp.exp2((causal(s) if masked else s) - bound)
                l_ref[hh, :, col0:] += jnp.sum(p, axis=0, keepdims=True)
                acc_ref[hh, :, col0:] += p_times_v(j, hh, p)

            per_head(j, col0, head)

        sweep(step)

    @pl.when(jnp.logical_not(fixed_shift))
    def _():
        m_ref[...] = jnp.full_like(m_ref, NEG_INF)

        def step(j, col0, masked):
            def head(hh, s):
                s = causal(s) if masked else s
                m = m_ref[hh, :, col0:]
                m_new = jnp.maximum(m, jnp.max(s, axis=0, keepdims=True))
                alpha = jnp.exp2(m - m_new)
                p = jnp.exp2(s - m_new)
                m_ref[hh, :, col0:] = m_new
                l_ref[hh, :, col0:] = alpha * l_ref[hh, :, col0:] + jnp.sum(p, axis=0, keepdims=True)
                acc_ref[hh, :, col0:] = alpha * acc_ref[hh, :, col0:] + p_times_v(j, hh, p)

            per_head(j, col0, head)

        sweep(step)


def _attn(q, k, vt, q_norm_g, k_norm_g, mix_weights, batch, seq):
    tq, tk = TQ_ATTN, TK_ATTN
    nq = seq // tq
    vt_cols = vt.shape[-1]
    hs = HEADS_PER_STEP
    groups = N_HEADS // hs
    cast_in, cast_out, cast_shape = _cast_sidecar(mix_weights, batch * groups * nq,
                                                  lambda b, h, i: (b * groups + h) * nq + i)
    return pl.pallas_call(
        functools.partial(_attn_kernel, tq=tq, tk=tk, vt_cols=vt_cols, q_scale=Q_SCALE),
        grid=(batch, groups, nq),
        in_specs=[
            _const_spec((1, QK_HEAD_DIM)),
            _const_spec((1, QK_HEAD_DIM)),
            pl.BlockSpec((tq, hs * HEAD_SLOT), lambda b, h, i: (b * nq + i, h)),
            pl.BlockSpec((seq, hs * HEAD_SLOT), lambda b, h, i: (b, h)),
            pl.BlockSpec((seq // vt_cols, hs * V_HEAD_DIM, vt_cols), lambda b, h, i: (b, h, 0)),
        ] + cast_in,
        out_specs=[pl.BlockSpec((tq, hs * V_HEAD_DIM), lambda b, h, i: (b * nq + i, h))] + cast_out,
        out_shape=[jax.ShapeDtypeStruct((batch * seq, N_HEADS * V_HEAD_DIM), BF16)] + cast_shape,
        scratch_shapes=[
            pltpu.VMEM((hs, 1, tq), F32),
            pltpu.VMEM((hs, 1, tq), F32),
            pltpu.VMEM((hs, V_HEAD_DIM, tq), F32),
        ],
        compiler_params=_params("arbitrary", "arbitrary", "arbitrary"),
        name="attn",
    )(q_norm_g, k_norm_g, q, k, vt, *mix_weights)


def _mix_out_kernel(gates_ref, zb_ref, zc_ref, zv_ref, zc_prev_ref, zv_prev_ref, o_ref, x_ref,
                    b_gate_ref, conv_w_ref, w_conv_out_ref, w_mla_out_ref, w_o_ref, ln2_g_ref,
                    h_ref, u2_ref, *, tm, tiles_per_seq):
    seq_start = (pl.program_id(0) % tiles_per_seq) == 0
    prev = zc_prev_ref[...].astype(F32) * zv_prev_ref[...].astype(F32)
    cv = jnp.concatenate([jnp.where(seq_start, 0.0, prev), zc_ref[...].astype(F32) * zv_ref[...].astype(F32)], axis=0)
    conv = _causal_conv3(cv, conv_w_ref[...], tm)
    conv_in = (zb_ref[...].astype(F32) * conv).astype(BF16)
    y_conv = _dot(conv_in, w_conv_out_ref[...])
    y_mla = _dot(o_ref[...], w_mla_out_ref[...])
    gates = jax.nn.sigmoid(gates_ref[...].astype(F32) + b_gate_ref[...])
    mix = (gates[:, :D_MODEL] * y_conv + gates[:, D_MODEL:] * y_mla).astype(BF16)
    h = x_ref[...] + _dot(mix, w_o_ref[...])
    h_ref[...] = h
    u2_ref[...] = (h * _rms_scale(h, D_MODEL) * ln2_g_ref[...]).astype(BF16)


def _mix_out(z, o, x2, b_gate, conv_w, w_conv_out, w_mla_out, w_o, ln2_g, seq):
    t = x2.shape[0]
    tm = TM_MIX
    halo_per_tile = tm // HALO
    cblk = lambda c: (lambda i: (i, c))
    prev = lambda c: (lambda i: (jnp.maximum(i * halo_per_tile - 1, 0), c))
    gate_blocks = 2 * D_MODEL // CONV_DIM
    return pl.pallas_call(
        functools.partial(_mix_out_kernel, tm=tm, tiles_per_seq=seq // tm),
        grid=(t // tm,),
        in_specs=[
            pl.BlockSpec((tm, 2 * D_MODEL), cblk(0)),
            pl.BlockSpec((tm, CONV_DIM), cblk(gate_blocks)),
            pl.BlockSpec((tm, CONV_DIM), cblk(gate_blocks + 1)),
            pl.BlockSpec((tm, CONV_DIM), cblk(gate_blocks + 2)),
            pl.BlockSpec((HALO, CONV_DIM), prev(gate_blocks + 1)),
            pl.BlockSpec((HALO, CONV_DIM), prev(gate_blocks + 2)),
            pl.BlockSpec((tm, D_MODEL), cblk(0)),
            pl.BlockSpec((tm, D_MODEL), cblk(0)),
            _const_spec((1, 2 * D_MODEL)),
            _const_spec((CONV_WIDTH, CONV_DIM)),
            _const_spec((CONV_DIM, D_MODEL)),
            _const_spec((D_MODEL, D_MODEL)),
            _const_spec((D_MODEL, D_MODEL)),
            _const_spec((1, D_MODEL)),
        ],
        out_specs=[pl.BlockSpec((tm, D_MODEL), cblk(0)), pl.BlockSpec((tm, D_MODEL), cblk(0))],
        out_shape=[jax.ShapeDtypeStruct((t, D_MODEL), F32), jax.ShapeDtypeStruct((t, D_MODEL), BF16)],
        compiler_params=_params("arbitrary"),
        name="mix_out",
    )(z, z, z, z, z, z, o, x2, b_gate, conv_w, w_conv_out, w_mla_out, w_o, ln2_g)


def _ffn_kernel(u_ref, u_prev_ref, h_ref, wg_ref, wu_ref, cwg_ref, cwu_ref, bg_ref, bu_ref, wd_ref,
                out_ref, lhs_ref, pg_ref, pu_ref, *, tm, row_chunk, tiles_per_seq):
    @pl.when(pl.program_id(1) == 0)
    def _():
        seq_start = (pl.program_id(0) % tiles_per_seq) == 0
        lhs_ref[0:HALO, :] = jnp.where(seq_start, jnp.zeros_like(u_prev_ref), u_prev_ref[...])
        lhs_ref[HALO:, :] = u_ref[...]
        out_ref[...] = h_ref[...]

    bounds = [0] + [HALO + (c + 1) * row_chunk for c in range(tm // row_chunk)]
    for lo, hi in zip(bounds[:-1], bounds[1:]):
        pg_ref[lo:hi, :] = _dot(lhs_ref[lo:hi, :], wg_ref[...])
        pu_ref[lo:hi, :] = _dot(lhs_ref[lo:hi, :], wu_ref[...])

    def conv(p_ref, cw_ref, b_ref, r0):
        cw = cw_ref[...]
        return (cw[0:1, :] * p_ref[r0 + HALO - 2:r0 + HALO - 2 + row_chunk, :]
                + cw[1:2, :] * p_ref[r0 + HALO - 1:r0 + HALO - 1 + row_chunk, :]
                + cw[2:3, :] * p_ref[r0 + HALO:r0 + HALO + row_chunk, :]) + b_ref[...]

    for r0 in range(0, tm, row_chunk):
        a_gate = conv(pg_ref, cwg_ref, bg_ref, r0)
        a_up = conv(pu_ref, cwu_ref, bu_ref, r0)
        act = (a_gate * jax.nn.sigmoid(a_gate) * a_up).astype(BF16)
        out_ref[r0:r0 + row_chunk, :] += _dot(act, wd_ref[...])


def _ffn(u2, h1, w_up, conv_w, conv_b, w_down, seq):
    t = u2.shape[0]
    tm, tf = TM_FFN, TF_FFN
    nf = D_FF // tf
    halo_per_tile = tm // HALO
    return pl.pallas_call(
        functools.partial(_ffn_kernel, tm=tm, row_chunk=FFN_ROW_CHUNK, tiles_per_seq=seq // tm),
        grid=(t // tm, nf),
        in_specs=[
            pl.BlockSpec((tm, D_MODEL), lambda i, j: (i, 0)),
            pl.BlockSpec((HALO, D_MODEL), lambda i, j: (jnp.maximum(i * halo_per_tile - 1, 0), 0)),
            pl.BlockSpec((tm, D_MODEL), lambda i, j: (i, 0)),
            pl.BlockSpec((D_MODEL, tf), lambda i, j: (0, j)),
            pl.BlockSpec((D_MODEL, tf), lambda i, j: (0, j + nf)),
            pl.BlockSpec((CONV_WIDTH, tf), lambda i, j: (0, j)),
            pl.BlockSpec((CONV_WIDTH, tf), lambda i, j: (0, j + nf)),
            pl.BlockSpec((1, tf), lambda i, j: (0, j)),
            pl.BlockSpec((1, tf), lambda i, j: (0, j + nf)),
            pl.BlockSpec((tf, D_MODEL), lambda i, j: (j, 0)),
        ],
        out_specs=pl.BlockSpec((tm, D_MODEL), lambda i, j: (i, 0)),
        out_shape=jax.ShapeDtypeStruct((t, D_MODEL), F32),
        scratch_shapes=[
            pltpu.VMEM((tm + HALO, D_MODEL), BF16),
            pltpu.VMEM((tm + HALO, tf), F32),
            pltpu.VMEM((tm + HALO, tf), F32),
        ],
        compiler_params=_params("arbitrary", "arbitrary"),
        name="ffn",
    )(u2, u2, h1, w_up, w_up, conv_w, conv_w, conv_b, conv_b, w_down)


def _row(v):
    return v.reshape(1, -1).astype(F32)


def _layer(h2, pos, inv_freq, batch, seq, ln1_g, w_in, b_gate, conv_w, w_conv_out, q_a_g, w_q_b, kv_a_g, w_kv_b,
           q_norm_g, k_norm_g, w_mla_out, w_o, ln2_g, w_ffn_up, ffn_conv_w, ffn_conv_b, w_ffn_down):
    w_in_t = w_in.T
    w_lat = w_in_t[LAT_OFFSET:LAT_OFFSET + LAT_WIDTH].astype(BF16)
    wq3 = w_q_b.reshape(Q_LORA_RANK, N_HEADS, QK_HEAD_DIM)
    w_qn = wq3[:, :, :QK_NOPE_DIM].reshape(Q_LORA_RANK, -1).astype(BF16)
    w_qr = wq3[:, :, QK_NOPE_DIM:].reshape(Q_LORA_RANK, -1).astype(BF16)
    wkv3 = w_kv_b.reshape(KV_LORA_RANK, N_HEADS, QK_NOPE_DIM + V_HEAD_DIM)
    w_kn = wkv3[:, :, :QK_NOPE_DIM].reshape(KV_LORA_RANK, -1).astype(BF16)
    w_v = wkv3[:, :, QK_NOPE_DIM:].reshape(KV_LORA_RANK, -1).astype(BF16)
    twice = lambda g: _row(jnp.concatenate([g, g]))

    q, k, vt, u, w_up, w_down = _mla_prep(
        h2, pos, _row(ln1_g), w_lat, _row(q_a_g), _row(kv_a_g), w_qn, w_qr, w_kn, w_v,
        _row(q_norm_g[:QK_NOPE_DIM]), twice(q_norm_g[QK_NOPE_DIM:]),
        _row(k_norm_g[:QK_NOPE_DIM]), twice(k_norm_g[QK_NOPE_DIM:]), inv_freq, (w_ffn_up, w_ffn_down))
    z = _in_proj(u, w_in_t)
    o, w_conv_out, w_mla_out, w_o = _attn(q, k, vt, _row(q_norm_g), _row(k_norm_g),
                                          (w_conv_out, w_mla_out, w_o), batch, seq)
    h1, u2 = _mix_out(z, o, h2, _row(b_gate), conv_w.astype(F32), w_conv_out, w_mla_out, w_o, _row(ln2_g), seq)
    return _ffn(u2, h1, w_up, ffn_conv_w.astype(F32), _row(ffn_conv_b), w_down, seq)


def kernel(x, positions, ln1_g, w_in, b_gate, conv_w, w_conv_out, q_a_g, w_q_b, kv_a_g, w_kv_b,
           q_norm_g, k_norm_g, w_mla_out, w_o, ln2_g, w_ffn_up, ffn_conv_w, ffn_conv_b, w_ffn_down):
    batch, seq, _ = x.shape
    depth = ln1_g.shape[0]
    h2 = x.reshape(batch * seq, D_MODEL)
    pos = positions.reshape(batch * seq, 1).astype(jnp.int32)
    freq = ROPE_THETA ** (-jnp.arange(0, QK_ROPE_DIM, 2, dtype=F32) / QK_ROPE_DIM)
    inv_freq = jnp.tile(freq, LANES // (QK_ROPE_DIM // 2)).reshape(1, LANES)
    for l in range(depth):
        h2 = _layer(h2, pos, inv_freq, batch, seq, ln1_g[l], w_in[l], b_gate[l], conv_w[l], w_conv_out[l],
                    q_a_g[l], w_q_b[l], kv_a_g[l], w_kv_b[l], q_norm_g[l], k_norm_g[l], w_mla_out[l],
                    w_o[l], ln2_g[l], w_ffn_up[l], ffn_conv_w[l], ffn_conv_b[l], w_ffn_down[l])
    return h2.reshape(batch, seq, D_MODEL)
```

```python
import functools
import math

import jax
import jax.numpy as jnp
from jax import lax
from jax.experimental import pallas as pl
from jax.experimental.pallas import tpu as pltpu

D_MODEL = 2048
CONV_DIM = 1024
CONV_WIDTH = 3
N_HEADS = 16
QK_NOPE_DIM = 128
QK_ROPE_DIM = 64
QK_HEAD_DIM = QK_NOPE_DIM + QK_ROPE_DIM
V_HEAD_DIM = 128
Q_LORA_RANK = 768
KV_LORA_RANK = 512
ROPE_THETA = 10000.0
D_FF = 5632
NORM_EPS = 1e-6
NEG_INF = -1e30

Q_SCALE = QK_HEAD_DIM ** -0.5 * math.log2(math.e)
FIXED_SHIFT_LIMIT = 48.0

LANES = 128
HEAD_SLOT = 2 * LANES
HALO = 16
LAT_OFFSET = 3 * CONV_DIM
LAT_WIDTH = Q_LORA_RANK + KV_LORA_RANK + LANES
GATE_OFFSET = LAT_OFFSET + Q_LORA_RANK + KV_LORA_RANK + QK_ROPE_DIM
Z_WIDTH = 2 * D_MODEL + 3 * CONV_DIM
VMEM_LIMIT = 56 * 1024 * 1024

TM_PREP = 256
TM_INPROJ = 2048
TN_INPROJ = 1024
TQ_ATTN = 1024
TK_ATTN = 512
HEADS_PER_STEP = 4
SCORES_AHEAD = 4
TM_MIX = 256
TM_FFN = 512
TF_FFN = 512
FFN_ROW_CHUNK = 256

F32 = jnp.float32
BF16 = jnp.bfloat16


def _dot(a, b):
    return jnp.dot(a, b, preferred_element_type=F32)


def _dot_nt(a, b):
    return lax.dot_general(a, b, (((1,), (1,)), ((), ())), preferred_element_type=F32)


def _rms_scale(x, width):
    return lax.rsqrt(jnp.sum(x * x, axis=-1, keepdims=True) * (1.0 / width) + NORM_EPS)


def _params(*sem):
    return pltpu.CompilerParams(dimension_semantics=sem, vmem_limit_bytes=VMEM_LIMIT)


def _const_spec(shape):
    return pl.BlockSpec(shape, lambda *_: (0,) * len(shape), pipeline_mode=pl.Buffered(1))


def _cast_sidecar(weights, n_steps, step_index):
    in_specs, out_specs, out_shape = [], [], []
    for w in weights:
        rows, cols = w.shape
        slab = rows // n_steps
        assert slab * n_steps == rows and slab % 16 == 0, (w.shape, n_steps)
        spec = pl.BlockSpec((slab, cols), lambda *ids: (step_index(*ids), 0))
        in_specs.append(spec)
        out_specs.append(spec)
        out_shape.append(jax.ShapeDtypeStruct(w.shape, BF16))
    return in_specs, out_specs, out_shape


def _cast_slabs(src_refs, dst_refs):
    for src, dst in zip(src_refs, dst_refs):
        dst[...] = src[...].astype(BF16)


def _causal_conv3(p, cw, rows):
    full = cw[2:3, :] * p + cw[1:2, :] * pltpu.roll(p, 1, axis=0) + cw[0:1, :] * pltpu.roll(p, 2, axis=0)
    return full[p.shape[0] - rows:, :]


def _mla_prep_kernel(x_ref, pos_ref, ln_g_ref, w_lat_ref, qa_g_ref, kva_g_ref, w_qn_ref, w_qr_ref,
                     w_kn_ref, w_v_ref, qg_nope_ref, qg_rope_ref, kg_nope_ref, kg_rope_ref, inv_freq_ref,
                     w_up_f32_ref, w_down_f32_ref,
                     q_ref, k_ref, vt_ref, u_ref, w_up_bf16_ref, w_down_bf16_ref,
                     qn_s, qr_s, kn_s, v_s, kr_s, *, q_scale):
    @pl.when(pl.program_id(0) == 0)
    def _():
        for s in (qn_s, qr_s, kn_s, v_s, kr_s):
            s[...] = jnp.zeros_like(s)

    _cast_slabs((w_down_f32_ref,), (w_down_bf16_ref,))
    for tile in range(D_FF // TF_FFN):
        for part in range(2):
            src = part * D_FF + tile * TF_FFN
            dst = (2 * tile + part) * TF_FFN
            w_up_bf16_ref[:, dst:dst + TF_FFN] = w_up_f32_ref[:, src:src + TF_FFN].astype(BF16)

    ang = pos_ref[...].astype(F32) * inv_freq_ref[...]
    cos = jnp.cos(ang)
    sin = jnp.sin(ang)
    lane = lax.broadcasted_iota(jnp.int32, (1, LANES), 1)
    first_half = (lane % QK_ROPE_DIM) < (QK_ROPE_DIM // 2)
    sin_signed = jnp.where(first_half, -sin, sin)

    def rope(t):
        rot = jnp.where(first_half, pltpu.roll(t, LANES - QK_ROPE_DIM // 2, axis=1),
                        pltpu.roll(t, QK_ROPE_DIM // 2, axis=1))
        return t * cos + rot * sin_signed

    k_rope = kr_s[...]
    k_rope = jnp.where(lane < QK_ROPE_DIM, k_rope, pltpu.roll(k_rope, QK_ROPE_DIM, axis=1))
    qg_nope = qg_nope_ref[...] * q_scale
    qg_rope = qg_rope_ref[...] * q_scale
    kg_nope = kg_nope_ref[...]
    k_tail = rope(k_rope * kg_rope_ref[...])
    k_rope_ss = 0.5 * jnp.sum(k_rope * k_rope, axis=-1, keepdims=True)

    for pair in range(N_HEADS // 2):
        q_pair = qr_s[:, pair * LANES:(pair + 1) * LANES]
        q_pair_roped = rope(q_pair * qg_rope)
        for sub in range(2):
            h = 2 * pair + sub
            mine = (lane // QK_ROPE_DIM) == sub
            q_nope = qn_s[:, h * QK_NOPE_DIM:(h + 1) * QK_NOPE_DIM]
            ss = jnp.sum(q_nope * q_nope + jnp.where(mine, q_pair * q_pair, 0.0), axis=-1, keepdims=True)
            r = lax.rsqrt(ss * (1.0 / QK_HEAD_DIM) + NORM_EPS)
            q_ref[:, h * HEAD_SLOT: h * HEAD_SLOT + LANES] = (q_nope * r * qg_nope).astype(BF16)
            q_ref[:, h * HEAD_SLOT + LANES:(h + 1) * HEAD_SLOT] = jnp.where(
                mine, q_pair_roped * r, 0.0).astype(BF16)

    for h in range(N_HEADS):
        k_nope = kn_s[:, h * QK_NOPE_DIM:(h + 1) * QK_NOPE_DIM]
        kss = jnp.sum(k_nope * k_nope, axis=-1, keepdims=True) + k_rope_ss
        kr = lax.rsqrt(kss * (1.0 / QK_HEAD_DIM) + NORM_EPS)
        k_ref[:, h * HEAD_SLOT: h * HEAD_SLOT + LANES] = (k_nope * kr * kg_nope).astype(BF16)
        k_ref[:, h * HEAD_SLOT + LANES:(h + 1) * HEAD_SLOT] = (k_tail * kr).astype(BF16)

    vt_ref[0] = v_s[...].T

    xf = x_ref[...]
    u = (xf * _rms_scale(xf, D_MODEL) * ln_g_ref[...]).astype(BF16)
    u_ref[...] = u
    q_lat = _dot_nt(u, w_lat_ref[:Q_LORA_RANK, :])
    kv_lat = _dot_nt(u, w_lat_ref[Q_LORA_RANK:Q_LORA_RANK + KV_LORA_RANK + LANES, :])
    kr_s[...] = kv_lat[:, KV_LORA_RANK:]
    kv_lat = kv_lat[:, :KV_LORA_RANK]
    qn = (q_lat * _rms_scale(q_lat, Q_LORA_RANK) * qa_g_ref[...]).astype(BF16)
    kvn = (kv_lat * _rms_scale(kv_lat, KV_LORA_RANK) * kva_g_ref[...]).astype(BF16)
    qn_s[...] = _dot(qn, w_qn_ref[...])
    qr_s[...] = _dot(qn, w_qr_ref[...])
    kn_s[...] = _dot(kvn, w_kn_ref[...])
    v_s[...] = _dot(kvn, w_v_ref[...]).astype(BF16)


def _mla_prep(x2, pos, ln_g, w_lat, qa_g, kva_g, w_qn, w_qr, w_kn, w_v, qg_nope, qg_rope, kg_nope, kg_rope,
              inv_freq, ffn_weights):
    t = x2.shape[0]
    tm = TM_PREP
    n = t // tm
    row = lambda i: (jnp.minimum(i, n - 1), 0)
    prev = lambda i: (jnp.maximum(i - 1, 0), 0)
    cast_in, cast_out, cast_shape = _cast_sidecar(ffn_weights, n, lambda i: jnp.minimum(i, n - 1))
    return pl.pallas_call(
        functools.partial(_mla_prep_kernel, q_scale=Q_SCALE),
        grid=(n + 1,),
        in_specs=[
            pl.BlockSpec((tm, D_MODEL), row),
            pl.BlockSpec((tm, 1), prev),
            _const_spec((1, D_MODEL)),
            _const_spec((LAT_WIDTH, D_MODEL)),
            _const_spec((1, Q_LORA_RANK)),
            _const_spec((1, KV_LORA_RANK)),
            _const_spec(w_qn.shape), _const_spec(w_qr.shape),
            _const_spec(w_kn.shape), _const_spec(w_v.shape),
            _const_spec((1, LANES)), _const_spec((1, LANES)),
            _const_spec((1, LANES)), _const_spec((1, LANES)),
            _const_spec((1, LANES)),
        ] + cast_in,
        out_specs=[
            pl.BlockSpec((tm, N_HEADS * HEAD_SLOT), prev),
            pl.BlockSpec((tm, N_HEADS * HEAD_SLOT), prev),
            pl.BlockSpec((1, N_HEADS * V_HEAD_DIM, tm), lambda i: (jnp.maximum(i - 1, 0), 0, 0)),
            pl.BlockSpec((tm, D_MODEL), row),
        ] + cast_out,
        out_shape=[
            jax.ShapeDtypeStruct((t, N_HEADS * HEAD_SLOT), BF16),
            jax.ShapeDtypeStruct((t, N_HEADS * HEAD_SLOT), BF16),
            jax.ShapeDtypeStruct((n, N_HEADS * V_HEAD_DIM, tm), BF16),
            jax.ShapeDtypeStruct((t, D_MODEL), BF16),
        ] + cast_shape,
        scratch_shapes=[
            pltpu.VMEM((tm, N_HEADS * QK_NOPE_DIM), F32),
            pltpu.VMEM((tm, N_HEADS * QK_ROPE_DIM), F32),
            pltpu.VMEM((tm, N_HEADS * QK_NOPE_DIM), F32),
            pltpu.VMEM((tm, N_HEADS * V_HEAD_DIM), BF16),
            pltpu.VMEM((tm, LANES), F32),
        ],
        compiler_params=_params("arbitrary"),
        name="mla_prep",
    )(x2, pos, ln_g, w_lat, qa_g, kva_g, w_qn, w_qr, w_kn, w_v, qg_nope, qg_rope, kg_nope, kg_rope, inv_freq,
      *ffn_weights)


def _in_proj_kernel(u_ref, w_ref, z_ref):
    z_ref[...] = _dot_nt(u_ref[...], w_ref[...].astype(BF16)).astype(BF16)


def _in_proj(u, w_in_t):
    t = u.shape[0]
    tm, tn = TM_INPROJ, TN_INPROJ
    gate_tiles = 2 * D_MODEL // tn

    def w_rows(i, j):
        rows = jnp.where(j < gate_tiles, GATE_OFFSET + j * tn, (j - gate_tiles) * tn)
        return pl.multiple_of(rows, math.gcd(GATE_OFFSET, tn)), 0

    return pl.pallas_call(
        _in_proj_kernel,
        grid=(t // tm, Z_WIDTH // tn),
        in_specs=[
            pl.BlockSpec((tm, D_MODEL), lambda i, j: (i, 0)),
            pl.BlockSpec((pl.Element(tn), pl.Element(D_MODEL)), w_rows),
        ],
        out_specs=pl.BlockSpec((tm, tn), lambda i, j: (i, j)),
        out_shape=jax.ShapeDtypeStruct((t, Z_WIDTH), BF16),
        compiler_params=_params("arbitrary", "arbitrary"),
        name="in_proj",
    )(u, w_in_t)


def _attn_kernel(qg_ref, kg_ref, q_ref, k_ref, vt_ref, wc_f32_ref, wm_f32_ref, wo_f32_ref,
                 o_ref, wc_bf16_ref, wm_bf16_ref, wo_bf16_ref, m_ref, l_ref, acc_ref, *, tq, tk, vt_cols, q_scale):
    i = pl.program_id(2)
    sub_blocks = tk // vt_cols
    kv_per_q = tq // tk
    gq = jnp.max(jnp.abs(qg_ref[...]), axis=-1, keepdims=True)
    gk = jnp.max(jnp.abs(kg_ref[...]), axis=-1, keepdims=True)
    bound = (gq * gk)[0, 0] * (q_scale * QK_HEAD_DIM)
    fixed_shift = bound <= FIXED_SHIFT_LIMIT
    _cast_slabs((wc_f32_ref, wm_f32_ref, wo_f32_ref), (wc_bf16_ref, wm_bf16_ref, wo_bf16_ref))

    def score(j, hh, col0):
        start = pl.multiple_of(j * tk, tk)
        q = q_ref[col0:, hh * HEAD_SLOT:(hh + 1) * HEAD_SLOT]
        kb = k_ref[pl.ds(start, tk), hh * HEAD_SLOT:(hh + 1) * HEAD_SLOT]
        return _dot_nt(kb, q)

    def per_head(j, col0, fn):
        sc = {hh: score(j, hh, col0) for hh in range(SCORES_AHEAD)}
        for hh in range(HEADS_PER_STEP):
            fn(hh, sc.pop(hh))
            if hh + SCORES_AHEAD < HEADS_PER_STEP:
                sc[hh + SCORES_AHEAD] = score(j, hh + SCORES_AHEAD, col0)

    def causal(s):
        kv_idx = lax.broadcasted_iota(jnp.int32, s.shape, 0)
        q_idx = lax.broadcasted_iota(jnp.int32, s.shape, 1)
        return jnp.where(kv_idx <= q_idx, s, NEG_INF)

    def p_times_v(j, hh, p):
        pb = p.astype(BF16)
        pv = None
        for c in range(sub_blocks):
            vt = vt_ref[j * sub_blocks + c, hh * V_HEAD_DIM:(hh + 1) * V_HEAD_DIM, :]
            part = _dot(vt, pb[c * vt_cols:(c + 1) * vt_cols, :])
            pv = part if pv is None else pv + part
        return pv

    def sweep(step):
        def body(j, c):
            step(j, 0, False)
            return c

        lax.fori_loop(0, i * kv_per_q, body, 0)
        for d in range(kv_per_q):
            step(i * kv_per_q + d, d * tk, True)
        for hh in range(HEADS_PER_STEP):
            o_ref[:, hh * V_HEAD_DIM:(hh + 1) * V_HEAD_DIM] = (acc_ref[hh] / l_ref[hh]).T.astype(BF16)

    l_ref[...] = jnp.zeros_like(l_ref)
    acc_ref[...] = jnp.zeros_like(acc_ref)

    @pl.when(fixed_shift)
    def _():
        def step(j, col0, masked):
            def head(hh, s):
                p = jnp.exp2((causal(s) if masked else s) - bound)
                l_ref[hh, :, col0:] += jnp.sum(p, axis=0, keepdims=True)
                acc_ref[hh, :, col0:] += p_times_v(j, hh, p)

            per_head(j, col0, head)

        sweep(step)

    @pl.when(jnp.logical_not(fixed_shift))
    def _():
        m_ref[...] = jnp.full_like(m_ref, NEG_INF)

        def step(j, col0, masked):
            def head(hh, s):
                s = causal(s) if masked else s
                m = m_ref[hh, :, col0:]
                m_new = jnp.maximum(m, jnp.max(s, axis=0, keepdims=True))
                alpha = jnp.exp2(m - m_new)
                p = jnp.exp2(s - m_new)
                m_ref[hh, :, col0:] = m_new
                l_ref[hh, :, col0:] = alpha * l_ref[hh, :, col0:] + jnp.sum(p, axis=0, keepdims=True)
                acc_ref[hh, :, col0:] = alpha * acc_ref[hh, :, col0:] + p_times_v(j, hh, p)

            per_head(j, col0, head)

        sweep(step)


def _attn(q, k, vt, q_norm_g, k_norm_g, mix_weights, batch, seq):
    tq, tk = TQ_ATTN, TK_ATTN
    nq = seq // tq
    vt_cols = vt.shape[-1]
    hs = HEADS_PER_STEP
    groups = N_HEADS // hs
    cast_in, cast_out, cast_shape = _cast_sidecar(mix_weights, batch * groups * nq,
                                                  lambda b, h, i: (b * groups + h) * nq + i)
    return pl.pallas_call(
        functools.partial(_attn_kernel, tq=tq, tk=tk, vt_cols=vt_cols, q_scale=Q_SCALE),
        grid=(batch, groups, nq),
        in_specs=[
            _const_spec((1, QK_HEAD_DIM)),
            _const_spec((1, QK_HEAD_DIM)),
            pl.BlockSpec((tq, hs * HEAD_SLOT), lambda b, h, i: (b * nq + i, h)),
            pl.BlockSpec((seq, hs * HEAD_SLOT), lambda b, h, i: (b, h)),
            pl.BlockSpec((seq // vt_cols, hs * V_HEAD_DIM, vt_cols), lambda b, h, i: (b, h, 0)),
        ] + cast_in,
        out_specs=[pl.BlockSpec((tq, hs * V_HEAD_DIM), lambda b, h, i: (b * nq + i, h))] + cast_out,
        out_shape=[jax.ShapeDtypeStruct((batch * seq, N_HEADS * V_HEAD_DIM), BF16)] + cast_shape,
        scratch_shapes=[
            pltpu.VMEM((hs, 1, tq), F32),
            pltpu.VMEM((hs, 1, tq), F32),
            pltpu.VMEM((hs, V_HEAD_DIM, tq), F32),
        ],
        compiler_params=_params("arbitrary", "arbitrary", "arbitrary"),
        name="attn",
    )(q_norm_g, k_norm_g, q, k, vt, *mix_weights)


def _mix_out_kernel(gates_ref, zb_ref, zc_ref, zv_ref, zc_prev_ref, zv_prev_ref, o_ref, x_ref,
                    b_gate_ref, conv_w_ref, w_conv_out_ref, w_mla_out_ref, w_o_ref, ln2_g_ref,
                    h_ref, u2_ref, *, tm, tiles_per_seq):
    seq_start = (pl.program_id(0) % tiles_per_seq) == 0
    prev = zc_prev_ref[...].astype(F32) * zv_prev_ref[...].astype(F32)
    cv = jnp.concatenate([jnp.where(seq_start, 0.0, prev), zc_ref[...].astype(F32) * zv_ref[...].astype(F32)], axis=0)
    conv = _causal_conv3(cv, conv_w_ref[...], tm)
    conv_in = (zb_ref[...].astype(F32) * conv).astype(BF16)
    y_conv = _dot(conv_in, w_conv_out_ref[...])
    y_mla = _dot(o_ref[...], w_mla_out_ref[...])
    gates = jax.nn.sigmoid(gates_ref[...].astype(F32) + b_gate_ref[...])
    mix = (gates[:, :D_MODEL] * y_conv + gates[:, D_MODEL:] * y_mla).astype(BF16)
    h = x_ref[...] + _dot(mix, w_o_ref[...])
    h_ref[...] = h
    u2_ref[...] = (h * _rms_scale(h, D_MODEL) * ln2_g_ref[...]).astype(BF16)


def _mix_out(z, o, x2, b_gate, conv_w, w_conv_out, w_mla_out, w_o, ln2_g, seq):
    t = x2.shape[0]
    tm = TM_MIX
    halo_per_tile = tm // HALO
    cblk = lambda c: (lambda i: (i, c))
    prev = lambda c: (lambda i: (jnp.maximum(i * halo_per_tile - 1, 0), c))
    gate_blocks = 2 * D_MODEL // CONV_DIM
    return pl.pallas_call(
        functools.partial(_mix_out_kernel, tm=tm, tiles_per_seq=seq // tm),
        grid=(t // tm,),
        in_specs=[
            pl.BlockSpec((tm, 2 * D_MODEL), cblk(0)),
            pl.BlockSpec((tm, CONV_DIM), cblk(gate_blocks)),
            pl.BlockSpec((tm, CONV_DIM), cblk(gate_blocks + 1)),
            pl.BlockSpec((tm, CONV_DIM), cblk(gate_blocks + 2)),
            pl.BlockSpec((HALO, CONV_DIM), prev(gate_blocks + 1)),
            pl.BlockSpec((HALO, CONV_DIM), prev(gate_blocks + 2)),
            pl.BlockSpec((tm, D_MODEL), cblk(0)),
            pl.BlockSpec((tm, D_MODEL), cblk(0)),
            _const_spec((1, 2 * D_MODEL)),
            _const_spec((CONV_WIDTH, CONV_DIM)),
            _const_spec((CONV_DIM, D_MODEL)),
            _const_spec((D_MODEL, D_MODEL)),
            _const_spec((D_MODEL, D_MODEL)),
            _const_spec((1, D_MODEL)),
        ],
        out_specs=[pl.BlockSpec((tm, D_MODEL), cblk(0)), pl.BlockSpec((tm, D_MODEL), cblk(0))],
        out_shape=[jax.ShapeDtypeStruct((t, D_MODEL), F32), jax.ShapeDtypeStruct((t, D_MODEL), BF16)],
        compiler_params=_params("arbitrary"),
        name="mix_out",
    )(z, z, z, z, z, z, o, x2, b_gate, conv_w, w_conv_out, w_mla_out, w_o, ln2_g)


def _ffn_kernel(u_ref, u_prev_ref, h_ref, w_up_ref, cw_ref, b_ref, wd_ref, out_ref, lhs_ref, p_ref,
                *, tm, tf, row_chunk, tiles_per_seq):
    @pl.when(pl.program_id(1) == 0)
    def _():
        seq_start = (pl.program_id(0) % tiles_per_seq) == 0
        lhs_ref[0:HALO, :] = jnp.where(seq_start, jnp.zeros_like(u_prev_ref), u_prev_ref[...])
        lhs_ref[HALO:, :] = u_ref[...]
        out_ref[...] = h_ref[...]

    bounds = [0] + [HALO + (c + 1) * row_chunk for c in range(tm // row_chunk)]
    for lo, hi in zip(bounds[:-1], bounds[1:]):
        p_ref[lo:hi, :] = _dot(lhs_ref[lo:hi, :], w_up_ref[...])

    cw = cw_ref[...]
    for r0 in range(0, tm, row_chunk):
        a = (cw[0:1, :] * p_ref[r0 + HALO - 2:r0 + HALO - 2 + row_chunk, :]
             + cw[1:2, :] * p_ref[r0 + HALO - 1:r0 + HALO - 1 + row_chunk, :]
             + cw[2:3, :] * p_ref[r0 + HALO:r0 + HALO + row_chunk, :]) + b_ref[...]
        a_gate, a_up = a[:, :tf], a[:, tf:]
        act = (a_gate * jax.nn.sigmoid(a_gate) * a_up).astype(BF16)
        out_ref[r0:r0 + row_chunk, :] += _dot(act, wd_ref[...])


def _pair_tiles(v, tf):
    rows = v.shape[0]
    return v.reshape(rows, 2, D_FF // tf, tf).transpose(0, 2, 1, 3).reshape(rows, 2 * D_FF)


def _ffn(u2, h1, w_up_paired, conv_w, conv_b, w_down, seq):
    t = u2.shape[0]
    tm, tf = TM_FFN, TF_FFN
    halo_per_tile = tm // HALO
    return pl.pallas_call(
        functools.partial(_ffn_kernel, tm=tm, tf=tf, row_chunk=FFN_ROW_CHUNK, tiles_per_seq=seq // tm),
        grid=(t // tm, D_FF // tf),
        in_specs=[
            pl.BlockSpec((tm, D_MODEL), lambda i, j: (i, 0)),
            pl.BlockSpec((HALO, D_MODEL), lambda i, j: (jnp.maximum(i * halo_per_tile - 1, 0), 0)),
            pl.BlockSpec((tm, D_MODEL), lambda i, j: (i, 0)),
            pl.BlockSpec((D_MODEL, 2 * tf), lambda i, j: (0, j)),
            pl.BlockSpec((CONV_WIDTH, 2 * tf), lambda i, j: (0, j)),
            pl.BlockSpec((1, 2 * tf), lambda i, j: (0, j)),
            pl.BlockSpec((tf, D_MODEL), lambda i, j: (j, 0)),
        ],
        out_specs=pl.BlockSpec((tm, D_MODEL), lambda i, j: (i, 0)),
        out_shape=jax.ShapeDtypeStruct((t, D_MODEL), F32),
        scratch_shapes=[
            pltpu.VMEM((tm + HALO, D_MODEL), BF16),
            pltpu.VMEM((tm + HALO, 2 * tf), F32),
        ],
        compiler_params=_params("arbitrary", "arbitrary"),
        name="ffn",
    )(u2, u2, h1, w_up_paired, _pair_tiles(conv_w, tf), _pair_tiles(conv_b, tf), w_down)


def _row(v):
    return v.reshape(1, -1).astype(F32)


def _layer(h2, pos, inv_freq, batch, seq, ln1_g, w_in, b_gate, conv_w, w_conv_out, q_a_g, w_q_b, kv_a_g, w_kv_b,
           q_norm_g, k_norm_g, w_mla_out, w_o, ln2_g, w_ffn_up, ffn_conv_w, ffn_conv_b, w_ffn_down):
    w_in_t = w_in.T
    w_lat = w_in_t[LAT_OFFSET:LAT_OFFSET + LAT_WIDTH].astype(BF16)
    wq3 = w_q_b.reshape(Q_LORA_RANK, N_HEADS, QK_HEAD_DIM)
    w_qn = wq3[:, :, :QK_NOPE_DIM].reshape(Q_LORA_RANK, -1).astype(BF16)
    w_qr = wq3[:, :, QK_NOPE_DIM:].reshape(Q_LORA_RANK, -1).astype(BF16)
    wkv3 = w_kv_b.reshape(KV_LORA_RANK, N_HEADS, QK_NOPE_DIM + V_HEAD_DIM)
    w_kn = wkv3[:, :, :QK_NOPE_DIM].reshape(KV_LORA_RANK, -1).astype(BF16)
    w_v = wkv3[:, :, QK_NOPE_DIM:].reshape(KV_LORA_RANK, -1).astype(BF16)
    twice = lambda g: _row(jnp.concatenate([g, g]))

    q, k, vt, u, w_up, w_down = _mla_prep(
        h2, pos, _row(ln1_g), w_lat, _row(q_a_g), _row(kv_a_g), w_qn, w_qr, w_kn, w_v,
        _row(q_norm_g[:QK_NOPE_DIM]), twice(q_norm_g[QK_NOPE_DIM:]),
        _row(k_norm_g[:QK_NOPE_DIM]), twice(k_norm_g[QK_NOPE_DIM:]), inv_freq, (w_ffn_up, w_ffn_down))
    z = _in_proj(u, w_in_t)
    o, w_conv_out, w_mla_out, w_o = _attn(q, k, vt, _row(q_norm_g), _row(k_norm_g),
                                          (w_conv_out, w_mla_out, w_o), batch, seq)
    h1, u2 = _mix_out(z, o, h2, _row(b_gate), conv_w.astype(F32), w_conv_out, w_mla_out, w_o, _row(ln2_g), seq)
    return _ffn(u2, h1, w_up, ffn_conv_w.astype(F32), _row(ffn_conv_b), w_down, seq)


def kernel(x, positions, ln1_g, w_in, b_gate, conv_w, w_conv_out, q_a_g, w_q_b, kv_a_g, w_kv_b,
           q_norm_g, k_norm_g, w_mla_out, w_o, ln2_g, w_ffn_up, ffn_conv_w, ffn_conv_b, w_ffn_down):
    batch, seq, _ = x.shape
    depth = ln1_g.shape[0]
    h2 = x.reshape(batch * seq, D_MODEL)
    pos = positions.reshape(batch * seq, 1).astype(jnp.int32)
    freq = ROPE_THETA ** (-jnp.arange(0, QK_ROPE_DIM, 2, dtype=F32) / QK_ROPE_DIM)
    inv_freq = jnp.tile(freq, LANES // (QK_ROPE_DIM // 2)).reshape(1, LANES)
    for l in range(depth):
        h2 = _layer(h2, pos, inv_freq, batch, seq, ln1_g[l], w_in[l], b_gate[l], conv_w[l], w_conv_out[l],
                    q_a_g[l], w_q_b[l], kv_a_g[l], w_kv_b[l], q_norm_g[l], k_norm_g[l], w_mla_out[l],
                    w_o[l], ln2_g[l], w_ffn_up[l], ffn_conv_w[l], ffn_conv_b[l], w_ffn_down[l])
    return h2.reshape(batch, seq, D_MODEL)
```

```python
import functools
import math

import jax
import jax.numpy as jnp
from jax import lax
from jax.experimental import pallas as pl
from jax.experimental.pallas import tpu as pltpu

D_MODEL = 2048
CONV_DIM = 1024
CONV_WIDTH = 3
N_HEADS = 16
QK_NOPE_DIM = 128
QK_ROPE_DIM = 64
QK_HEAD_DIM = QK_NOPE_DIM + QK_ROPE_DIM
V_HEAD_DIM = 128
Q_LORA_RANK = 768
KV_LORA_RANK = 512
ROPE_THETA = 10000.0
D_FF = 5632
NORM_EPS = 1e-6
NEG_INF = -1e30

Q_SCALE = QK_HEAD_DIM ** -0.5 * math.log2(math.e)
FIXED_SHIFT_LIMIT = 48.0

LANES = 128
HEAD_SLOT = 2 * LANES
HALO = 16
LAT_OFFSET = 3 * CONV_DIM
LAT_WIDTH = Q_LORA_RANK + KV_LORA_RANK + LANES
GATE_OFFSET = LAT_OFFSET + Q_LORA_RANK + KV_LORA_RANK + QK_ROPE_DIM
Z_WIDTH = 2 * D_MODEL + 3 * CONV_DIM
VMEM_LIMIT = 56 * 1024 * 1024

TM_PREP = 256
TM_INPROJ = 2048
TN_INPROJ = 1024
TQ_ATTN = 1024
TK_ATTN = 512
HEADS_PER_STEP = 4
SCORES_AHEAD = 4
TM_MIX = 256
TM_FFN = 512
TF_FFN = 512
FFN_ROW_CHUNK = 256

F32 = jnp.float32
BF16 = jnp.bfloat16


def _dot(a, b):
    return jnp.dot(a, b, preferred_element_type=F32)


def _dot_nt(a, b):
    return lax.dot_general(a, b, (((1,), (1,)), ((), ())), preferred_element_type=F32)


def _rms_scale(x, width):
    return lax.rsqrt(jnp.sum(x * x, axis=-1, keepdims=True) * (1.0 / width) + NORM_EPS)


def _params(*sem):
    return pltpu.CompilerParams(dimension_semantics=sem, vmem_limit_bytes=VMEM_LIMIT)


def _const_spec(shape):
    return pl.BlockSpec(shape, lambda *_: (0,) * len(shape), pipeline_mode=pl.Buffered(1))


def _cast_sidecar(weights, n_steps, step_index):
    in_specs, out_specs, out_shape = [], [], []
    for w in weights:
        rows, cols = w.shape
        slab = rows // n_steps
        assert slab * n_steps == rows and slab % 16 == 0, (w.shape, n_steps)
        spec = pl.BlockSpec((slab, cols), lambda *ids: (step_index(*ids), 0))
        in_specs.append(spec)
        out_specs.append(spec)
        out_shape.append(jax.ShapeDtypeStruct(w.shape, BF16))
    return in_specs, out_specs, out_shape


def _cast_slabs(src_refs, dst_refs):
    for src, dst in zip(src_refs, dst_refs):
        dst[...] = src[...].astype(BF16)


def _causal_conv3(p, cw, rows):
    full = cw[2:3, :] * p + cw[1:2, :] * pltpu.roll(p, 1, axis=0) + cw[0:1, :] * pltpu.roll(p, 2, axis=0)
    return full[p.shape[0] - rows:, :]


def _mla_prep_kernel(x_ref, pos_ref, ln_g_ref, w_lat_ref, qa_g_ref, kva_g_ref, w_qn_ref, w_qr_ref,
                     w_kn_ref, w_v_ref, qg_nope_ref, qg_rope_ref, kg_nope_ref, kg_rope_ref, inv_freq_ref,
                     q_ref, k_ref, vt_ref, u_ref, qn_s, qr_s, kn_s, v_s, kr_s, *, q_scale):
    @pl.when(pl.program_id(0) == 0)
    def _():
        for s in (qn_s, qr_s, kn_s, v_s, kr_s):
            s[...] = jnp.zeros_like(s)

    ang = pos_ref[...].astype(F32) * inv_freq_ref[...]
    cos = jnp.cos(ang)
    sin = jnp.sin(ang)
    lane = lax.broadcasted_iota(jnp.int32, (1, LANES), 1)
    first_half = (lane % QK_ROPE_DIM) < (QK_ROPE_DIM // 2)
    sin_signed = jnp.where(first_half, -sin, sin)

    def rope(t):
        rot = jnp.where(first_half, pltpu.roll(t, LANES - QK_ROPE_DIM // 2, axis=1),
                        pltpu.roll(t, QK_ROPE_DIM // 2, axis=1))
        return t * cos + rot * sin_signed

    k_rope = kr_s[...]
    k_rope = jnp.where(lane < QK_ROPE_DIM, k_rope, pltpu.roll(k_rope, QK_ROPE_DIM, axis=1))
    qg_nope = qg_nope_ref[...] * q_scale
    qg_rope = qg_rope_ref[...] * q_scale
    kg_nope = kg_nope_ref[...]
    k_tail = rope(k_rope * kg_rope_ref[...])
    k_rope_ss = 0.5 * jnp.sum(k_rope * k_rope, axis=-1, keepdims=True)

    for pair in range(N_HEADS // 2):
        q_pair = qr_s[:, pair * LANES:(pair + 1) * LANES]
        q_pair_roped = rope(q_pair * qg_rope)
        for sub in range(2):
            h = 2 * pair + sub
            mine = (lane // QK_ROPE_DIM) == sub
            q_nope = qn_s[:, h * QK_NOPE_DIM:(h + 1) * QK_NOPE_DIM]
            ss = jnp.sum(q_nope * q_nope + jnp.where(mine, q_pair * q_pair, 0.0), axis=-1, keepdims=True)
            r = lax.rsqrt(ss * (1.0 / QK_HEAD_DIM) + NORM_EPS)
            q_ref[:, h * HEAD_SLOT: h * HEAD_SLOT + LANES] = (q_nope * r * qg_nope).astype(BF16)
            q_ref[:, h * HEAD_SLOT + LANES:(h + 1) * HEAD_SLOT] = jnp.where(
                mine, q_pair_roped * r, 0.0).astype(BF16)

    for h in range(N_HEADS):
        k_nope = kn_s[:, h * QK_NOPE_DIM:(h + 1) * QK_NOPE_DIM]
        kss = jnp.sum(k_nope * k_nope, axis=-1, keepdims=True) + k_rope_ss
        kr = lax.rsqrt(kss * (1.0 / QK_HEAD_DIM) + NORM_EPS)
        k_ref[:, h * HEAD_SLOT: h * HEAD_SLOT + LANES] = (k_nope * kr * kg_nope).astype(BF16)
        k_ref[:, h * HEAD_SLOT + LANES:(h + 1) * HEAD_SLOT] = (k_tail * kr).astype(BF16)

    vt_ref[0] = v_s[...].T

    xf = x_ref[...]
    u = (xf * _rms_scale(xf, D_MODEL) * ln_g_ref[...]).astype(BF16)
    u_ref[...] = u
    q_lat = _dot_nt(u, w_lat_ref[:Q_LORA_RANK, :])
    kv_lat = _dot_nt(u, w_lat_ref[Q_LORA_RANK:Q_LORA_RANK + KV_LORA_RANK + LANES, :])
    kr_s[...] = kv_lat[:, KV_LORA_RANK:]
    kv_lat = kv_lat[:, :KV_LORA_RANK]
    qn = (q_lat * _rms_scale(q_lat, Q_LORA_RANK) * qa_g_ref[...]).astype(BF16)
    kvn = (kv_lat * _rms_scale(kv_lat, KV_LORA_RANK) * kva_g_ref[...]).astype(BF16)
    qn_s[...] = _dot(qn, w_qn_ref[...])
    qr_s[...] = _dot(qn, w_qr_ref[...])
    kn_s[...] = _dot(kvn, w_kn_ref[...])
    v_s[...] = _dot(kvn, w_v_ref[...]).astype(BF16)


def _mla_prep(x2, pos, ln_g, w_lat, qa_g, kva_g, w_qn, w_qr, w_kn, w_v, qg_nope, qg_rope, kg_nope, kg_rope,
              inv_freq):
    t = x2.shape[0]
    tm = TM_PREP
    n = t // tm
    row = lambda i: (jnp.minimum(i, n - 1), 0)
    prev = lambda i: (jnp.maximum(i - 1, 0), 0)
    return pl.pallas_call(
        functools.partial(_mla_prep_kernel, q_scale=Q_SCALE),
        grid=(n + 1,),
        in_specs=[
            pl.BlockSpec((tm, D_MODEL), row),
            pl.BlockSpec((tm, 1), prev),
            _const_spec((1, D_MODEL)),
            _const_spec((LAT_WIDTH, D_MODEL)),
            _const_spec((1, Q_LORA_RANK)),
            _const_spec((1, KV_LORA_RANK)),
            _const_spec(w_qn.shape), _const_spec(w_qr.shape),
            _const_spec(w_kn.shape), _const_spec(w_v.shape),
            _const_spec((1, LANES)), _const_spec((1, LANES)),
            _const_spec((1, LANES)), _const_spec((1, LANES)),
            _const_spec((1, LANES)),
        ],
        out_specs=[
            pl.BlockSpec((tm, N_HEADS * HEAD_SLOT), prev),
            pl.BlockSpec((tm, N_HEADS * HEAD_SLOT), prev),
            pl.BlockSpec((1, N_HEADS * V_HEAD_DIM, tm), lambda i: (jnp.maximum(i - 1, 0), 0, 0)),
            pl.BlockSpec((tm, D_MODEL), row),
        ],
        out_shape=[
            jax.ShapeDtypeStruct((t, N_HEADS * HEAD_SLOT), BF16),
            jax.ShapeDtypeStruct((t, N_HEADS * HEAD_SLOT), BF16),
            jax.ShapeDtypeStruct((n, N_HEADS * V_HEAD_DIM, tm), BF16),
            jax.ShapeDtypeStruct((t, D_MODEL), BF16),
        ],
        scratch_shapes=[
            pltpu.VMEM((tm, N_HEADS * QK_NOPE_DIM), F32),
            pltpu.VMEM((tm, N_HEADS * QK_ROPE_DIM), F32),
            pltpu.VMEM((tm, N_HEADS * QK_NOPE_DIM), F32),
            pltpu.VMEM((tm, N_HEADS * V_HEAD_DIM), BF16),
            pltpu.VMEM((tm, LANES), F32),
        ],
        compiler_params=_params("arbitrary"),
        name="mla_prep",
    )(x2, pos, ln_g, w_lat, qa_g, kva_g, w_qn, w_qr, w_kn, w_v, qg_nope, qg_rope, kg_nope, kg_rope, inv_freq)


def _in_proj_kernel(u_ref, w_ref, z_ref):
    z_ref[...] = _dot_nt(u_ref[...], w_ref[...].astype(BF16)).astype(BF16)


def _in_proj(u, w_in_t):
    t = u.shape[0]
    tm, tn = TM_INPROJ, TN_INPROJ
    gate_tiles = 2 * D_MODEL // tn

    def w_rows(i, j):
        rows = jnp.where(j < gate_tiles, GATE_OFFSET + j * tn, (j - gate_tiles) * tn)
        return pl.multiple_of(rows, math.gcd(GATE_OFFSET, tn)), 0

    return pl.pallas_call(
        _in_proj_kernel,
        grid=(t // tm, Z_WIDTH // tn),
        in_specs=[
            pl.BlockSpec((tm, D_MODEL), lambda i, j: (i, 0)),
            pl.BlockSpec((pl.Element(tn), pl.Element(D_MODEL)), w_rows),
        ],
        out_specs=pl.BlockSpec((tm, tn), lambda i, j: (i, j)),
        out_shape=jax.ShapeDtypeStruct((t, Z_WIDTH), BF16),
        compiler_params=_params("arbitrary", "arbitrary"),
        name="in_proj",
    )(u, w_in_t)


def _attn_kernel(qg_ref, kg_ref, q_ref, k_ref, vt_ref, wc_f32_ref, wm_f32_ref, wo_f32_ref, wd_f32_ref,
                 o_ref, wc_bf16_ref, wm_bf16_ref, wo_bf16_ref, wd_bf16_ref, m_ref, l_ref, acc_ref,
                 *, tq, tk, vt_cols, q_scale):
    i = pl.program_id(2)
    sub_blocks = tk // vt_cols
    kv_per_q = tq // tk
    gq = jnp.max(jnp.abs(qg_ref[...]), axis=-1, keepdims=True)
    gk = jnp.max(jnp.abs(kg_ref[...]), axis=-1, keepdims=True)
    bound = (gq * gk)[0, 0] * (q_scale * QK_HEAD_DIM)
    fixed_shift = bound <= FIXED_SHIFT_LIMIT
    _cast_slabs((wc_f32_ref, wm_f32_ref, wo_f32_ref, wd_f32_ref),
                (wc_bf16_ref, wm_bf16_ref, wo_bf16_ref, wd_bf16_ref))

    def score(j, hh, col0):
        start = pl.multiple_of(j * tk, tk)
        q = q_ref[col0:, hh * HEAD_SLOT:(hh + 1) * HEAD_SLOT]
        kb = k_ref[pl.ds(start, tk), hh * HEAD_SLOT:(hh + 1) * HEAD_SLOT]
        return _dot_nt(kb, q)

    def per_head(j, col0, fn):
        sc = {hh: score(j, hh, col0) for hh in range(SCORES_AHEAD)}
        for hh in range(HEADS_PER_STEP):
            fn(hh, sc.pop(hh))
            if hh + SCORES_AHEAD < HEADS_PER_STEP:
                sc[hh + SCORES_AHEAD] = score(j, hh + SCORES_AHEAD, col0)

    def causal(s):
        kv_idx = lax.broadcasted_iota(jnp.int32, s.shape, 0)
        q_idx = lax.broadcasted_iota(jnp.int32, s.shape, 1)
        return jnp.where(kv_idx <= q_idx, s, NEG_INF)

    def p_times_v(j, hh, p):
        pb = p.astype(BF16)
        pv = None
        for c in range(sub_blocks):
            vt = vt_ref[j * sub_blocks + c, hh * V_HEAD_DIM:(hh + 1) * V_HEAD_DIM, :]
            part = _dot(vt, pb[c * vt_cols:(c + 1) * vt_cols, :])
            pv = part if pv is None else pv + part
        return pv

    def sweep(step):
        def body(j, c):
            step(j, 0, False)
            return c

        lax.fori_loop(0, i * kv_per_q, body, 0)
        for d in range(kv_per_q):
            step(i * kv_per_q + d, d * tk, True)
        for hh in range(HEADS_PER_STEP):
            o_ref[:, hh * V_HEAD_DIM:(hh + 1) * V_HEAD_DIM] = (acc_ref[hh] / l_ref[hh]).T.astype(BF16)

    l_ref[...] = jnp.zeros_like(l_ref)
    acc_ref[...] = jnp.zeros_like(acc_ref)

    @pl.when(fixed_shift)
    def _():
        def step(j, col0, masked):
            def head(hh, s):
                p = jnp.exp2((causal(s) if masked else s) - bound)
                l_ref[hh, :, col0:] += jnp.sum(p, axis=0, keepdims=True)
                acc_ref[hh, :, col0:] += p_times_v(j, hh, p)

            per_head(j, col0, head)

        sweep(step)

    @pl.when(jnp.logical_not(fixed_shift))
    def _():
        m_ref[...] = jnp.full_like(m_ref, NEG_INF)

        def step(j, col0, masked):
            def head(hh, s):
                s = causal(s) if masked else s
                m = m_ref[hh, :, col0:]
                m_new = jnp.maximum(m, jnp.max(s, axis=0, keepdims=True))
                alpha = jnp.exp2(m - m_new)
                p = jnp.exp2(s - m_new)
                m_ref[hh, :, col0:] = m_new
                l_ref[hh, :, col0:] = alpha * l_ref[hh, :, col0:] + jnp.sum(p, axis=0, keepdims=True)
                acc_ref[hh, :, col0:] = alpha * acc_ref[hh, :, col0:] + p_times_v(j, hh, p)

            per_head(j, col0, head)

        sweep(step)


def _attn(q, k, vt, q_norm_g, k_norm_g, later_weights, batch, seq):
    tq, tk = TQ_ATTN, TK_ATTN
    nq = seq // tq
    vt_cols = vt.shape[-1]
    hs = HEADS_PER_STEP
    groups = N_HEADS // hs
    cast_in, cast_out, cast_shape = _cast_sidecar(later_weights, batch * groups * nq,
                                                  lambda b, h, i: (b * groups + h) * nq + i)
    return pl.pallas_call(
        functools.partial(_attn_kernel, tq=tq, tk=tk, vt_cols=vt_cols, q_scale=Q_SCALE),
        grid=(batch, groups, nq),
        in_specs=[
            _const_spec((1, QK_HEAD_DIM)),
            _const_spec((1, QK_HEAD_DIM)),
            pl.BlockSpec((tq, hs * HEAD_SLOT), lambda b, h, i: (b * nq + i, h)),
            pl.BlockSpec((seq, hs * HEAD_SLOT), lambda b, h, i: (b, h)),
            pl.BlockSpec((seq // vt_cols, hs * V_HEAD_DIM, vt_cols), lambda b, h, i: (b, h, 0)),
        ] + cast_in,
        out_specs=[pl.BlockSpec((tq, hs * V_HEAD_DIM), lambda b, h, i: (b * nq + i, h))] + cast_out,
        out_shape=[jax.ShapeDtypeStruct((batch * seq, N_HEADS * V_HEAD_DIM), BF16)] + cast_shape,
        scratch_shapes=[
            pltpu.VMEM((hs, 1, tq), F32),
            pltpu.VMEM((hs, 1, tq), F32),
            pltpu.VMEM((hs, V_HEAD_DIM, tq), F32),
        ],
        compiler_params=_params("arbitrary", "arbitrary", "arbitrary"),
        name="attn",
    )(q_norm_g, k_norm_g, q, k, vt, *later_weights)


def _mix_out_kernel(gates_ref, zb_ref, zc_ref, zv_ref, zc_prev_ref, zv_prev_ref, o_ref, x_ref,
                    b_gate_ref, conv_w_ref, w_conv_out_ref, w_mla_out_ref, w_o_ref, ln2_g_ref, w_up_f32_ref,
                    h_ref, u2_ref, w_up_bf16_ref, *, tm, tiles_per_seq):
    for tile in range(D_FF // TF_FFN):
        for part in range(2):
            src = part * D_FF + tile * TF_FFN
            dst = (2 * tile + part) * TF_FFN
            w_up_bf16_ref[:, dst:dst + TF_FFN] = w_up_f32_ref[:, src:src + TF_FFN].astype(BF16)

    seq_start = (pl.program_id(0) % tiles_per_seq) == 0
    prev = zc_prev_ref[...].astype(F32) * zv_prev_ref[...].astype(F32)
    cv = jnp.concatenate([jnp.where(seq_start, 0.0, prev), zc_ref[...].astype(F32) * zv_ref[...].astype(F32)], axis=0)
    conv = _causal_conv3(cv, conv_w_ref[...], tm)
    conv_in = (zb_ref[...].astype(F32) * conv).astype(BF16)
    y_conv = _dot(conv_in, w_conv_out_ref[...])
    y_mla = _dot(o_ref[...], w_mla_out_ref[...])
    gates = jax.nn.sigmoid(gates_ref[...].astype(F32) + b_gate_ref[...])
    mix = (gates[:, :D_MODEL] * y_conv + gates[:, D_MODEL:] * y_mla).astype(BF16)
    h = x_ref[...] + _dot(mix, w_o_ref[...])
    h_ref[...] = h
    u2_ref[...] = (h * _rms_scale(h, D_MODEL) * ln2_g_ref[...]).astype(BF16)


def _mix_out(z, o, x2, b_gate, conv_w, w_conv_out, w_mla_out, w_o, ln2_g, w_ffn_up, seq):
    t = x2.shape[0]
    tm = TM_MIX
    halo_per_tile = tm // HALO
    cast_in, cast_out, cast_shape = _cast_sidecar((w_ffn_up,), t // tm, lambda i: i)
    cblk = lambda c: (lambda i: (i, c))
    prev = lambda c: (lambda i: (jnp.maximum(i * halo_per_tile - 1, 0), c))
    gate_blocks = 2 * D_MODEL // CONV_DIM
    return pl.pallas_call(
        functools.partial(_mix_out_kernel, tm=tm, tiles_per_seq=seq // tm),
        grid=(t // tm,),
        in_specs=[
            pl.BlockSpec((tm, 2 * D_MODEL), cblk(0)),
            pl.BlockSpec((tm, CONV_DIM), cblk(gate_blocks)),
            pl.BlockSpec((tm, CONV_DIM), cblk(gate_blocks + 1)),
            pl.BlockSpec((tm, CONV_DIM), cblk(gate_blocks + 2)),
            pl.BlockSpec((HALO, CONV_DIM), prev(gate_blocks + 1)),
            pl.BlockSpec((HALO, CONV_DIM), prev(gate_blocks + 2)),
            pl.BlockSpec((tm, D_MODEL), cblk(0)),
            pl.BlockSpec((tm, D_MODEL), cblk(0)),
            _const_spec((1, 2 * D_MODEL)),
            _const_spec((CONV_WIDTH, CONV_DIM)),
            _const_spec((CONV_DIM, D_MODEL)),
            _const_spec((D_MODEL, D_MODEL)),
            _const_spec((D_MODEL, D_MODEL)),
            _const_spec((1, D_MODEL)),
        ] + cast_in,
        out_specs=[pl.BlockSpec((tm, D_MODEL), cblk(0)), pl.BlockSpec((tm, D_MODEL), cblk(0))] + cast_out,
        out_shape=[jax.ShapeDtypeStruct((t, D_MODEL), F32), jax.ShapeDtypeStruct((t, D_MODEL), BF16)] + cast_shape,
        compiler_params=_params("arbitrary"),
        name="mix_out",
    )(z, z, z, z, z, z, o, x2, b_gate, conv_w, w_conv_out, w_mla_out, w_o, ln2_g, w_ffn_up)


def _ffn_kernel(u_ref, u_prev_ref, h_ref, w_up_ref, cw_ref, b_ref, wd_ref, out_ref, lhs_ref, p_ref,
                *, tm, tf, row_chunk, tiles_per_seq):
    @pl.when(pl.program_id(1) == 0)
    def _():
        seq_start = (pl.program_id(0) % tiles_per_seq) == 0
        lhs_ref[0:HALO, :] = jnp.where(seq_start, jnp.zeros_like(u_prev_ref), u_prev_ref[...])
        lhs_ref[HALO:, :] = u_ref[...]
        out_ref[...] = h_ref[...]

    bounds = [0] + [HALO + (c + 1) * row_chunk for c in range(tm // row_chunk)]
    for lo, hi in zip(bounds[:-1], bounds[1:]):
        p_ref[lo:hi, :] = _dot(lhs_ref[lo:hi, :], w_up_ref[...])

    cw = cw_ref[...]
    for r0 in range(0, tm, row_chunk):
        a = (cw[0:1, :] * p_ref[r0 + HALO - 2:r0 + HALO - 2 + row_chunk, :]
             + cw[1:2, :] * p_ref[r0 + HALO - 1:r0 + HALO - 1 + row_chunk, :]
             + cw[2:3, :] * p_ref[r0 + HALO:r0 + HALO + row_chunk, :]) + b_ref[...]
        a_gate, a_up = a[:, :tf], a[:, tf:]
        act = (a_gate * jax.nn.sigmoid(a_gate) * a_up).astype(BF16)
        out_ref[r0:r0 + row_chunk, :] += _dot(act, wd_ref[...])


def _pair_tiles(v, tf):
    rows = v.shape[0]
    return v.reshape(rows, 2, D_FF // tf, tf).transpose(0, 2, 1, 3).reshape(rows, 2 * D_FF)


def _ffn(u2, h1, w_up_paired, conv_w, conv_b, w_down, seq):
    t = u2.shape[0]
    tm, tf = TM_FFN, TF_FFN
    halo_per_tile = tm // HALO
    return pl.pallas_call(
        functools.partial(_ffn_kernel, tm=tm, tf=tf, row_chunk=FFN_ROW_CHUNK, tiles_per_seq=seq // tm),
        grid=(t // tm, D_FF // tf),
        in_specs=[
            pl.BlockSpec((tm, D_MODEL), lambda i, j: (i, 0)),
            pl.BlockSpec((HALO, D_MODEL), lambda i, j: (jnp.maximum(i * halo_per_tile - 1, 0), 0)),
            pl.BlockSpec((tm, D_MODEL), lambda i, j: (i, 0)),
            pl.BlockSpec((D_MODEL, 2 * tf), lambda i, j: (0, j)),
            pl.BlockSpec((CONV_WIDTH, 2 * tf), lambda i, j: (0, j)),
            pl.BlockSpec((1, 2 * tf), lambda i, j: (0, j)),
            pl.BlockSpec((tf, D_MODEL), lambda i, j: (j, 0)),
        ],
        out_specs=pl.BlockSpec((tm, D_MODEL), lambda i, j: (i, 0)),
        out_shape=jax.ShapeDtypeStruct((t, D_MODEL), F32),
        scratch_shapes=[
            pltpu.VMEM((tm + HALO, D_MODEL), BF16),
            pltpu.VMEM((tm + HALO, 2 * tf), F32),
        ],
        compiler_params=_params("arbitrary", "arbitrary"),
        name="ffn",
    )(u2, u2, h1, w_up_paired, _pair_tiles(conv_w, tf), _pair_tiles(conv_b, tf), w_down)


def _row(v):
    return v.reshape(1, -1).astype(F32)


def _layer(h2, pos, inv_freq, batch, seq, ln1_g, w_in, b_gate, conv_w, w_conv_out, q_a_g, w_q_b, kv_a_g, w_kv_b,
           q_norm_g, k_norm_g, w_mla_out, w_o, ln2_g, w_ffn_up, ffn_conv_w, ffn_conv_b, w_ffn_down):
    w_in_t = w_in.T
    w_lat = w_in_t[LAT_OFFSET:LAT_OFFSET + LAT_WIDTH].astype(BF16)
    wq3 = w_q_b.reshape(Q_LORA_RANK, N_HEADS, QK_HEAD_DIM)
    w_qn = wq3[:, :, :QK_NOPE_DIM].reshape(Q_LORA_RANK, -1).astype(BF16)
    w_qr = wq3[:, :, QK_NOPE_DIM:].reshape(Q_LORA_RANK, -1).astype(BF16)
    wkv3 = w_kv_b.reshape(KV_LORA_RANK, N_HEADS, QK_NOPE_DIM + V_HEAD_DIM)
    w_kn = wkv3[:, :, :QK_NOPE_DIM].reshape(KV_LORA_RANK, -1).astype(BF16)
    w_v = wkv3[:, :, QK_NOPE_DIM:].reshape(KV_LORA_RANK, -1).astype(BF16)
    twice = lambda g: _row(jnp.concatenate([g, g]))

    q, k, vt, u = _mla_prep(
        h2, pos, _row(ln1_g), w_lat, _row(q_a_g), _row(kv_a_g), w_qn, w_qr, w_kn, w_v,
        _row(q_norm_g[:QK_NOPE_DIM]), twice(q_norm_g[QK_NOPE_DIM:]),
        _row(k_norm_g[:QK_NOPE_DIM]), twice(k_norm_g[QK_NOPE_DIM:]), inv_freq)
    z = _in_proj(u, w_in_t)
    o, w_conv_out, w_mla_out, w_o, w_down = _attn(q, k, vt, _row(q_norm_g), _row(k_norm_g),
                                                  (w_conv_out, w_mla_out, w_o, w_ffn_down), batch, seq)
    h1, u2, w_up = _mix_out(z, o, h2, _row(b_gate), conv_w.astype(F32), w_conv_out, w_mla_out, w_o, _row(ln2_g),
                            w_ffn_up, seq)
    return _ffn(u2, h1, w_up, ffn_conv_w.astype(F32), _row(ffn_conv_b), w_down, seq)


def kernel(x, positions, ln1_g, w_in, b_gate, conv_w, w_conv_out, q_a_g, w_q_b, kv_a_g, w_kv_b,
           q_norm_g, k_norm_g, w_mla_out, w_o, ln2_g, w_ffn_up, ffn_conv_w, ffn_conv_b, w_ffn_down):
    batch, seq, _ = x.shape
    depth = ln1_g.shape[0]
    h2 = x.reshape(batch * seq, D_MODEL)
    pos = positions.reshape(batch * seq, 1).astype(jnp.int32)
    freq = ROPE_THETA ** (-jnp.arange(0, QK_ROPE_DIM, 2, dtype=F32) / QK_ROPE_DIM)
    inv_freq = jnp.tile(freq, LANES // (QK_ROPE_DIM // 2)).reshape(1, LANES)
    for l in range(depth):
        h2 = _layer(h2, pos, inv_freq, batch, seq, ln1_g[l], w_in[l], b_gate[l], conv_w[l], w_conv_out[l],
                    q_a_g[l], w_q_b[l], kv_a_g[l], w_kv_b[l], q_norm_g[l], k_norm_g[l], w_mla_out[l],
                    w_o[l], ln2_g[l], w_ffn_up[l], ffn_conv_w[l], ffn_conv_b[l], w_ffn_down[l])
    return h2.reshape(batch, seq, D_MODEL)
```

```python
import functools
import math

import jax
import jax.numpy as jnp
from jax import lax
from jax.experimental import pallas as pl
from jax.experimental.pallas import tpu as pltpu

D_MODEL = 2048
CONV_DIM = 1024
CONV_WIDTH = 3
N_HEADS = 16
QK_NOPE_DIM = 128
QK_ROPE_DIM = 64
QK_HEAD_DIM = QK_NOPE_DIM + QK_ROPE_DIM
V_HEAD_DIM = 128
Q_LORA_RANK = 768
KV_LORA_RANK = 512
ROPE_THETA = 10000.0
D_FF = 5632
NORM_EPS = 1e-6
NEG_INF = -1e30

Q_SCALE = QK_HEAD_DIM ** -0.5 * math.log2(math.e)
FIXED_SHIFT_LIMIT = 48.0

LANES = 128
HEAD_SLOT = 2 * LANES
HALO = 16
LAT_OFFSET = 3 * CONV_DIM
LAT_WIDTH = Q_LORA_RANK + KV_LORA_RANK + LANES
GATE_OFFSET = LAT_OFFSET + Q_LORA_RANK + KV_LORA_RANK + QK_ROPE_DIM
Z_WIDTH = 2 * D_MODEL + 3 * CONV_DIM
VMEM_LIMIT = 56 * 1024 * 1024

TM_PREP = 256
TM_INPROJ = 2048
TN_INPROJ = 1024
TQ_ATTN = 1024
TK_ATTN = 512
HEADS_PER_STEP = 4
SCORES_AHEAD = 4
TM_MIX = 256
TM_FFN = 512
TF_FFN = 512
FFN_ROW_CHUNK = 256

F32 = jnp.float32
BF16 = jnp.bfloat16


def _dot(a, b):
    return jnp.dot(a, b, preferred_element_type=F32)


def _dot_nt(a, b):
    return lax.dot_general(a, b, (((1,), (1,)), ((), ())), preferred_element_type=F32)


def _rms_scale(x, width):
    return lax.rsqrt(jnp.sum(x * x, axis=-1, keepdims=True) * (1.0 / width) + NORM_EPS)


def _params(*sem):
    return pltpu.CompilerParams(dimension_semantics=sem, vmem_limit_bytes=VMEM_LIMIT)


def _const_spec(shape):
    return pl.BlockSpec(shape, lambda *_: (0,) * len(shape), pipeline_mode=pl.Buffered(1))


def _cast_sidecar(weights, n_steps, step_index):
    in_specs, out_specs, out_shape = [], [], []
    for w in weights:
        rows, cols = w.shape
        slab = rows // n_steps
        assert slab * n_steps == rows and slab % 16 == 0, (w.shape, n_steps)
        spec = pl.BlockSpec((slab, cols), lambda *ids: (step_index(*ids), 0))
        in_specs.append(spec)
        out_specs.append(spec)
        out_shape.append(jax.ShapeDtypeStruct(w.shape, BF16))
    return in_specs, out_specs, out_shape


def _cast_slabs(src_refs, dst_refs):
    for src, dst in zip(src_refs, dst_refs):
        dst[...] = src[...].astype(BF16)


def _causal_conv3(p, cw, rows):
    full = cw[2:3, :] * p + cw[1:2, :] * pltpu.roll(p, 1, axis=0) + cw[0:1, :] * pltpu.roll(p, 2, axis=0)
    return full[p.shape[0] - rows:, :]


def _mla_prep_kernel(x_ref, pos_ref, ln_g_ref, w_lat_ref, qa_g_ref, kva_g_ref, w_qn_ref, w_qr_ref,
                     w_kv_ref, qg_nope_ref, qg_rope_ref, kg_nope_ref, kg_rope_ref, inv_freq_ref,
                     q_ref, k_ref, vt_ref, u_ref, qn_s, qr_s, kn_s, v_s, kr_s, *, q_scale):
    @pl.when(pl.program_id(0) == 0)
    def _():
        for s in (qn_s, qr_s, kn_s, v_s, kr_s):
            s[...] = jnp.zeros_like(s)

    ang = pos_ref[...].astype(F32) * inv_freq_ref[...]
    cos = jnp.cos(ang)
    sin = jnp.sin(ang)
    lane = lax.broadcasted_iota(jnp.int32, (1, LANES), 1)
    first_half = (lane % QK_ROPE_DIM) < (QK_ROPE_DIM // 2)
    sin_signed = jnp.where(first_half, -sin, sin)

    def rope(t):
        rot = jnp.where(first_half, pltpu.roll(t, LANES - QK_ROPE_DIM // 2, axis=1),
                        pltpu.roll(t, QK_ROPE_DIM // 2, axis=1))
        return t * cos + rot * sin_signed

    k_rope = kr_s[...]
    k_rope = jnp.where(lane < QK_ROPE_DIM, k_rope, pltpu.roll(k_rope, QK_ROPE_DIM, axis=1))
    qg_nope = qg_nope_ref[...] * q_scale
    qg_rope = qg_rope_ref[...] * q_scale
    kg_nope = kg_nope_ref[...]
    k_tail = rope(k_rope * kg_rope_ref[...])
    k_rope_ss = 0.5 * jnp.sum(k_rope * k_rope, axis=-1, keepdims=True)

    for pair in range(N_HEADS // 2):
        q_pair = qr_s[:, pair * LANES:(pair + 1) * LANES]
        q_pair_roped = rope(q_pair * qg_rope)
        for sub in range(2):
            h = 2 * pair + sub
            mine = (lane // QK_ROPE_DIM) == sub
            q_nope = qn_s[:, h * QK_NOPE_DIM:(h + 1) * QK_NOPE_DIM]
            ss = jnp.sum(q_nope * q_nope + jnp.where(mine, q_pair * q_pair, 0.0), axis=-1, keepdims=True)
            r = lax.rsqrt(ss * (1.0 / QK_HEAD_DIM) + NORM_EPS)
            q_ref[:, h * HEAD_SLOT: h * HEAD_SLOT + LANES] = (q_nope * r * qg_nope).astype(BF16)
            q_ref[:, h * HEAD_SLOT + LANES:(h + 1) * HEAD_SLOT] = jnp.where(
                mine, q_pair_roped * r, 0.0).astype(BF16)

    for h in range(N_HEADS):
        k_nope = kn_s[:, h * QK_NOPE_DIM:(h + 1) * QK_NOPE_DIM]
        kss = jnp.sum(k_nope * k_nope, axis=-1, keepdims=True) + k_rope_ss
        kr = lax.rsqrt(kss * (1.0 / QK_HEAD_DIM) + NORM_EPS)
        k_ref[:, h * HEAD_SLOT: h * HEAD_SLOT + LANES] = (k_nope * kr * kg_nope).astype(BF16)
        k_ref[:, h * HEAD_SLOT + LANES:(h + 1) * HEAD_SLOT] = (k_tail * kr).astype(BF16)

    vt_ref[0] = v_s[...].T

    xf = x_ref[...]
    u = (xf * _rms_scale(xf, D_MODEL) * ln_g_ref[...]).astype(BF16)
    u_ref[...] = u
    q_lat = _dot_nt(u, w_lat_ref[:Q_LORA_RANK, :])
    kv_lat = _dot_nt(u, w_lat_ref[Q_LORA_RANK:Q_LORA_RANK + KV_LORA_RANK + LANES, :])
    kr_s[...] = kv_lat[:, KV_LORA_RANK:]
    kv_lat = kv_lat[:, :KV_LORA_RANK]
    qn = (q_lat * _rms_scale(q_lat, Q_LORA_RANK) * qa_g_ref[...]).astype(BF16)
    kvn = (kv_lat * _rms_scale(kv_lat, KV_LORA_RANK) * kva_g_ref[...]).astype(BF16)
    qn_s[...] = _dot(qn, w_qn_ref[...])
    qr_s[...] = _dot(qn, w_qr_ref[...])
    kv = _dot(kvn, w_kv_ref[...])
    for h in range(N_HEADS):
        k0 = h * (QK_NOPE_DIM + V_HEAD_DIM)
        v0 = k0 + QK_NOPE_DIM
        kn_s[:, h * QK_NOPE_DIM:(h + 1) * QK_NOPE_DIM] = kv[:, k0:v0]
        v_s[:, h * V_HEAD_DIM:(h + 1) * V_HEAD_DIM] = kv[:, v0:v0 + V_HEAD_DIM].astype(BF16)


def _mla_prep(x2, pos, ln_g, w_lat, qa_g, kva_g, w_qn, w_qr, w_kv, qg_nope, qg_rope, kg_nope, kg_rope, inv_freq):
    t = x2.shape[0]
    tm = TM_PREP
    n = t // tm
    row = lambda i: (jnp.minimum(i, n - 1), 0)
    prev = lambda i: (jnp.maximum(i - 1, 0), 0)
    return pl.pallas_call(
        functools.partial(_mla_prep_kernel, q_scale=Q_SCALE),
        grid=(n + 1,),
        in_specs=[
            pl.BlockSpec((tm, D_MODEL), row),
            pl.BlockSpec((tm, 1), prev),
            _const_spec((1, D_MODEL)),
            _const_spec((LAT_WIDTH, D_MODEL)),
            _const_spec((1, Q_LORA_RANK)),
            _const_spec((1, KV_LORA_RANK)),
            _const_spec(w_qn.shape), _const_spec(w_qr.shape), _const_spec(w_kv.shape),
            _const_spec((1, LANES)), _const_spec((1, LANES)),
            _const_spec((1, LANES)), _const_spec((1, LANES)),
            _const_spec((1, LANES)),
        ],
        out_specs=[
            pl.BlockSpec((tm, N_HEADS * HEAD_SLOT), prev),
            pl.BlockSpec((tm, N_HEADS * HEAD_SLOT), prev),
            pl.BlockSpec((1, N_HEADS * V_HEAD_DIM, tm), lambda i: (jnp.maximum(i - 1, 0), 0, 0)),
            pl.BlockSpec((tm, D_MODEL), row),
        ],
        out_shape=[
            jax.ShapeDtypeStruct((t, N_HEADS * HEAD_SLOT), BF16),
            jax.ShapeDtypeStruct((t, N_HEADS * HEAD_SLOT), BF16),
            jax.ShapeDtypeStruct((n, N_HEADS * V_HEAD_DIM, tm), BF16),
            jax.ShapeDtypeStruct((t, D_MODEL), BF16),
        ],
        scratch_shapes=[
            pltpu.VMEM((tm, N_HEADS * QK_NOPE_DIM), F32),
            pltpu.VMEM((tm, N_HEADS * QK_ROPE_DIM), F32),
            pltpu.VMEM((tm, N_HEADS * QK_NOPE_DIM), F32),
            pltpu.VMEM((tm, N_HEADS * V_HEAD_DIM), BF16),
            pltpu.VMEM((tm, LANES), F32),
        ],
        compiler_params=_params("arbitrary"),
        name="mla_prep",
    )(x2, pos, ln_g, w_lat, qa_g, kva_g, w_qn, w_qr, w_kv, qg_nope, qg_rope, kg_nope, kg_rope, inv_freq)


def _in_proj_kernel(u_ref, w_ref, z_ref):
    z_ref[...] = _dot_nt(u_ref[...], w_ref[...].astype(BF16)).astype(BF16)


def _in_proj(u, w_in_t):
    t = u.shape[0]
    tm, tn = TM_INPROJ, TN_INPROJ
    gate_tiles = 2 * D_MODEL // tn

    def w_rows(i, j):
        rows = jnp.where(j < gate_tiles, GATE_OFFSET + j * tn, (j - gate_tiles) * tn)
        return pl.multiple_of(rows, math.gcd(GATE_OFFSET, tn)), 0

    return pl.pallas_call(
        _in_proj_kernel,
        grid=(t // tm, Z_WIDTH // tn),
        in_specs=[
            pl.BlockSpec((tm, D_MODEL), lambda i, j: (i, 0)),
            pl.BlockSpec((pl.Element(tn), pl.Element(D_MODEL)), w_rows),
        ],
        out_specs=pl.BlockSpec((tm, tn), lambda i, j: (i, j)),
        out_shape=jax.ShapeDtypeStruct((t, Z_WIDTH), BF16),
        compiler_params=_params("arbitrary", "arbitrary"),
        name="in_proj",
    )(u, w_in_t)


def _attn_kernel(qg_ref, kg_ref, q_ref, k_ref, vt_ref, wc_f32_ref, wm_f32_ref, wo_f32_ref, wd_f32_ref,
                 o_ref, wc_bf16_ref, wm_bf16_ref, wo_bf16_ref, wd_bf16_ref, m_ref, l_ref, acc_ref,
                 *, tq, tk, vt_cols, q_scale):
    i = pl.program_id(2)
    sub_blocks = tk // vt_cols
    kv_per_q = tq // tk
    gq = jnp.max(jnp.abs(qg_ref[...]), axis=-1, keepdims=True)
    gk = jnp.max(jnp.abs(kg_ref[...]), axis=-1, keepdims=True)
    bound = (gq * gk)[0, 0] * (q_scale * QK_HEAD_DIM)
    fixed_shift = bound <= FIXED_SHIFT_LIMIT
    _cast_slabs((wc_f32_ref, wm_f32_ref, wo_f32_ref, wd_f32_ref),
                (wc_bf16_ref, wm_bf16_ref, wo_bf16_ref, wd_bf16_ref))

    def score(j, hh, col0):
        start = pl.multiple_of(j * tk, tk)
        q = q_ref[col0:, hh * HEAD_SLOT:(hh + 1) * HEAD_SLOT]
        kb = k_ref[pl.ds(start, tk), hh * HEAD_SLOT:(hh + 1) * HEAD_SLOT]
        return _dot_nt(kb, q)

    def per_head(j, col0, fn):
        sc = {hh: score(j, hh, col0) for hh in range(SCORES_AHEAD)}
        for hh in range(HEADS_PER_STEP):
            fn(hh, sc.pop(hh))
            if hh + SCORES_AHEAD < HEADS_PER_STEP:
                sc[hh + SCORES_AHEAD] = score(j, hh + SCORES_AHEAD, col0)

    def causal(s):
        kv_idx = lax.broadcasted_iota(jnp.int32, s.shape, 0)
        q_idx = lax.broadcasted_iota(jnp.int32, s.shape, 1)
        return jnp.where(kv_idx <= q_idx, s, NEG_INF)

    def p_times_v(j, hh, p):
        pb = p.astype(BF16)
        pv = None
        for c in range(sub_blocks):
            vt = vt_ref[j * sub_blocks + c, hh * V_HEAD_DIM:(hh + 1) * V_HEAD_DIM, :]
            part = _dot(vt, pb[c * vt_cols:(c + 1) * vt_cols, :])
            pv = part if pv is None else pv + part
        return pv

    def sweep(step):
        def body(g, c):
            for d in range(kv_per_q):
                step(g * kv_per_q + d, 0, False)
            return c

        lax.fori_loop(0, i, body, 0)
        for d in range(kv_per_q):
            step(i * kv_per_q + d, d * tk, True)
        for hh in range(HEADS_PER_STEP):
            o_ref[:, hh * V_HEAD_DIM:(hh + 1) * V_HEAD_DIM] = (acc_ref[hh] / l_ref[hh]).T.astype(BF16)

    l_ref[...] = jnp.zeros_like(l_ref)
    acc_ref[...] = jnp.zeros_like(acc_ref)

    @pl.when(fixed_shift)
    def _():
        def step(j, col0, masked):
            def head(hh, s):
                p = jnp.exp2((causal(s) if masked else s) - bound)
                l_ref[hh, :, col0:] += jnp.sum(p, axis=0, keepdims=True)
                acc_ref[hh, :, col0:] += p_times_v(j, hh, p)

            per_head(j, col0, head)

        sweep(step)

    @pl.when(jnp.logical_not(fixed_shift))
    def _():
        m_ref[...] = jnp.full_like(m_ref, NEG_INF)

        def step(j, col0, masked):
            def head(hh, s):
                s = causal(s) if masked else s
                m = m_ref[hh, :, col0:]
                m_new = jnp.maximum(m, jnp.max(s, axis=0, keepdims=True))
                alpha = jnp.exp2(m - m_new)
                p = jnp.exp2(s - m_new)
                m_ref[hh, :, col0:] = m_new
                l_ref[hh, :, col0:] = alpha * l_ref[hh, :, col0:] + jnp.sum(p, axis=0, keepdims=True)
                acc_ref[hh, :, col0:] = alpha * acc_ref[hh, :, col0:] + p_times_v(j, hh, p)

            per_head(j, col0, head)

        sweep(step)


def _attn(q, k, vt, q_norm_g, k_norm_g, later_weights, batch, seq):
    tq, tk = TQ_ATTN, TK_ATTN
    nq = seq // tq
    vt_cols = vt.shape[-1]
    hs = HEADS_PER_STEP
    groups = N_HEADS // hs
    cast_in, cast_out, cast_shape = _cast_sidecar(later_weights, batch * groups * nq,
                                                  lambda b, h, i: (b * groups + h) * nq + i)
    return pl.pallas_call(
        functools.partial(_attn_kernel, tq=tq, tk=tk, vt_cols=vt_cols, q_scale=Q_SCALE),
        grid=(batch, groups, nq),
        in_specs=[
            _const_spec((1, QK_HEAD_DIM)),
            _const_spec((1, QK_HEAD_DIM)),
            pl.BlockSpec((tq, hs * HEAD_SLOT), lambda b, h, i: (b * nq + i, h)),
            pl.BlockSpec((seq, hs * HEAD_SLOT), lambda b, h, i: (b, h)),
            pl.BlockSpec((seq // vt_cols, hs * V_HEAD_DIM, vt_cols), lambda b, h, i: (b, h, 0)),
        ] + cast_in,
        out_specs=[pl.BlockSpec((tq, hs * V_HEAD_DIM), lambda b, h, i: (b * nq + i, h))] + cast_out,
        out_shape=[jax.ShapeDtypeStruct((batch * seq, N_HEADS * V_HEAD_DIM), BF16)] + cast_shape,
        scratch_shapes=[
            pltpu.VMEM((hs, 1, tq), F32),
            pltpu.VMEM((hs, 1, tq), F32),
            pltpu.VMEM((hs, V_HEAD_DIM, tq), F32),
        ],
        compiler_params=_params("arbitrary", "arbitrary", "arbitrary"),
        name="attn",
    )(q_norm_g, k_norm_g, q, k, vt, *later_weights)


def _mix_out_kernel(gates_ref, zb_ref, zc_ref, zv_ref, zc_prev_ref, zv_prev_ref, o_ref, x_ref,
                    b_gate_ref, conv_w_ref, w_conv_out_ref, w_mla_out_ref, w_o_ref, ln2_g_ref, w_up_f32_ref,
                    h_ref, u2_ref, w_up_bf16_ref, *, tm, tiles_per_seq):
    for tile in range(D_FF // TF_FFN):
        for part in range(2):
            src = part * D_FF + tile * TF_FFN
            dst = (2 * tile + part) * TF_FFN
            w_up_bf16_ref[:, dst:dst + TF_FFN] = w_up_f32_ref[:, src:src + TF_FFN].astype(BF16)

    seq_start = (pl.program_id(0) % tiles_per_seq) == 0
    prev = zc_prev_ref[...].astype(F32) * zv_prev_ref[...].astype(F32)
    cv = jnp.concatenate([jnp.where(seq_start, 0.0, prev), zc_ref[...].astype(F32) * zv_ref[...].astype(F32)], axis=0)
    conv = _causal_conv3(cv, conv_w_ref[...], tm)
    conv_in = (zb_ref[...].astype(F32) * conv).astype(BF16)
    y_conv = _dot(conv_in, w_conv_out_ref[...])
    y_mla = _dot(o_ref[...], w_mla_out_ref[...])
    gates = jax.nn.sigmoid(gates_ref[...].astype(F32) + b_gate_ref[...])
    mix = (gates[:, :D_MODEL] * y_conv + gates[:, D_MODEL:] * y_mla).astype(BF16)
    h = x_ref[...] + _dot(mix, w_o_ref[...])
    h_ref[...] = h
    u2_ref[...] = (h * _rms_scale(h, D_MODEL) * ln2_g_ref[...]).astype(BF16)


def _mix_out(z, o, x2, b_gate, conv_w, w_conv_out, w_mla_out, w_o, ln2_g, w_ffn_up, seq):
    t = x2.shape[0]
    tm = TM_MIX
    halo_per_tile = tm // HALO
    cast_in, cast_out, cast_shape = _cast_sidecar((w_ffn_up,), t // tm, lambda i: i)
    cblk = lambda c: (lambda i: (i, c))
    prev = lambda c: (lambda i: (jnp.maximum(i * halo_per_tile - 1, 0), c))
    gate_blocks = 2 * D_MODEL // CONV_DIM
    return pl.pallas_call(
        functools.partial(_mix_out_kernel, tm=tm, tiles_per_seq=seq // tm),
        grid=(t // tm,),
        in_specs=[
            pl.BlockSpec((tm, 2 * D_MODEL), cblk(0)),
            pl.BlockSpec((tm, CONV_DIM), cblk(gate_blocks)),
            pl.BlockSpec((tm, CONV_DIM), cblk(gate_blocks + 1)),
            pl.BlockSpec((tm, CONV_DIM), cblk(gate_blocks + 2)),
            pl.BlockSpec((HALO, CONV_DIM), prev(gate_blocks + 1)),
            pl.BlockSpec((HALO, CONV_DIM), prev(gate_blocks + 2)),
            pl.BlockSpec((tm, D_MODEL), cblk(0)),
            pl.BlockSpec((tm, D_MODEL), cblk(0)),
            _const_spec((1, 2 * D_MODEL)),
            _const_spec((CONV_WIDTH, CONV_DIM)),
            _const_spec((CONV_DIM, D_MODEL)),
            _const_spec((D_MODEL, D_MODEL)),
            _const_spec((D_MODEL, D_MODEL)),
            _const_spec((1, D_MODEL)),
        ] + cast_in,
        out_specs=[pl.BlockSpec((tm, D_MODEL), cblk(0)), pl.BlockSpec((tm, D_MODEL), cblk(0))] + cast_out,
        out_shape=[jax.ShapeDtypeStruct((t, D_MODEL), F32), jax.ShapeDtypeStruct((t, D_MODEL), BF16)] + cast_shape,
        compiler_params=_params("arbitrary"),
        name="mix_out",
    )(z, z, z, z, z, z, o, x2, b_gate, conv_w, w_conv_out, w_mla_out, w_o, ln2_g, w_ffn_up)


def _ffn_kernel(u_ref, u_prev_ref, h_ref, w_up_ref, cw_ref, b_ref, wd_ref, out_ref, lhs_ref, p_ref,
                *, tm, tf, row_chunk, tiles_per_seq):
    @pl.when(pl.program_id(1) == 0)
    def _():
        seq_start = (pl.program_id(0) % tiles_per_seq) == 0
        lhs_ref[0:HALO, :] = jnp.where(seq_start, jnp.zeros_like(u_prev_ref), u_prev_ref[...])
        lhs_ref[HALO:, :] = u_ref[...]
        out_ref[...] = h_ref[...]

    bounds = [0] + [HALO + (c + 1) * row_chunk for c in range(tm // row_chunk)]
    for lo, hi in zip(bounds[:-1], bounds[1:]):
        p_ref[lo:hi, :] = _dot(lhs_ref[lo:hi, :], w_up_ref[...])

    cw = cw_ref[...]
    for r0 in range(0, tm, row_chunk):
        a = (cw[0:1, :] * p_ref[r0 + HALO - 2:r0 + HALO - 2 + row_chunk, :]
             + cw[1:2, :] * p_ref[r0 + HALO - 1:r0 + HALO - 1 + row_chunk, :]
             + cw[2:3, :] * p_ref[r0 + HALO:r0 + HALO + row_chunk, :]) + b_ref[...]
        a_gate, a_up = a[:, :tf], a[:, tf:]
        act = (a_gate * jax.nn.sigmoid(a_gate) * a_up).astype(BF16)
        out_ref[r0:r0 + row_chunk, :] += _dot(act, wd_ref[...])


def _pair_tiles(v, tf):
    rows = v.shape[0]
    return v.reshape(rows, 2, D_FF // tf, tf).transpose(0, 2, 1, 3).reshape(rows, 2 * D_FF)


def _ffn(u2, h1, w_up_paired, conv_w, conv_b, w_down, seq):
    t = u2.shape[0]
    tm, tf = TM_FFN, TF_FFN
    halo_per_tile = tm // HALO
    return pl.pallas_call(
        functools.partial(_ffn_kernel, tm=tm, tf=tf, row_chunk=FFN_ROW_CHUNK, tiles_per_seq=seq // tm),
        grid=(t // tm, D_FF // tf),
        in_specs=[
            pl.BlockSpec((tm, D_MODEL), lambda i, j: (i, 0)),
            pl.BlockSpec((HALO, D_MODEL), lambda i, j: (jnp.maximum(i * halo_per_tile - 1, 0), 0)),
            pl.BlockSpec((tm, D_MODEL), lambda i, j: (i, 0)),
            pl.BlockSpec((D_MODEL, 2 * tf), lambda i, j: (0, j)),
            pl.BlockSpec((CONV_WIDTH, 2 * tf), lambda i, j: (0, j)),
            pl.BlockSpec((1, 2 * tf), lambda i, j: (0, j)),
            pl.BlockSpec((tf, D_MODEL), lambda i, j: (j, 0)),
        ],
        out_specs=pl.BlockSpec((tm, D_MODEL), lambda i, j: (i, 0)),
        out_shape=jax.ShapeDtypeStruct((t, D_MODEL), F32),
        scratch_shapes=[
            pltpu.VMEM((tm + HALO, D_MODEL), BF16),
            pltpu.VMEM((tm + HALO, 2 * tf), F32),
        ],
        compiler_params=_params("arbitrary", "arbitrary"),
        name="ffn",
    )(u2, u2, h1, w_up_paired, _pair_tiles(conv_w, tf), _pair_tiles(conv_b, tf), w_down)


def _row(v):
    return v.reshape(1, -1).astype(F32)


def _layer(h2, pos, inv_freq, batch, seq, ln1_g, w_in, b_gate, conv_w, w_conv_out, q_a_g, w_q_b, kv_a_g, w_kv_b,
           q_norm_g, k_norm_g, w_mla_out, w_o, ln2_g, w_ffn_up, ffn_conv_w, ffn_conv_b, w_ffn_down):
    w_in_t = w_in.T
    w_lat = w_in_t[LAT_OFFSET:LAT_OFFSET + LAT_WIDTH].astype(BF16)
    wq3 = w_q_b.reshape(Q_LORA_RANK, N_HEADS, QK_HEAD_DIM)
    w_qn = wq3[:, :, :QK_NOPE_DIM].reshape(Q_LORA_RANK, -1).astype(BF16)
    w_qr = wq3[:, :, QK_NOPE_DIM:].reshape(Q_LORA_RANK, -1).astype(BF16)
    w_kv = w_kv_b.astype(BF16)
    twice = lambda g: _row(jnp.concatenate([g, g]))

    q, k, vt, u = _mla_prep(
        h2, pos, _row(ln1_g), w_lat, _row(q_a_g), _row(kv_a_g), w_qn, w_qr, w_kv,
        _row(q_norm_g[:QK_NOPE_DIM]), twice(q_norm_g[QK_NOPE_DIM:]),
        _row(k_norm_g[:QK_NOPE_DIM]), twice(k_norm_g[QK_NOPE_DIM:]), inv_freq)
    z = _in_proj(u, w_in_t)
    o, w_conv_out, w_mla_out, w_o, w_down = _attn(q, k, vt, _row(q_norm_g), _row(k_norm_g),
                                                  (w_conv_out, w_mla_out, w_o, w_ffn_down), batch, seq)
    h1, u2, w_up = _mix_out(z, o, h2, _row(b_gate), conv_w.astype(F32), w_conv_out, w_mla_out, w_o, _row(ln2_g),
                            w_ffn_up, seq)
    return _ffn(u2, h1, w_up, ffn_conv_w.astype(F32), _row(ffn_conv_b), w_down, seq)


def kernel(x, positions, ln1_g, w_in, b_gate, conv_w, w_conv_out, q_a_g, w_q_b, kv_a_g, w_kv_b,
           q_norm_g, k_norm_g, w_mla_out, w_o, ln2_g, w_ffn_up, ffn_conv_w, ffn_conv_b, w_ffn_down):
    batch, seq, _ = x.shape
    depth = ln1_g.shape[0]
    h2 = x.reshape(batch * seq, D_MODEL)
    pos = positions.reshape(batch * seq, 1).astype(jnp.int32)
    freq = ROPE_THETA ** (-jnp.arange(0, QK_ROPE_DIM, 2, dtype=F32) / QK_ROPE_DIM)
    inv_freq = jnp.tile(freq, LANES // (QK_ROPE_DIM // 2)).reshape(1, LANES)
    for l in range(depth):
        h2 = _layer(h2, pos, inv_freq, batch, seq, ln1_g[l], w_in[l], b_gate[l], conv_w[l], w_conv_out[l],
                    q_a_g[l], w_q_b[l], kv_a_g[l], w_kv_b[l], q_norm_g[l], k_norm_g[l], w_mla_out[l],
                    w_o[l], ln2_g[l], w_ffn_up[l], ffn_conv_w[l], ffn_conv_b[l], w_ffn_down[l])
    return h2.reshape(batch, seq, D_MODEL)
```

```python
import functools
import math

import jax
import jax.numpy as jnp
from jax import lax
from jax.experimental import pallas as pl
from jax.experimental.pallas import tpu as pltpu

D_MODEL = 2048
CONV_DIM = 1024
CONV_WIDTH = 3
N_HEADS = 16
QK_NOPE_DIM = 128
QK_ROPE_DIM = 64
QK_HEAD_DIM = QK_NOPE_DIM + QK_ROPE_DIM
V_HEAD_DIM = 128
Q_LORA_RANK = 768
KV_LORA_RANK = 512
ROPE_THETA = 10000.0
D_FF = 5632
NORM_EPS = 1e-6
NEG_INF = -1e30

Q_SCALE = QK_HEAD_DIM ** -0.5 * math.log2(math.e)
FIXED_SHIFT_LIMIT = 48.0

LANES = 128
HEAD_SLOT = 2 * LANES
HALO = 16
LAT_OFFSET = 3 * CONV_DIM
LAT_WIDTH = Q_LORA_RANK + KV_LORA_RANK + LANES
GATE_OFFSET = LAT_OFFSET + Q_LORA_RANK + KV_LORA_RANK + QK_ROPE_DIM
Z_WIDTH = 2 * D_MODEL + 3 * CONV_DIM
VMEM_LIMIT = 56 * 1024 * 1024

TM_PREP = 256
TM_INPROJ = 2048
TN_INPROJ = 1024
TQ_ATTN = 1024
TK_ATTN = 512
HEADS_PER_STEP = 4
SCORES_AHEAD = 4
TM_MIX = 256
TM_FFN = 512
TF_FFN = 512
FFN_ROW_CHUNK = 256
FFN_TILE_GROUP = 2

F32 = jnp.float32
BF16 = jnp.bfloat16


def _dot(a, b):
    return jnp.dot(a, b, preferred_element_type=F32)


def _dot_nt(a, b):
    return lax.dot_general(a, b, (((1,), (1,)), ((), ())), preferred_element_type=F32)


def _rms_scale(x, width):
    return lax.rsqrt(jnp.sum(x * x, axis=-1, keepdims=True) * (1.0 / width) + NORM_EPS)


def _params(*sem):
    return pltpu.CompilerParams(dimension_semantics=sem, vmem_limit_bytes=VMEM_LIMIT)


def _const_spec(shape):
    return pl.BlockSpec(shape, lambda *_: (0,) * len(shape), pipeline_mode=pl.Buffered(1))


def _cast_sidecar(weights, n_steps, step_index):
    in_specs, out_specs, out_shape = [], [], []
    for w in weights:
        rows, cols = w.shape
        slab = rows // n_steps
        assert slab * n_steps == rows and slab % 16 == 0, (w.shape, n_steps)
        spec = pl.BlockSpec((slab, cols), lambda *ids: (step_index(*ids), 0))
        in_specs.append(spec)
        out_specs.append(spec)
        out_shape.append(jax.ShapeDtypeStruct(w.shape, BF16))
    return in_specs, out_specs, out_shape


def _cast_slabs(src_refs, dst_refs):
    for src, dst in zip(src_refs, dst_refs):
        dst[...] = src[...].astype(BF16)


def _causal_conv3(p, cw, rows):
    full = cw[2:3, :] * p + cw[1:2, :] * pltpu.roll(p, 1, axis=0) + cw[0:1, :] * pltpu.roll(p, 2, axis=0)
    return full[p.shape[0] - rows:, :]


def _mla_prep_kernel(x_ref, pos_ref, ln_g_ref, w_lat_ref, qa_g_ref, kva_g_ref, w_qn_ref, w_qr_ref,
                     w_kv_ref, qg_nope_ref, qg_rope_ref, kg_nope_ref, kg_rope_ref, inv_freq_ref,
                     q_ref, k_ref, vt_ref, u_ref, qn_s, qr_s, kn_s, v_s, kr_s, *, q_scale):
    @pl.when(pl.program_id(0) == 0)
    def _():
        for s in (qn_s, qr_s, kn_s, v_s, kr_s):
            s[...] = jnp.zeros_like(s)

    ang = pos_ref[...].astype(F32) * inv_freq_ref[...]
    cos = jnp.cos(ang)
    sin = jnp.sin(ang)
    lane = lax.broadcasted_iota(jnp.int32, (1, LANES), 1)
    first_half = (lane % QK_ROPE_DIM) < (QK_ROPE_DIM // 2)
    sin_signed = jnp.where(first_half, -sin, sin)

    def rope(t):
        rot = jnp.where(first_half, pltpu.roll(t, LANES - QK_ROPE_DIM // 2, axis=1),
                        pltpu.roll(t, QK_ROPE_DIM // 2, axis=1))
        return t * cos + rot * sin_signed

    k_rope = kr_s[...]
    k_rope = jnp.where(lane < QK_ROPE_DIM, k_rope, pltpu.roll(k_rope, QK_ROPE_DIM, axis=1))
    qg_nope = qg_nope_ref[...] * q_scale
    qg_rope = qg_rope_ref[...] * q_scale
    kg_nope = kg_nope_ref[...]
    k_tail = rope(k_rope * kg_rope_ref[...])
    k_rope_ss = 0.5 * jnp.sum(k_rope * k_rope, axis=-1, keepdims=True)

    for pair in range(N_HEADS // 2):
        q_pair = qr_s[:, pair * LANES:(pair + 1) * LANES]
        q_pair_roped = rope(q_pair * qg_rope)
        for sub in range(2):
            h = 2 * pair + sub
            mine = (lane // QK_ROPE_DIM) == sub
            q_nope = qn_s[:, h * QK_NOPE_DIM:(h + 1) * QK_NOPE_DIM]
            ss = jnp.sum(q_nope * q_nope + jnp.where(mine, q_pair * q_pair, 0.0), axis=-1, keepdims=True)
            r = lax.rsqrt(ss * (1.0 / QK_HEAD_DIM) + NORM_EPS)
            q_ref[:, h * HEAD_SLOT: h * HEAD_SLOT + LANES] = (q_nope * r * qg_nope).astype(BF16)
            q_ref[:, h * HEAD_SLOT + LANES:(h + 1) * HEAD_SLOT] = jnp.where(
                mine, q_pair_roped * r, 0.0).astype(BF16)

    for h in range(N_HEADS):
        k_nope = kn_s[:, h * QK_NOPE_DIM:(h + 1) * QK_NOPE_DIM]
        kss = jnp.sum(k_nope * k_nope, axis=-1, keepdims=True) + k_rope_ss
        kr = lax.rsqrt(kss * (1.0 / QK_HEAD_DIM) + NORM_EPS)
        k_ref[:, h * HEAD_SLOT: h * HEAD_SLOT + LANES] = (k_nope * kr * kg_nope).astype(BF16)
        k_ref[:, h * HEAD_SLOT + LANES:(h + 1) * HEAD_SLOT] = (k_tail * kr).astype(BF16)

    vt_ref[0] = v_s[...].T

    xf = x_ref[...]
    u = (xf * _rms_scale(xf, D_MODEL) * ln_g_ref[...]).astype(BF16)
    u_ref[...] = u
    q_lat = _dot_nt(u, w_lat_ref[:Q_LORA_RANK, :])
    kv_lat = _dot_nt(u, w_lat_ref[Q_LORA_RANK:Q_LORA_RANK + KV_LORA_RANK + LANES, :])
    kr_s[...] = kv_lat[:, KV_LORA_RANK:]
    kv_lat = kv_lat[:, :KV_LORA_RANK]
    qn = (q_lat * _rms_scale(q_lat, Q_LORA_RANK) * qa_g_ref[...]).astype(BF16)
    kvn = (kv_lat * _rms_scale(kv_lat, KV_LORA_RANK) * kva_g_ref[...]).astype(BF16)
    qn_s[...] = _dot(qn, w_qn_ref[...])
    qr_s[...] = _dot(qn, w_qr_ref[...])
    kv = _dot(kvn, w_kv_ref[...])
    for h in range(N_HEADS):
        k0 = h * (QK_NOPE_DIM + V_HEAD_DIM)
        v0 = k0 + QK_NOPE_DIM
        kn_s[:, h * QK_NOPE_DIM:(h + 1) * QK_NOPE_DIM] = kv[:, k0:v0]
        v_s[:, h * V_HEAD_DIM:(h + 1) * V_HEAD_DIM] = kv[:, v0:v0 + V_HEAD_DIM].astype(BF16)


def _mla_prep(x2, pos, ln_g, w_lat, qa_g, kva_g, w_qn, w_qr, w_kv, qg_nope, qg_rope, kg_nope, kg_rope, inv_freq):
    t = x2.shape[0]
    tm = TM_PREP
    n = t // tm
    row = lambda i: (jnp.minimum(i, n - 1), 0)
    prev = lambda i: (jnp.maximum(i - 1, 0), 0)
    return pl.pallas_call(
        functools.partial(_mla_prep_kernel, q_scale=Q_SCALE),
        grid=(n + 1,),
        in_specs=[
            pl.BlockSpec((tm, D_MODEL), row),
            pl.BlockSpec((tm, 1), prev),
            _const_spec((1, D_MODEL)),
            _const_spec((LAT_WIDTH, D_MODEL)),
            _const_spec((1, Q_LORA_RANK)),
            _const_spec((1, KV_LORA_RANK)),
            _const_spec(w_qn.shape), _const_spec(w_qr.shape), _const_spec(w_kv.shape),
            _const_spec((1, LANES)), _const_spec((1, LANES)),
            _const_spec((1, LANES)), _const_spec((1, LANES)),
            _const_spec((1, LANES)),
        ],
        out_specs=[
            pl.BlockSpec((tm, N_HEADS * HEAD_SLOT), prev),
            pl.BlockSpec((tm, N_HEADS * HEAD_SLOT), prev),
            pl.BlockSpec((1, N_HEADS * V_HEAD_DIM, tm), lambda i: (jnp.maximum(i - 1, 0), 0, 0)),
            pl.BlockSpec((tm, D_MODEL), row),
        ],
        out_shape=[
            jax.ShapeDtypeStruct((t, N_HEADS * HEAD_SLOT), BF16),
            jax.ShapeDtypeStruct((t, N_HEADS * HEAD_SLOT), BF16),
            jax.ShapeDtypeStruct((n, N_HEADS * V_HEAD_DIM, tm), BF16),
            jax.ShapeDtypeStruct((t, D_MODEL), BF16),
        ],
        scratch_shapes=[
            pltpu.VMEM((tm, N_HEADS * QK_NOPE_DIM), F32),
            pltpu.VMEM((tm, N_HEADS * QK_ROPE_DIM), F32),
            pltpu.VMEM((tm, N_HEADS * QK_NOPE_DIM), F32),
            pltpu.VMEM((tm, N_HEADS * V_HEAD_DIM), BF16),
            pltpu.VMEM((tm, LANES), F32),
        ],
        compiler_params=_params("arbitrary"),
        name="mla_prep",
    )(x2, pos, ln_g, w_lat, qa_g, kva_g, w_qn, w_qr, w_kv, qg_nope, qg_rope, kg_nope, kg_rope, inv_freq)


def _in_proj_kernel(u_ref, w_ref, z_ref):
    z_ref[...] = _dot_nt(u_ref[...], w_ref[...].astype(BF16)).astype(BF16)


def _in_proj(u, w_in_t):
    t = u.shape[0]
    tm, tn = TM_INPROJ, TN_INPROJ
    gate_tiles = 2 * D_MODEL // tn

    def w_rows(i, j):
        rows = jnp.where(j < gate_tiles, GATE_OFFSET + j * tn, (j - gate_tiles) * tn)
        return pl.multiple_of(rows, math.gcd(GATE_OFFSET, tn)), 0

    return pl.pallas_call(
        _in_proj_kernel,
        grid=(t // tm, Z_WIDTH // tn),
        in_specs=[
            pl.BlockSpec((tm, D_MODEL), lambda i, j: (i, 0)),
            pl.BlockSpec((pl.Element(tn), pl.Element(D_MODEL)), w_rows),
        ],
        out_specs=pl.BlockSpec((tm, tn), lambda i, j: (i, j)),
        out_shape=jax.ShapeDtypeStruct((t, Z_WIDTH), BF16),
        compiler_params=_params("arbitrary", "arbitrary"),
        name="in_proj",
    )(u, w_in_t)


def _attn_kernel(qg_ref, kg_ref, q_ref, k_ref, vt_ref, wc_f32_ref, wm_f32_ref, wo_f32_ref, wd_f32_ref,
                 o_ref, wc_bf16_ref, wm_bf16_ref, wo_bf16_ref, wd_bf16_ref, m_ref, l_ref, acc_ref,
                 *, tq, tk, vt_cols, q_scale):
    i = pl.program_id(2)
    sub_blocks = tk // vt_cols
    kv_per_q = tq // tk
    gq = jnp.max(jnp.abs(qg_ref[...]), axis=-1, keepdims=True)
    gk = jnp.max(jnp.abs(kg_ref[...]), axis=-1, keepdims=True)
    bound = (gq * gk)[0, 0] * (q_scale * QK_HEAD_DIM)
    fixed_shift = bound <= FIXED_SHIFT_LIMIT
    _cast_slabs((wc_f32_ref, wm_f32_ref, wo_f32_ref, wd_f32_ref),
                (wc_bf16_ref, wm_bf16_ref, wo_bf16_ref, wd_bf16_ref))

    def score(j, hh, col0):
        start = pl.multiple_of(j * tk, tk)
        q = q_ref[col0:, hh * HEAD_SLOT:(hh + 1) * HEAD_SLOT]
        kb = k_ref[pl.ds(start, tk), hh * HEAD_SLOT:(hh + 1) * HEAD_SLOT]
        return _dot_nt(kb, q)

    def per_head(j, col0, fn):
        sc = {hh: score(j, hh, col0) for hh in range(SCORES_AHEAD)}
        for hh in range(HEADS_PER_STEP):
            fn(hh, sc.pop(hh))
            if hh + SCORES_AHEAD < HEADS_PER_STEP:
                sc[hh + SCORES_AHEAD] = score(j, hh + SCORES_AHEAD, col0)

    def causal(s):
        kv_idx = lax.broadcasted_iota(jnp.int32, s.shape, 0)
        q_idx = lax.broadcasted_iota(jnp.int32, s.shape, 1)
        return jnp.where(kv_idx <= q_idx, s, NEG_INF)

    def p_times_v(j, hh, p):
        pb = p.astype(BF16)
        pv = None
        for c in range(sub_blocks):
            vt = vt_ref[j * sub_blocks + c, hh * V_HEAD_DIM:(hh + 1) * V_HEAD_DIM, :]
            part = _dot(vt, pb[c * vt_cols:(c + 1) * vt_cols, :])
            pv = part if pv is None else pv + part
        return pv

    def sweep(step):
        def body(g, c):
            for d in range(kv_per_q):
                step(g * kv_per_q + d, 0, False)
            return c

        lax.fori_loop(0, i, body, 0)
        for d in range(kv_per_q):
            step(i * kv_per_q + d, d * tk, True)
        for hh in range(HEADS_PER_STEP):
            o_ref[:, hh * V_HEAD_DIM:(hh + 1) * V_HEAD_DIM] = (acc_ref[hh] / l_ref[hh]).T.astype(BF16)

    l_ref[...] = jnp.zeros_like(l_ref)
    acc_ref[...] = jnp.zeros_like(acc_ref)

    @pl.when(fixed_shift)
    def _():
        def step(j, col0, masked):
            def head(hh, s):
                p = jnp.exp2((causal(s) if masked else s) - bound)
                l_ref[hh, :, col0:] += jnp.sum(p, axis=0, keepdims=True)
                acc_ref[hh, :, col0:] += p_times_v(j, hh, p)

            per_head(j, col0, head)

        sweep(step)

    @pl.when(jnp.logical_not(fixed_shift))
    def _():
        m_ref[...] = jnp.full_like(m_ref, NEG_INF)

        def step(j, col0, masked):
            def head(hh, s):
                s = causal(s) if masked else s
                m = m_ref[hh, :, col0:]
                m_new = jnp.maximum(m, jnp.max(s, axis=0, keepdims=True))
                alpha = jnp.exp2(m - m_new)
                p = jnp.exp2(s - m_new)
                m_ref[hh, :, col0:] = m_new
                l_ref[hh, :, col0:] = alpha * l_ref[hh, :, col0:] + jnp.sum(p, axis=0, keepdims=True)
                acc_ref[hh, :, col0:] = alpha * acc_ref[hh, :, col0:] + p_times_v(j, hh, p)

            per_head(j, col0, head)

        sweep(step)


def _attn(q, k, vt, q_norm_g, k_norm_g, later_weights, batch, seq):
    tq, tk = TQ_ATTN, TK_ATTN
    nq = seq // tq
    vt_cols = vt.shape[-1]
    hs = HEADS_PER_STEP
    groups = N_HEADS // hs
    cast_in, cast_out, cast_shape = _cast_sidecar(later_weights, batch * groups * nq,
                                                  lambda b, h, i: (b * groups + h) * nq + i)
    return pl.pallas_call(
        functools.partial(_attn_kernel, tq=tq, tk=tk, vt_cols=vt_cols, q_scale=Q_SCALE),
        grid=(batch, groups, nq),
        in_specs=[
            _const_spec((1, QK_HEAD_DIM)),
            _const_spec((1, QK_HEAD_DIM)),
            pl.BlockSpec((tq, hs * HEAD_SLOT), lambda b, h, i: (b * nq + i, h)),
            pl.BlockSpec((seq, hs * HEAD_SLOT), lambda b, h, i: (b, h)),
            pl.BlockSpec((seq // vt_cols, hs * V_HEAD_DIM, vt_cols), lambda b, h, i: (b, h, 0)),
        ] + cast_in,
        out_specs=[pl.BlockSpec((tq, hs * V_HEAD_DIM), lambda b, h, i: (b * nq + i, h))] + cast_out,
        out_shape=[jax.ShapeDtypeStruct((batch * seq, N_HEADS * V_HEAD_DIM), BF16)] + cast_shape,
        scratch_shapes=[
            pltpu.VMEM((hs, 1, tq), F32),
            pltpu.VMEM((hs, 1, tq), F32),
            pltpu.VMEM((hs, V_HEAD_DIM, tq), F32),
        ],
        compiler_params=_params("arbitrary", "arbitrary", "arbitrary"),
        name="attn",
    )(q_norm_g, k_norm_g, q, k, vt, *later_weights)


def _mix_out_kernel(gates_ref, zb_ref, zc_ref, zv_ref, zc_prev_ref, zv_prev_ref, o_ref, x_ref,
                    b_gate_ref, conv_w_ref, w_conv_out_ref, w_mla_out_ref, w_o_ref, ln2_g_ref, w_up_f32_ref,
                    h_ref, u2_ref, w_up_bf16_ref, *, tm, tiles_per_seq):
    for tile in range(D_FF // TF_FFN):
        for part in range(2):
            src = part * D_FF + tile * TF_FFN
            dst = (2 * tile + part) * TF_FFN
            w_up_bf16_ref[:, dst:dst + TF_FFN] = w_up_f32_ref[:, src:src + TF_FFN].astype(BF16)

    seq_start = (pl.program_id(0) % tiles_per_seq) == 0
    prev = zc_prev_ref[...].astype(F32) * zv_prev_ref[...].astype(F32)
    cv = jnp.concatenate([jnp.where(seq_start, 0.0, prev), zc_ref[...].astype(F32) * zv_ref[...].astype(F32)], axis=0)
    conv = _causal_conv3(cv, conv_w_ref[...], tm)
    conv_in = (zb_ref[...].astype(F32) * conv).astype(BF16)
    y_conv = _dot(conv_in, w_conv_out_ref[...])
    y_mla = _dot(o_ref[...], w_mla_out_ref[...])
    gates = jax.nn.sigmoid(gates_ref[...].astype(F32) + b_gate_ref[...])
    mix = (gates[:, :D_MODEL] * y_conv + gates[:, D_MODEL:] * y_mla).astype(BF16)
    h = x_ref[...] + _dot(mix, w_o_ref[...])
    h_ref[...] = h
    u2_ref[...] = (h * _rms_scale(h, D_MODEL) * ln2_g_ref[...]).astype(BF16)


def _mix_out(z, o, x2, b_gate, conv_w, w_conv_out, w_mla_out, w_o, ln2_g, w_ffn_up, seq):
    t = x2.shape[0]
    tm = TM_MIX
    halo_per_tile = tm // HALO
    cast_in, cast_out, cast_shape = _cast_sidecar((w_ffn_up,), t // tm, lambda i: i)
    cblk = lambda c: (lambda i: (i, c))
    prev = lambda c: (lambda i: (jnp.maximum(i * halo_per_tile - 1, 0), c))
    gate_blocks = 2 * D_MODEL // CONV_DIM
    return pl.pallas_call(
        functools.partial(_mix_out_kernel, tm=tm, tiles_per_seq=seq // tm),
        grid=(t // tm,),
        in_specs=[
            pl.BlockSpec((tm, 2 * D_MODEL), cblk(0)),
            pl.BlockSpec((tm, CONV_DIM), cblk(gate_blocks)),
            pl.BlockSpec((tm, CONV_DIM), cblk(gate_blocks + 1)),
            pl.BlockSpec((tm, CONV_DIM), cblk(gate_blocks + 2)),
            pl.BlockSpec((HALO, CONV_DIM), prev(gate_blocks + 1)),
            pl.BlockSpec((HALO, CONV_DIM), prev(gate_blocks + 2)),
            pl.BlockSpec((tm, D_MODEL), cblk(0)),
            pl.BlockSpec((tm, D_MODEL), cblk(0)),
            _const_spec((1, 2 * D_MODEL)),
            _const_spec((CONV_WIDTH, CONV_DIM)),
            _const_spec((CONV_DIM, D_MODEL)),
            _const_spec((D_MODEL, D_MODEL)),
            _const_spec((D_MODEL, D_MODEL)),
            _const_spec((1, D_MODEL)),
        ] + cast_in,
        out_specs=[pl.BlockSpec((tm, D_MODEL), cblk(0)), pl.BlockSpec((tm, D_MODEL), cblk(0))] + cast_out,
        out_shape=[jax.ShapeDtypeStruct((t, D_MODEL), F32), jax.ShapeDtypeStruct((t, D_MODEL), BF16)] + cast_shape,
        compiler_params=_params("arbitrary"),
        name="mix_out",
    )(z, z, z, z, z, z, o, x2, b_gate, conv_w, w_conv_out, w_mla_out, w_o, ln2_g, w_ffn_up)


def _ffn_kernel(u_ref, u_prev_ref, h_ref, w_up_ref, cw_ref, b_ref, wd_ref, out_ref, lhs_ref, *p_refs,
                tm, tf, row_chunk, tiles_per_seq, n_tiles):
    group = len(p_refs)
    jj = pl.program_id(1)

    @pl.when(jj == 0)
    def _():
        seq_start = (pl.program_id(0) % tiles_per_seq) == 0
        lhs_ref[0:HALO, :] = jnp.where(seq_start, jnp.zeros_like(u_prev_ref), u_prev_ref[...])
        lhs_ref[HALO:, :] = u_ref[...]
        out_ref[...] = h_ref[...]

    def d_ff_tile(c):
        p_ref = p_refs[c]
        cols = slice(c * 2 * tf, (c + 1) * 2 * tf)
        bounds = [0] + [HALO + (r + 1) * row_chunk for r in range(tm // row_chunk)]
        for lo, hi in zip(bounds[:-1], bounds[1:]):
            p_ref[lo:hi, :] = _dot(lhs_ref[lo:hi, :], w_up_ref[:, cols])
        cw = cw_ref[:, cols]
        for r0 in range(0, tm, row_chunk):
            a = (cw[0:1, :] * p_ref[r0 + HALO - 2:r0 + HALO - 2 + row_chunk, :]
                 + cw[1:2, :] * p_ref[r0 + HALO - 1:r0 + HALO - 1 + row_chunk, :]
                 + cw[2:3, :] * p_ref[r0 + HALO:r0 + HALO + row_chunk, :]) + b_ref[:, cols]
            a_gate, a_up = a[:, :tf], a[:, tf:]
            act = (a_gate * jax.nn.sigmoid(a_gate) * a_up).astype(BF16)
            out_ref[r0:r0 + row_chunk, :] += _dot(act, wd_ref[c * tf:(c + 1) * tf, :])

    full_groups, rest = divmod(n_tiles, group)

    @pl.when(jj < full_groups)
    def _():
        for c in range(group):
            d_ff_tile(c)

    if rest:
        @pl.when(jj == full_groups)
        def _():
            for c in range(rest):
                d_ff_tile(c)


def _pair_tiles(v, tf):
    rows = v.shape[0]
    return v.reshape(rows, 2, D_FF // tf, tf).transpose(0, 2, 1, 3).reshape(rows, 2 * D_FF)


def _ffn(u2, h1, w_up_paired, conv_w, conv_b, w_down, seq):
    t = u2.shape[0]
    tm, tf, group = TM_FFN, TF_FFN, FFN_TILE_GROUP
    n_tiles = D_FF // tf
    halo_per_tile = tm // HALO
    return pl.pallas_call(
        functools.partial(_ffn_kernel, tm=tm, tf=tf, row_chunk=FFN_ROW_CHUNK, tiles_per_seq=seq // tm,
                          n_tiles=n_tiles),
        grid=(t // tm, pl.cdiv(n_tiles, group)),
        in_specs=[
            pl.BlockSpec((tm, D_MODEL), lambda i, j: (i, 0)),
            pl.BlockSpec((HALO, D_MODEL), lambda i, j: (jnp.maximum(i * halo_per_tile - 1, 0), 0)),
            pl.BlockSpec((tm, D_MODEL), lambda i, j: (i, 0)),
            pl.BlockSpec((D_MODEL, group * 2 * tf), lambda i, j: (0, j)),
            pl.BlockSpec((CONV_WIDTH, group * 2 * tf), lambda i, j: (0, j)),
            pl.BlockSpec((1, group * 2 * tf), lambda i, j: (0, j)),
            pl.BlockSpec((group * tf, D_MODEL), lambda i, j: (j, 0)),
        ],
        out_specs=pl.BlockSpec((tm, D_MODEL), lambda i, j: (i, 0)),
        out_shape=jax.ShapeDtypeStruct((t, D_MODEL), F32),
        scratch_shapes=[pltpu.VMEM((tm + HALO, D_MODEL), BF16)]
        + [pltpu.VMEM((tm + HALO, 2 * tf), F32) for _ in range(group)],
        compiler_params=_params("arbitrary", "arbitrary"),
        name="ffn",
    )(u2, u2, h1, w_up_paired, _pair_tiles(conv_w, tf), _pair_tiles(conv_b, tf), w_down)


def _row(v):
    return v.reshape(1, -1).astype(F32)


def _layer(h2, pos, inv_freq, batch, seq, ln1_g, w_in, b_gate, conv_w, w_conv_out, q_a_g, w_q_b, kv_a_g, w_kv_b,
           q_norm_g, k_norm_g, w_mla_out, w_o, ln2_g, w_ffn_up, ffn_conv_w, ffn_conv_b, w_ffn_down):
    w_in_t = w_in.T
    w_lat = w_in_t[LAT_OFFSET:LAT_OFFSET + LAT_WIDTH].astype(BF16)
    wq3 = w_q_b.reshape(Q_LORA_RANK, N_HEADS, QK_HEAD_DIM)
    w_qn = wq3[:, :, :QK_NOPE_DIM].reshape(Q_LORA_RANK, -1).astype(BF16)
    w_qr = wq3[:, :, QK_NOPE_DIM:].reshape(Q_LORA_RANK, -1).astype(BF16)
    w_kv = w_kv_b.astype(BF16)
    twice = lambda g: _row(jnp.concatenate([g, g]))

    q, k, vt, u = _mla_prep(
        h2, pos, _row(ln1_g), w_lat, _row(q_a_g), _row(kv_a_g), w_qn, w_qr, w_kv,
        _row(q_norm_g[:QK_NOPE_DIM]), twice(q_norm_g[QK_NOPE_DIM:]),
        _row(k_norm_g[:QK_NOPE_DIM]), twice(k_norm_g[QK_NOPE_DIM:]), inv_freq)
    z = _in_proj(u, w_in_t)
    o, w_conv_out, w_mla_out, w_o, w_down = _attn(q, k, vt, _row(q_norm_g), _row(k_norm_g),
                                                  (w_conv_out, w_mla_out, w_o, w_ffn_down), batch, seq)
    h1, u2, w_up = _mix_out(z, o, h2, _row(b_gate), conv_w.astype(F32), w_conv_out, w_mla_out, w_o, _row(ln2_g),
                            w_ffn_up, seq)
    return _ffn(u2, h1, w_up, ffn_conv_w.astype(F32), _row(ffn_conv_b), w_down, seq)


def kernel(x, positions, ln1_g, w_in, b_gate, conv_w, w_conv_out, q_a_g, w_q_b, kv_a_g, w_kv_b,
           q_norm_g, k_norm_g, w_mla_out, w_o, ln2_g, w_ffn_up, ffn_conv_w, ffn_conv_b, w_ffn_down):
    batch, seq, _ = x.shape
    depth = ln1_g.shape[0]
    h2 = x.reshape(batch * seq, D_MODEL)
    pos = positions.reshape(batch * seq, 1).astype(jnp.int32)
    freq = ROPE_THETA ** (-jnp.arange(0, QK_ROPE_DIM, 2, dtype=F32) / QK_ROPE_DIM)
    inv_freq = jnp.tile(freq, LANES // (QK_ROPE_DIM // 2)).reshape(1, LANES)
    for l in range(depth):
        h2 = _layer(h2, pos, inv_freq, batch, seq, ln1_g[l], w_in[l], b_gate[l], conv_w[l], w_conv_out[l],
                    q_a_g[l], w_q_b[l], kv_a_g[l], w_kv_b[l], q_norm_g[l], k_norm_g[l], w_mla_out[l],
                    w_o[l], ln2_g[l], w_ffn_up[l], ffn_conv_w[l], ffn_conv_b[l], w_ffn_down[l])
    return h2.reshape(batch, seq, D_MODEL)
```

```python
import functools
import math

import jax
import jax.numpy as jnp
from jax import lax
from jax.experimental import pallas as pl
from jax.experimental.pallas import tpu as pltpu

D_MODEL = 2048
CONV_DIM = 1024
CONV_WIDTH = 3
N_HEADS = 16
QK_NOPE_DIM = 128
QK_ROPE_DIM = 64
QK_HEAD_DIM = QK_NOPE_DIM + QK_ROPE_DIM
V_HEAD_DIM = 128
Q_LORA_RANK = 768
KV_LORA_RANK = 512
ROPE_THETA = 10000.0
D_FF = 5632
NORM_EPS = 1e-6
NEG_INF = -1e30

Q_SCALE = QK_HEAD_DIM ** -0.5 * math.log2(math.e)
FIXED_SHIFT_LIMIT = 48.0

LANES = 128
HEAD_SLOT = 2 * LANES
HALO = 16
LAT_OFFSET = 3 * CONV_DIM
LAT_WIDTH = Q_LORA_RANK + KV_LORA_RANK + LANES
GATE_OFFSET = LAT_OFFSET + Q_LORA_RANK + KV_LORA_RANK + QK_ROPE_DIM
Z_WIDTH = 2 * D_MODEL + 3 * CONV_DIM
VMEM_LIMIT = 56 * 1024 * 1024

TM_PREP = 256
TM_INPROJ = 2048
TN_INPROJ = 1024
TQ_ATTN = 1024
TK_ATTN = 512
HEADS_PER_STEP = 4
SCORES_AHEAD = 4
TM_MIX = 256
TM_FFN = 512
TF_FFN = 512
FFN_ROW_CHUNK = 256
FFN_TILE_GROUP = 2

F32 = jnp.float32
BF16 = jnp.bfloat16


def _dot(a, b):
    return jnp.dot(a, b, preferred_element_type=F32)


def _dot_nt(a, b):
    return lax.dot_general(a, b, (((1,), (1,)), ((), ())), preferred_element_type=F32)


def _rms_scale(x, width):
    return lax.rsqrt(jnp.sum(x * x, axis=-1, keepdims=True) * (1.0 / width) + NORM_EPS)


def _params(*sem):
    return pltpu.CompilerParams(dimension_semantics=sem, vmem_limit_bytes=VMEM_LIMIT)


def _const_spec(shape):
    return pl.BlockSpec(shape, lambda *_: (0,) * len(shape), pipeline_mode=pl.Buffered(1))


def _cast_sidecar(weights, n_steps, step_index):
    in_specs, out_specs, out_shape = [], [], []
    for w in weights:
        rows, cols = w.shape
        slab = rows // n_steps
        assert slab * n_steps == rows and slab % 16 == 0, (w.shape, n_steps)
        spec = pl.BlockSpec((slab, cols), lambda *ids: (step_index(*ids), 0))
        in_specs.append(spec)
        out_specs.append(spec)
        out_shape.append(jax.ShapeDtypeStruct(w.shape, BF16))
    return in_specs, out_specs, out_shape


def _cast_slabs(src_refs, dst_refs):
    for src, dst in zip(src_refs, dst_refs):
        dst[...] = src[...].astype(BF16)


def _causal_conv3(p, cw, rows):
    full = cw[2:3, :] * p + cw[1:2, :] * pltpu.roll(p, 1, axis=0) + cw[0:1, :] * pltpu.roll(p, 2, axis=0)
    return full[p.shape[0] - rows:, :]


def _mla_prep_kernel(x_ref, pos_ref, ln_g_ref, w_lat_ref, qa_g_ref, kva_g_ref, w_qn_ref, w_qr_ref,
                     w_kv_ref, qg_nope_ref, qg_rope_ref, kg_nope_ref, kg_rope_ref, inv_freq_ref,
                     q_ref, k_ref, vt_ref, u_ref, qn_s, qr_s, kn_s, v_s, kr_s, *, q_scale):
    @pl.when(pl.program_id(0) == 0)
    def _():
        for s in (qn_s, qr_s, kn_s, v_s, kr_s):
            s[...] = jnp.zeros_like(s)

    ang = pos_ref[...].astype(F32) * inv_freq_ref[...]
    cos = jnp.cos(ang)
    sin = jnp.sin(ang)
    lane = lax.broadcasted_iota(jnp.int32, (1, LANES), 1)
    first_half = (lane % QK_ROPE_DIM) < (QK_ROPE_DIM // 2)
    sin_signed = jnp.where(first_half, -sin, sin)

    def rope(t):
        rot = jnp.where(first_half, pltpu.roll(t, LANES - QK_ROPE_DIM // 2, axis=1),
                        pltpu.roll(t, QK_ROPE_DIM // 2, axis=1))
        return t * cos + rot * sin_signed

    k_rope = kr_s[...]
    k_rope = jnp.where(lane < QK_ROPE_DIM, k_rope, pltpu.roll(k_rope, QK_ROPE_DIM, axis=1))
    qg_nope = qg_nope_ref[...] * q_scale
    qg_rope = qg_rope_ref[...] * q_scale
    kg_nope = kg_nope_ref[...]
    k_tail = rope(k_rope * kg_rope_ref[...])
    k_rope_ss = 0.5 * jnp.sum(k_rope * k_rope, axis=-1, keepdims=True)

    for pair in range(N_HEADS // 2):
        q_pair = qr_s[:, pair * LANES:(pair + 1) * LANES]
        q_pair_roped = rope(q_pair * qg_rope)
        for sub in range(2):
            h = 2 * pair + sub
            mine = (lane // QK_ROPE_DIM) == sub
            q_nope = qn_s[:, h * QK_NOPE_DIM:(h + 1) * QK_NOPE_DIM]
            ss = jnp.sum(q_nope * q_nope + jnp.where(mine, q_pair * q_pair, 0.0), axis=-1, keepdims=True)
            r = lax.rsqrt(ss * (1.0 / QK_HEAD_DIM) + NORM_EPS)
            q_ref[:, h * HEAD_SLOT: h * HEAD_SLOT + LANES] = (q_nope * r * qg_nope).astype(BF16)
            q_ref[:, h * HEAD_SLOT + LANES:(h + 1) * HEAD_SLOT] = jnp.where(
                mine, q_pair_roped * r, 0.0).astype(BF16)

    for h in range(N_HEADS):
        k_nope = kn_s[:, h * QK_NOPE_DIM:(h + 1) * QK_NOPE_DIM]
        kss = jnp.sum(k_nope * k_nope, axis=-1, keepdims=True) + k_rope_ss
        kr = lax.rsqrt(kss * (1.0 / QK_HEAD_DIM) + NORM_EPS)
        k_ref[:, h * HEAD_SLOT: h * HEAD_SLOT + LANES] = (k_nope * kr * kg_nope).astype(BF16)
        k_ref[:, h * HEAD_SLOT + LANES:(h + 1) * HEAD_SLOT] = (k_tail * kr).astype(BF16)

    vt_ref[0] = v_s[...].T

    xf = x_ref[...]
    u = (xf * _rms_scale(xf, D_MODEL) * ln_g_ref[...]).astype(BF16)
    u_ref[...] = u
    q_lat = _dot_nt(u, w_lat_ref[:Q_LORA_RANK, :])
    kv_lat = _dot_nt(u, w_lat_ref[Q_LORA_RANK:Q_LORA_RANK + KV_LORA_RANK + LANES, :])
    kr_s[...] = kv_lat[:, KV_LORA_RANK:]
    kv_lat = kv_lat[:, :KV_LORA_RANK]
    qn = (q_lat * _rms_scale(q_lat, Q_LORA_RANK) * qa_g_ref[...]).astype(BF16)
    kvn = (kv_lat * _rms_scale(kv_lat, KV_LORA_RANK) * kva_g_ref[...]).astype(BF16)
    qn_s[...] = _dot(qn, w_qn_ref[...])
    qr_s[...] = _dot(qn, w_qr_ref[...])
    kv = _dot(kvn, w_kv_ref[...])
    for h in range(N_HEADS):
        k0 = h * (QK_NOPE_DIM + V_HEAD_DIM)
        v0 = k0 + QK_NOPE_DIM
        kn_s[:, h * QK_NOPE_DIM:(h + 1) * QK_NOPE_DIM] = kv[:, k0:v0]
        v_s[:, h * V_HEAD_DIM:(h + 1) * V_HEAD_DIM] = kv[:, v0:v0 + V_HEAD_DIM].astype(BF16)


def _mla_prep(x2, pos, ln_g, w_lat, qa_g, kva_g, w_qn, w_qr, w_kv, qg_nope, qg_rope, kg_nope, kg_rope, inv_freq):
    t = x2.shape[0]
    tm = TM_PREP
    n = t // tm
    row = lambda i: (jnp.minimum(i, n - 1), 0)
    prev = lambda i: (jnp.maximum(i - 1, 0), 0)
    return pl.pallas_call(
        functools.partial(_mla_prep_kernel, q_scale=Q_SCALE),
        grid=(n + 1,),
        in_specs=[
            pl.BlockSpec((tm, D_MODEL), row),
            pl.BlockSpec((tm, 1), prev),
            _const_spec((1, D_MODEL)),
            _const_spec((LAT_WIDTH, D_MODEL)),
            _const_spec((1, Q_LORA_RANK)),
            _const_spec((1, KV_LORA_RANK)),
            _const_spec(w_qn.shape), _const_spec(w_qr.shape), _const_spec(w_kv.shape),
            _const_spec((1, LANES)), _const_spec((1, LANES)),
            _const_spec((1, LANES)), _const_spec((1, LANES)),
            _const_spec((1, LANES)),
        ],
        out_specs=[
            pl.BlockSpec((tm, N_HEADS * HEAD_SLOT), prev),
            pl.BlockSpec((tm, N_HEADS * HEAD_SLOT), prev),
            pl.BlockSpec((1, N_HEADS * V_HEAD_DIM, tm), lambda i: (jnp.maximum(i - 1, 0), 0, 0)),
            pl.BlockSpec((tm, D_MODEL), row),
        ],
        out_shape=[
            jax.ShapeDtypeStruct((t, N_HEADS * HEAD_SLOT), BF16),
            jax.ShapeDtypeStruct((t, N_HEADS * HEAD_SLOT), BF16),
            jax.ShapeDtypeStruct((n, N_HEADS * V_HEAD_DIM, tm), BF16),
            jax.ShapeDtypeStruct((t, D_MODEL), BF16),
        ],
        scratch_shapes=[
            pltpu.VMEM((tm, N_HEADS * QK_NOPE_DIM), F32),
            pltpu.VMEM((tm, N_HEADS * QK_ROPE_DIM), F32),
            pltpu.VMEM((tm, N_HEADS * QK_NOPE_DIM), F32),
            pltpu.VMEM((tm, N_HEADS * V_HEAD_DIM), BF16),
            pltpu.VMEM((tm, LANES), F32),
        ],
        compiler_params=_params("arbitrary"),
        name="mla_prep",
    )(x2, pos, ln_g, w_lat, qa_g, kva_g, w_qn, w_qr, w_kv, qg_nope, qg_rope, kg_nope, kg_rope, inv_freq)


def _in_proj_kernel(u_ref, w_ref, z_ref):
    z_ref[...] = _dot_nt(u_ref[...], w_ref[...].astype(BF16)).astype(BF16)


def _in_proj(u, w_in_t):
    t = u.shape[0]
    tm, tn = TM_INPROJ, TN_INPROJ
    gate_tiles = 2 * D_MODEL // tn

    def w_rows(i, j):
        rows = jnp.where(j < gate_tiles, GATE_OFFSET + j * tn, (j - gate_tiles) * tn)
        return pl.multiple_of(rows, math.gcd(GATE_OFFSET, tn)), 0

    return pl.pallas_call(
        _in_proj_kernel,
        grid=(t // tm, Z_WIDTH // tn),
        in_specs=[
            pl.BlockSpec((tm, D_MODEL), lambda i, j: (i, 0)),
            pl.BlockSpec((pl.Element(tn), pl.Element(D_MODEL)), w_rows),
        ],
        out_specs=pl.BlockSpec((tm, tn), lambda i, j: (i, j)),
        out_shape=jax.ShapeDtypeStruct((t, Z_WIDTH), BF16),
        compiler_params=_params("arbitrary", "arbitrary"),
        name="in_proj",
    )(u, w_in_t)


def _attn_kernel(qg_ref, kg_ref, q_ref, k_ref, vt_ref, wc_f32_ref, wm_f32_ref, wo_f32_ref, wd_f32_ref,
                 o_ref, wc_bf16_ref, wm_bf16_ref, wo_bf16_ref, wd_bf16_ref, m_ref, l_ref, acc_ref,
                 *, tq, tk, vt_cols, q_scale):
    i = pl.program_id(2)
    sub_blocks = tk // vt_cols
    kv_per_q = tq // tk
    gq = jnp.max(jnp.abs(qg_ref[...]), axis=-1, keepdims=True)
    gk = jnp.max(jnp.abs(kg_ref[...]), axis=-1, keepdims=True)
    bound = (gq * gk)[0, 0] * (q_scale * QK_HEAD_DIM)
    fixed_shift = bound <= FIXED_SHIFT_LIMIT
    _cast_slabs((wc_f32_ref, wm_f32_ref, wo_f32_ref, wd_f32_ref),
                (wc_bf16_ref, wm_bf16_ref, wo_bf16_ref, wd_bf16_ref))

    def score(j, hh, col0):
        start = pl.multiple_of(j * tk, tk)
        q = q_ref[col0:, hh * HEAD_SLOT:(hh + 1) * HEAD_SLOT]
        kb = k_ref[pl.ds(start, tk), hh * HEAD_SLOT:(hh + 1) * HEAD_SLOT]
        return _dot_nt(kb, q)

    def per_head(j, col0, fn):
        sc = {hh: score(j, hh, col0) for hh in range(SCORES_AHEAD)}
        for hh in range(HEADS_PER_STEP):
            fn(hh, sc.pop(hh))
            if hh + SCORES_AHEAD < HEADS_PER_STEP:
                sc[hh + SCORES_AHEAD] = score(j, hh + SCORES_AHEAD, col0)

    def causal(s):
        kv_idx = lax.broadcasted_iota(jnp.int32, s.shape, 0)
        q_idx = lax.broadcasted_iota(jnp.int32, s.shape, 1)
        return jnp.where(kv_idx <= q_idx, s, NEG_INF)

    def p_times_v(j, hh, p):
        pb = p.astype(BF16)
        pv = None
        for c in range(sub_blocks):
            vt = vt_ref[j * sub_blocks + c, hh * V_HEAD_DIM:(hh + 1) * V_HEAD_DIM, :]
            part = _dot(vt, pb[c * vt_cols:(c + 1) * vt_cols, :])
            pv = part if pv is None else pv + part
        return pv

    def sweep(step):
        def body(g, c):
            for d in range(kv_per_q):
                step(g * kv_per_q + d, 0, False)
            return c

        lax.fori_loop(0, i, body, 0)
        for d in range(kv_per_q):
            step(i * kv_per_q + d, d * tk, True)
        for hh in range(HEADS_PER_STEP):
            o_ref[:, hh * V_HEAD_DIM:(hh + 1) * V_HEAD_DIM] = (acc_ref[hh] / l_ref[hh]).T.astype(BF16)

    l_ref[...] = jnp.zeros_like(l_ref)
    acc_ref[...] = jnp.zeros_like(acc_ref)

    @pl.when(fixed_shift)
    def _():
        def step(j, col0, masked):
            def head(hh, s):
                p = jnp.exp2((causal(s) if masked else s) - bound)
                l_ref[hh, :, col0:] += jnp.sum(p, axis=0, keepdims=True)
                acc_ref[hh, :, col0:] += p_times_v(j, hh, p)

            per_head(j, col0, head)

        sweep(step)

    @pl.when(jnp.logical_not(fixed_shift))
    def _():
        m_ref[...] = jnp.full_like(m_ref, NEG_INF)

        def step(j, col0, masked):
            def head(hh, s):
                s = causal(s) if masked else s
                m = m_ref[hh, :, col0:]
                m_new = jnp.maximum(m, jnp.max(s, axis=0, keepdims=True))
                alpha = jnp.exp2(m - m_new)
                p = jnp.exp2(s - m_new)
                m_ref[hh, :, col0:] = m_new
                l_ref[hh, :, col0:] = alpha * l_ref[hh, :, col0:] + jnp.sum(p, axis=0, keepdims=True)
                acc_ref[hh, :, col0:] = alpha * acc_ref[hh, :, col0:] + p_times_v(j, hh, p)

            per_head(j, col0, head)

        sweep(step)


def _attn(q, k, vt, q_norm_g, k_norm_g, later_weights, batch, seq):
    tq, tk = TQ_ATTN, TK_ATTN
    nq = seq // tq
    vt_cols = vt.shape[-1]
    hs = HEADS_PER_STEP
    groups = N_HEADS // hs
    cast_in, cast_out, cast_shape = _cast_sidecar(later_weights, batch * groups * nq,
                                                  lambda b, h, i: (b * groups + h) * nq + i)
    return pl.pallas_call(
        functools.partial(_attn_kernel, tq=tq, tk=tk, vt_cols=vt_cols, q_scale=Q_SCALE),
        grid=(batch, groups, nq),
        in_specs=[
            _const_spec((1, QK_HEAD_DIM)),
            _const_spec((1, QK_HEAD_DIM)),
            pl.BlockSpec((tq, hs * HEAD_SLOT), lambda b, h, i: (b * nq + i, h)),
            pl.BlockSpec((seq, hs * HEAD_SLOT), lambda b, h, i: (b, h)),
            pl.BlockSpec((seq // vt_cols, hs * V_HEAD_DIM, vt_cols), lambda b, h, i: (b, h, 0)),
        ] + cast_in,
        out_specs=[pl.BlockSpec((tq, hs * V_HEAD_DIM), lambda b, h, i: (b * nq + i, h))] + cast_out,
        out_shape=[jax.ShapeDtypeStruct((batch * seq, N_HEADS * V_HEAD_DIM), BF16)] + cast_shape,
        scratch_shapes=[
            pltpu.VMEM((hs, 1, tq), F32),
            pltpu.VMEM((hs, 1, tq), F32),
            pltpu.VMEM((hs, V_HEAD_DIM, tq), F32),
        ],
        compiler_params=_params("arbitrary", "arbitrary", "arbitrary"),
        name="attn",
    )(q_norm_g, k_norm_g, q, k, vt, *later_weights)


def _mix_out_kernel(gates_ref, zb_ref, zc_ref, zv_ref, zc_prev_ref, zv_prev_ref, o_ref, x_ref,
                    b_gate_ref, conv_w_ref, w_conv_out_ref, w_mla_out_ref, w_o_ref, ln2_g_ref, w_up_f32_ref,
                    h_ref, u2_ref, w_up_bf16_ref, *, tm, tiles_per_seq):
    for tile in range(D_FF // TF_FFN):
        for part in range(2):
            src = part * D_FF + tile * TF_FFN
            dst = (2 * tile + part) * TF_FFN
            w_up_bf16_ref[:, dst:dst + TF_FFN] = w_up_f32_ref[:, src:src + TF_FFN].astype(BF16)

    seq_start = (pl.program_id(0) % tiles_per_seq) == 0
    prev = zc_prev_ref[...].astype(F32) * zv_prev_ref[...].astype(F32)
    cv = jnp.concatenate([jnp.where(seq_start, 0.0, prev), zc_ref[...].astype(F32) * zv_ref[...].astype(F32)], axis=0)
    conv = _causal_conv3(cv, conv_w_ref[...], tm)
    conv_in = (zb_ref[...].astype(F32) * conv).astype(BF16)
    y_conv = _dot(conv_in, w_conv_out_ref[...])
    y_mla = _dot(o_ref[...], w_mla_out_ref[...])
    gates = jax.nn.sigmoid(gates_ref[...].astype(F32) + b_gate_ref[...])
    mix = (gates[:, :D_MODEL] * y_conv + gates[:, D_MODEL:] * y_mla).astype(BF16)
    h = x_ref[...] + _dot(mix, w_o_ref[...])
    h_ref[...] = h
    u2_ref[...] = (h * _rms_scale(h, D_MODEL) * ln2_g_ref[...]).astype(BF16)


def _mix_out(z, o, x2, b_gate, conv_w, w_conv_out, w_mla_out, w_o, ln2_g, w_ffn_up, seq):
    t = x2.shape[0]
    tm = TM_MIX
    halo_per_tile = tm // HALO
    cast_in, cast_out, cast_shape = _cast_sidecar((w_ffn_up,), t // tm, lambda i: i)
    cblk = lambda c: (lambda i: (i, c))
    prev = lambda c: (lambda i: (jnp.maximum(i * halo_per_tile - 1, 0), c))
    gate_blocks = 2 * D_MODEL // CONV_DIM
    return pl.pallas_call(
        functools.partial(_mix_out_kernel, tm=tm, tiles_per_seq=seq // tm),
        grid=(t // tm,),
        in_specs=[
            pl.BlockSpec((tm, 2 * D_MODEL), cblk(0)),
            pl.BlockSpec((tm, CONV_DIM), cblk(gate_blocks)),
            pl.BlockSpec((tm, CONV_DIM), cblk(gate_blocks + 1)),
            pl.BlockSpec((tm, CONV_DIM), cblk(gate_blocks + 2)),
            pl.BlockSpec((HALO, CONV_DIM), prev(gate_blocks + 1)),
            pl.BlockSpec((HALO, CONV_DIM), prev(gate_blocks + 2)),
            pl.BlockSpec((tm, D_MODEL), cblk(0)),
            pl.BlockSpec((tm, D_MODEL), cblk(0)),
            _const_spec((1, 2 * D_MODEL)),
            _const_spec((CONV_WIDTH, CONV_DIM)),
            _const_spec((CONV_DIM, D_MODEL)),
            _const_spec((D_MODEL, D_MODEL)),
            _const_spec((D_MODEL, D_MODEL)),
            _const_spec((1, D_MODEL)),
        ] + cast_in,
        out_specs=[pl.BlockSpec((tm, D_MODEL), cblk(0)), pl.BlockSpec((tm, D_MODEL), cblk(0))] + cast_out,
        out_shape=[jax.ShapeDtypeStruct((t, D_MODEL), F32), jax.ShapeDtypeStruct((t, D_MODEL), BF16)] + cast_shape,
        compiler_params=_params("arbitrary"),
        name="mix_out",
    )(z, z, z, z, z, z, o, x2, b_gate, conv_w, w_conv_out, w_mla_out, w_o, ln2_g, w_ffn_up)


def _ffn_kernel(u_ref, u_prev_ref, h_ref, w_up_ref, cw_ref, b_ref, wd_ref, out_ref, lhs_ref, *p_refs,
                tm, tf, row_chunk, tiles_per_seq, n_tiles):
    group = len(p_refs)
    jj = pl.program_id(1)

    @pl.when(jj == 0)
    def _():
        seq_start = (pl.program_id(0) % tiles_per_seq) == 0
        lhs_ref[0:HALO, :] = jnp.where(seq_start, jnp.zeros_like(u_prev_ref), u_prev_ref[...])
        lhs_ref[HALO:, :] = u_ref[...]
        out_ref[...] = h_ref[...]

    def d_ff_tile(c):
        p_ref = p_refs[c]
        cols = slice(c * 2 * tf, (c + 1) * 2 * tf)
        bounds = [0] + [HALO + (r + 1) * row_chunk for r in range(tm // row_chunk)]
        for lo, hi in zip(bounds[:-1], bounds[1:]):
            p_ref[lo:hi, :] = _dot(lhs_ref[lo:hi, :], w_up_ref[:, cols])
        cw = cw_ref[:, cols]
        for r0 in range(0, tm, row_chunk):
            a = (cw[0:1, :] * p_ref[r0 + HALO - 2:r0 + HALO - 2 + row_chunk, :]
                 + cw[1:2, :] * p_ref[r0 + HALO - 1:r0 + HALO - 1 + row_chunk, :]
                 + cw[2:3, :] * p_ref[r0 + HALO:r0 + HALO + row_chunk, :]) + b_ref[:, cols]
            a_gate, a_up = a[:, :tf], a[:, tf:]
            act = (a_gate * jax.nn.sigmoid(a_gate) * a_up).astype(BF16)
            out_ref[r0:r0 + row_chunk, :] += _dot(act, wd_ref[c * tf:(c + 1) * tf, :])

    rest = n_tiles % group

    @pl.when(jj >= (1 if rest else 0))
    def _():
        for c in range(group):
            d_ff_tile(c)

    if rest:
        @pl.when(jj == 0)
        def _():
            for c in range(rest):
                d_ff_tile(c)


def _pair_tiles(v, tf):
    rows = v.shape[0]
    return v.reshape(rows, 2, D_FF // tf, tf).transpose(0, 2, 1, 3).reshape(rows, 2 * D_FF)


def _ffn(u2, h1, w_up_paired, conv_w, conv_b, w_down, seq):
    t = u2.shape[0]
    tm, tf, group = TM_FFN, TF_FFN, FFN_TILE_GROUP
    n_tiles = D_FF // tf
    n_groups = pl.cdiv(n_tiles, group)
    halo_per_tile = tm // HALO
    blk = (lambda j: (j + n_groups - 1) % n_groups) if n_tiles % group else (lambda j: j)
    return pl.pallas_call(
        functools.partial(_ffn_kernel, tm=tm, tf=tf, row_chunk=FFN_ROW_CHUNK, tiles_per_seq=seq // tm,
                          n_tiles=n_tiles),
        grid=(t // tm, n_groups),
        in_specs=[
            pl.BlockSpec((tm, D_MODEL), lambda i, j: (i, 0)),
            pl.BlockSpec((HALO, D_MODEL), lambda i, j: (jnp.maximum(i * halo_per_tile - 1, 0), 0)),
            pl.BlockSpec((tm, D_MODEL), lambda i, j: (i, 0)),
            pl.BlockSpec((D_MODEL, group * 2 * tf), lambda i, j: (0, blk(j))),
            pl.BlockSpec((CONV_WIDTH, group * 2 * tf), lambda i, j: (0, blk(j))),
            pl.BlockSpec((1, group * 2 * tf), lambda i, j: (0, blk(j))),
            pl.BlockSpec((group * tf, D_MODEL), lambda i, j: (blk(j), 0)),
        ],
        out_specs=pl.BlockSpec((tm, D_MODEL), lambda i, j: (i, 0)),
        out_shape=jax.ShapeDtypeStruct((t, D_MODEL), F32),
        scratch_shapes=[pltpu.VMEM((tm + HALO, D_MODEL), BF16)]
        + [pltpu.VMEM((tm + HALO, 2 * tf), F32) for _ in range(group)],
        compiler_params=_params("arbitrary", "arbitrary"),
        name="ffn",
    )(u2, u2, h1, w_up_paired, _pair_tiles(conv_w, tf), _pair_tiles(conv_b, tf), w_down)


def _row(v):
    return v.reshape(1, -1).astype(F32)


def _layer(h2, pos, inv_freq, batch, seq, ln1_g, w_in, b_gate, conv_w, w_conv_out, q_a_g, w_q_b, kv_a_g, w_kv_b,
           q_norm_g, k_norm_g, w_mla_out, w_o, ln2_g, w_ffn_up, ffn_conv_w, ffn_conv_b, w_ffn_down):
    w_in_t = w_in.T
    w_lat = w_in_t[LAT_OFFSET:LAT_OFFSET + LAT_WIDTH].astype(BF16)
    wq3 = w_q_b.reshape(Q_LORA_RANK, N_HEADS, QK_HEAD_DIM)
    w_qn = wq3[:, :, :QK_NOPE_DIM].reshape(Q_LORA_RANK, -1).astype(BF16)
    w_qr = wq3[:, :, QK_NOPE_DIM:].reshape(Q_LORA_RANK, -1).astype(BF16)
    w_kv = w_kv_b.astype(BF16)
    twice = lambda g: _row(jnp.concatenate([g, g]))

    q, k, vt, u = _mla_prep(
        h2, pos, _row(ln1_g), w_lat, _row(q_a_g), _row(kv_a_g), w_qn, w_qr, w_kv,
        _row(q_norm_g[:QK_NOPE_DIM]), twice(q_norm_g[QK_NOPE_DIM:]),
        _row(k_norm_g[:QK_NOPE_DIM]), twice(k_norm_g[QK_NOPE_DIM:]), inv_freq)
    z = _in_proj(u, w_in_t)
    o, w_conv_out, w_mla_out, w_o, w_down = _attn(q, k, vt, _row(q_norm_g), _row(k_norm_g),
                                                  (w_conv_out, w_mla_out, w_o, w_ffn_down), batch, seq)
    h1, u2, w_up = _mix_out(z, o, h2, _row(b_gate), conv_w.astype(F32), w_conv_out, w_mla_out, w_o, _row(ln2_g),
                            w_ffn_up, seq)
    return _ffn(u2, h1, w_up, ffn_conv_w.astype(F32), _row(ffn_conv_b), w_down, seq)


def kernel(x, positions, ln1_g, w_in, b_gate, conv_w, w_conv_out, q_a_g, w_q_b, kv_a_g, w_kv_b,
           q_norm_g, k_norm_g, w_mla_out, w_o, ln2_g, w_ffn_up, ffn_conv_w, ffn_conv_b, w_ffn_down):
    batch, seq, _ = x.shape
    depth = ln1_g.shape[0]
    h2 = x.reshape(batch * seq, D_MODEL)
    pos = positions.reshape(batch * seq, 1).astype(jnp.int32)
    freq = ROPE_THETA ** (-jnp.arange(0, QK_ROPE_DIM, 2, dtype=F32) / QK_ROPE_DIM)
    inv_freq = jnp.tile(freq, LANES // (QK_ROPE_DIM // 2)).reshape(1, LANES)
    for l in range(depth):
        h2 = _layer(h2, pos, inv_freq, batch, seq, ln1_g[l], w_in[l], b_gate[l], conv_w[l], w_conv_out[l],
                    q_a_g[l], w_q_b[l], kv_a_g[l], w_kv_b[l], q_norm_g[l], k_norm_g[l], w_mla_out[l],
                    w_o[l], ln2_g[l], w_ffn_up[l], ffn_conv_w[l], ffn_conv_b[l], w_ffn_down[l])
    return h2.reshape(batch, seq, D_MODEL)
```

```python
import functools
import math

import jax
import jax.numpy as jnp
from jax import lax
from jax.experimental import pallas as pl
from jax.experimental.pallas import tpu as pltpu

D_MODEL = 2048
CONV_DIM = 1024
CONV_WIDTH = 3
N_HEADS = 16
QK_NOPE_DIM = 128
QK_ROPE_DIM = 64
QK_HEAD_DIM = QK_NOPE_DIM + QK_ROPE_DIM
V_HEAD_DIM = 128
Q_LORA_RANK = 768
KV_LORA_RANK = 512
ROPE_THETA = 10000.0
D_FF = 5632
NORM_EPS = 1e-6
NEG_INF = -1e30

Q_SCALE = QK_HEAD_DIM ** -0.5 * math.log2(math.e)
FIXED_SHIFT_LIMIT = 48.0

LANES = 128
HEAD_SLOT = 2 * LANES
HALO = 16
LAT_OFFSET = 3 * CONV_DIM
LAT_WIDTH = Q_LORA_RANK + KV_LORA_RANK + LANES
GATE_OFFSET = LAT_OFFSET + Q_LORA_RANK + KV_LORA_RANK + QK_ROPE_DIM
Z_WIDTH = 2 * D_MODEL + 3 * CONV_DIM
VMEM_LIMIT = 56 * 1024 * 1024

TM_PREP = 256
TM_INPROJ = 2048
TN_INPROJ = 1024
TQ_ATTN = 1024
TK_ATTN = 512
HEADS_PER_STEP = 4
SCORES_AHEAD = 4
TM_MIX = 256
TM_FFN = 512
TF_FFN = 512
FFN_ROW_CHUNK = 256
FFN_TILE_GROUP = 2

F32 = jnp.float32
BF16 = jnp.bfloat16


def _dot(a, b):
    return jnp.dot(a, b, preferred_element_type=F32)


def _dot_nt(a, b):
    return lax.dot_general(a, b, (((1,), (1,)), ((), ())), preferred_element_type=F32)


def _rms_scale(x, width):
    return lax.rsqrt(jnp.sum(x * x, axis=-1, keepdims=True) * (1.0 / width) + NORM_EPS)


def _params(*sem):
    return pltpu.CompilerParams(dimension_semantics=sem, vmem_limit_bytes=VMEM_LIMIT)


def _const_spec(shape):
    return pl.BlockSpec(shape, lambda *_: (0,) * len(shape), pipeline_mode=pl.Buffered(1))


def _cast_sidecar(weights, n_steps, step_index):
    in_specs, out_specs, out_shape = [], [], []
    for w in weights:
        rows, cols = w.shape
        slab = rows // n_steps
        assert slab * n_steps == rows and slab % 16 == 0, (w.shape, n_steps)
        spec = pl.BlockSpec((slab, cols), lambda *ids: (step_index(*ids), 0))
        in_specs.append(spec)
        out_specs.append(spec)
        out_shape.append(jax.ShapeDtypeStruct(w.shape, BF16))
    return in_specs, out_specs, out_shape


def _cast_slabs(src_refs, dst_refs):
    for src, dst in zip(src_refs, dst_refs):
        dst[...] = src[...].astype(BF16)


def _causal_conv3(p, cw, rows):
    full = cw[2:3, :] * p + cw[1:2, :] * pltpu.roll(p, 1, axis=0) + cw[0:1, :] * pltpu.roll(p, 2, axis=0)
    return full[p.shape[0] - rows:, :]


def _mla_prep_kernel(x_ref, pos_ref, ln_g_ref, w_lat_ref, qa_g_ref, kva_g_ref, w_qn_ref, w_qr_ref,
                     w_kv_ref, qg_nope_ref, qg_rope_ref, kg_nope_ref, kg_rope_ref, inv_freq_ref,
                     q_ref, k_ref, vt_ref, u_ref, qn_s, qr_s, kn_s, v_s, kr_s, *, q_scale):
    @pl.when(pl.program_id(0) == 0)
    def _():
        for s in (qn_s, qr_s, kn_s, v_s, kr_s):
            s[...] = jnp.zeros_like(s)

    ang = pos_ref[...].astype(F32) * inv_freq_ref[...]
    cos = jnp.cos(ang)
    sin = jnp.sin(ang)
    lane = lax.broadcasted_iota(jnp.int32, (1, LANES), 1)
    first_half = (lane % QK_ROPE_DIM) < (QK_ROPE_DIM // 2)
    sin_signed = jnp.where(first_half, -sin, sin)

    def rope(t):
        rot = jnp.where(first_half, pltpu.roll(t, LANES - QK_ROPE_DIM // 2, axis=1),
                        pltpu.roll(t, QK_ROPE_DIM // 2, axis=1))
        return t * cos + rot * sin_signed

    k_rope = kr_s[...]
    k_rope = jnp.where(lane < QK_ROPE_DIM, k_rope, pltpu.roll(k_rope, QK_ROPE_DIM, axis=1))
    qg_nope = qg_nope_ref[...] * q_scale
    qg_rope = qg_rope_ref[...] * q_scale
    kg_nope = kg_nope_ref[...]
    k_tail = rope(k_rope * kg_rope_ref[...])
    k_rope_ss = 0.5 * jnp.sum(k_rope * k_rope, axis=-1, keepdims=True)

    for pair in range(N_HEADS // 2):
        q_pair = qr_s[:, pair * LANES:(pair + 1) * LANES]
        q_pair_roped = rope(q_pair * qg_rope)
        for sub in range(2):
            h = 2 * pair + sub
            mine = (lane // QK_ROPE_DIM) == sub
            q_nope = qn_s[:, h * QK_NOPE_DIM:(h + 1) * QK_NOPE_DIM]
            ss = jnp.sum(q_nope * q_nope + jnp.where(mine, q_pair * q_pair, 0.0), axis=-1, keepdims=True)
            r = lax.rsqrt(ss * (1.0 / QK_HEAD_DIM) + NORM_EPS)
            q_ref[:, h * HEAD_SLOT: h * HEAD_SLOT + LANES] = (q_nope * r * qg_nope).astype(BF16)
            q_ref[:, h * HEAD_SLOT + LANES:(h + 1) * HEAD_SLOT] = jnp.where(
                mine, q_pair_roped * r, 0.0).astype(BF16)

    for h in range(N_HEADS):
        k_nope = kn_s[:, h * QK_NOPE_DIM:(h + 1) * QK_NOPE_DIM]
        kss = jnp.sum(k_nope * k_nope, axis=-1, keepdims=True) + k_rope_ss
        kr = lax.rsqrt(kss * (1.0 / QK_HEAD_DIM) + NORM_EPS)
        k_ref[:, h * HEAD_SLOT: h * HEAD_SLOT + LANES] = (k_nope * kr * kg_nope).astype(BF16)
        k_ref[:, h * HEAD_SLOT + LANES:(h + 1) * HEAD_SLOT] = (k_tail * kr).astype(BF16)

    vt_ref[0] = v_s[...].T

    xf = x_ref[...]
    u = (xf * _rms_scale(xf, D_MODEL) * ln_g_ref[...]).astype(BF16)
    u_ref[...] = u
    q_lat = _dot_nt(u, w_lat_ref[:Q_LORA_RANK, :])
    kv_lat = _dot_nt(u, w_lat_ref[Q_LORA_RANK:Q_LORA_RANK + KV_LORA_RANK + LANES, :])
    kr_s[...] = kv_lat[:, KV_LORA_RANK:]
    kv_lat = kv_lat[:, :KV_LORA_RANK]
    qn = (q_lat * _rms_scale(q_lat, Q_LORA_RANK) * qa_g_ref[...]).astype(BF16)
    kvn = (kv_lat * _rms_scale(kv_lat, KV_LORA_RANK) * kva_g_ref[...]).astype(BF16)
    qn_s[...] = _dot(qn, w_qn_ref[...])
    qr_s[...] = _dot(qn, w_qr_ref[...])
    kv = _dot(kvn, w_kv_ref[...])
    for h in range(N_HEADS):
        k0 = h * (QK_NOPE_DIM + V_HEAD_DIM)
        v0 = k0 + QK_NOPE_DIM
        kn_s[:, h * QK_NOPE_DIM:(h + 1) * QK_NOPE_DIM] = kv[:, k0:v0]
        v_s[:, h * V_HEAD_DIM:(h + 1) * V_HEAD_DIM] = kv[:, v0:v0 + V_HEAD_DIM].astype(BF16)


def _mla_prep(x2, pos, ln_g, w_lat, qa_g, kva_g, w_qn, w_qr, w_kv, qg_nope, qg_rope, kg_nope, kg_rope, inv_freq):
    t = x2.shape[0]
    tm = TM_PREP
    n = t // tm
    row = lambda i: (jnp.minimum(i, n - 1), 0)
    prev = lambda i: (jnp.maximum(i - 1, 0), 0)
    return pl.pallas_call(
        functools.partial(_mla_prep_kernel, q_scale=Q_SCALE),
        grid=(n + 1,),
        in_specs=[
            pl.BlockSpec((tm, D_MODEL), row),
            pl.BlockSpec((tm, 1), prev),
            _const_spec((1, D_MODEL)),
            _const_spec((LAT_WIDTH, D_MODEL)),
            _const_spec((1, Q_LORA_RANK)),
            _const_spec((1, KV_LORA_RANK)),
            _const_spec(w_qn.shape), _const_spec(w_qr.shape), _const_spec(w_kv.shape),
            _const_spec((1, LANES)), _const_spec((1, LANES)),
            _const_spec((1, LANES)), _const_spec((1, LANES)),
            _const_spec((1, LANES)),
        ],
        out_specs=[
            pl.BlockSpec((tm, N_HEADS * HEAD_SLOT), prev),
            pl.BlockSpec((tm, N_HEADS * HEAD_SLOT), prev),
            pl.BlockSpec((1, N_HEADS * V_HEAD_DIM, tm), lambda i: (jnp.maximum(i - 1, 0), 0, 0)),
            pl.BlockSpec((tm, D_MODEL), row),
        ],
        out_shape=[
            jax.ShapeDtypeStruct((t, N_HEADS * HEAD_SLOT), BF16),
            jax.ShapeDtypeStruct((t, N_HEADS * HEAD_SLOT), BF16),
            jax.ShapeDtypeStruct((n, N_HEADS * V_HEAD_DIM, tm), BF16),
            jax.ShapeDtypeStruct((t, D_MODEL), BF16),
        ],
        scratch_shapes=[
            pltpu.VMEM((tm, N_HEADS * QK_NOPE_DIM), F32),
            pltpu.VMEM((tm, N_HEADS * QK_ROPE_DIM), F32),
            pltpu.VMEM((tm, N_HEADS * QK_NOPE_DIM), F32),
            pltpu.VMEM((tm, N_HEADS * V_HEAD_DIM), BF16),
            pltpu.VMEM((tm, LANES), F32),
        ],
        compiler_params=_params("arbitrary"),
        name="mla_prep",
    )(x2, pos, ln_g, w_lat, qa_g, kva_g, w_qn, w_qr, w_kv, qg_nope, qg_rope, kg_nope, kg_rope, inv_freq)


def _in_proj_kernel(u_ref, w_ref, z_ref):
    z_ref[...] = _dot_nt(u_ref[...], w_ref[...].astype(BF16)).astype(BF16)


def _in_proj(u, w_in_t):
    t = u.shape[0]
    tm, tn = TM_INPROJ, TN_INPROJ
    gate_tiles = 2 * D_MODEL // tn

    def w_rows(i, j):
        rows = jnp.where(j < gate_tiles, GATE_OFFSET + j * tn, (j - gate_tiles) * tn)
        return pl.multiple_of(rows, math.gcd(GATE_OFFSET, tn)), 0

    return pl.pallas_call(
        _in_proj_kernel,
        grid=(t // tm, Z_WIDTH // tn),
        in_specs=[
            pl.BlockSpec((tm, D_MODEL), lambda i, j: (i, 0)),
            pl.BlockSpec((pl.Element(tn), pl.Element(D_MODEL)), w_rows),
        ],
        out_specs=pl.BlockSpec((tm, tn), lambda i, j: (i, j)),
        out_shape=jax.ShapeDtypeStruct((t, Z_WIDTH), BF16),
        compiler_params=_params("arbitrary", "arbitrary"),
        name="in_proj",
    )(u, w_in_t)


def _attn_kernel(qg_ref, kg_ref, q_ref, k_ref, vt_ref, wc_f32_ref, wm_f32_ref, wo_f32_ref, wd_f32_ref,
                 o_ref, wc_bf16_ref, wm_bf16_ref, wo_bf16_ref, wd_bf16_ref, m_ref, l_ref, acc_ref,
                 *, tq, tk, vt_cols, q_scale):
    i = pl.program_id(2)
    sub_blocks = tk // vt_cols
    kv_per_q = tq // tk
    gq = jnp.max(jnp.abs(qg_ref[...]), axis=-1, keepdims=True)
    gk = jnp.max(jnp.abs(kg_ref[...]), axis=-1, keepdims=True)
    bound = (gq * gk)[0, 0] * (q_scale * QK_HEAD_DIM)
    fixed_shift = bound <= FIXED_SHIFT_LIMIT
    _cast_slabs((wc_f32_ref, wm_f32_ref, wo_f32_ref, wd_f32_ref),
                (wc_bf16_ref, wm_bf16_ref, wo_bf16_ref, wd_bf16_ref))

    def score(j, hh, col0):
        start = pl.multiple_of(j * tk, tk)
        q = q_ref[col0:, hh * HEAD_SLOT:(hh + 1) * HEAD_SLOT]
        kb = k_ref[pl.ds(start, tk), hh * HEAD_SLOT:(hh + 1) * HEAD_SLOT]
        return _dot_nt(kb, q)

    def per_head(j, col0, fn):
        sc = {hh: score(j, hh, col0) for hh in range(SCORES_AHEAD)}
        for hh in range(HEADS_PER_STEP):
            fn(hh, sc.pop(hh))
            if hh + SCORES_AHEAD < HEADS_PER_STEP:
                sc[hh + SCORES_AHEAD] = score(j, hh + SCORES_AHEAD, col0)

    def causal(s):
        kv_idx = lax.broadcasted_iota(jnp.int32, s.shape, 0)
        q_idx = lax.broadcasted_iota(jnp.int32, s.shape, 1)
        return jnp.where(kv_idx <= q_idx, s, NEG_INF)

    def p_times_v(j, hh, p):
        vt = jnp.concatenate([vt_ref[j * sub_blocks + c, hh * V_HEAD_DIM:(hh + 1) * V_HEAD_DIM, :]
                              for c in range(sub_blocks)], axis=1)
        return _dot(vt, p.astype(BF16))

    def sweep(step):
        def body(g, c):
            for d in range(kv_per_q):
                step(g * kv_per_q + d, 0, False)
            return c

        lax.fori_loop(0, i, body, 0)
        for d in range(kv_per_q):
            step(i * kv_per_q + d, d * tk, True)
        for hh in range(HEADS_PER_STEP):
            o_ref[:, hh * V_HEAD_DIM:(hh + 1) * V_HEAD_DIM] = (acc_ref[hh] / l_ref[hh]).T.astype(BF16)

    l_ref[...] = jnp.zeros_like(l_ref)
    acc_ref[...] = jnp.zeros_like(acc_ref)

    @pl.when(fixed_shift)
    def _():
        def step(j, col0, masked):
            def head(hh, s):
                p = jnp.exp2((causal(s) if masked else s) - bound)
                l_ref[hh, :, col0:] += jnp.sum(p, axis=0, keepdims=True)
                acc_ref[hh, :, col0:] += p_times_v(j, hh, p)

            per_head(j, col0, head)

        sweep(step)

    @pl.when(jnp.logical_not(fixed_shift))
    def _():
        m_ref[...] = jnp.full_like(m_ref, NEG_INF)

        def step(j, col0, masked):
            def head(hh, s):
                s = causal(s) if masked else s
                m = m_ref[hh, :, col0:]
                m_new = jnp.maximum(m, jnp.max(s, axis=0, keepdims=True))
                alpha = jnp.exp2(m - m_new)
                p = jnp.exp2(s - m_new)
                m_ref[hh, :, col0:] = m_new
                l_ref[hh, :, col0:] = alpha * l_ref[hh, :, col0:] + jnp.sum(p, axis=0, keepdims=True)
                acc_ref[hh, :, col0:] = alpha * acc_ref[hh, :, col0:] + p_times_v(j, hh, p)

            per_head(j, col0, head)

        sweep(step)


def _attn(q, k, vt, q_norm_g, k_norm_g, later_weights, batch, seq):
    tq, tk = TQ_ATTN, TK_ATTN
    nq = seq // tq
    vt_cols = vt.shape[-1]
    hs = HEADS_PER_STEP
    groups = N_HEADS // hs
    cast_in, cast_out, cast_shape = _cast_sidecar(later_weights, batch * groups * nq,
                                                  lambda b, h, i: (b * groups + h) * nq + i)
    return pl.pallas_call(
        functools.partial(_attn_kernel, tq=tq, tk=tk, vt_cols=vt_cols, q_scale=Q_SCALE),
        grid=(batch, groups, nq),
        in_specs=[
            _const_spec((1, QK_HEAD_DIM)),
            _const_spec((1, QK_HEAD_DIM)),
            pl.BlockSpec((tq, hs * HEAD_SLOT), lambda b, h, i: (b * nq + i, h)),
            pl.BlockSpec((seq, hs * HEAD_SLOT), lambda b, h, i: (b, h)),
            pl.BlockSpec((seq // vt_cols, hs * V_HEAD_DIM, vt_cols), lambda b, h, i: (b, h, 0)),
        ] + cast_in,
        out_specs=[pl.BlockSpec((tq, hs * V_HEAD_DIM), lambda b, h, i: (b * nq + i, h))] + cast_out,
        out_shape=[jax.ShapeDtypeStruct((batch * seq, N_HEADS * V_HEAD_DIM), BF16)] + cast_shape,
        scratch_shapes=[
            pltpu.VMEM((hs, 1, tq), F32),
            pltpu.VMEM((hs, 1, tq), F32),
            pltpu.VMEM((hs, V_HEAD_DIM, tq), F32),
        ],
        compiler_params=_params("arbitrary", "arbitrary", "arbitrary"),
        name="attn",
    )(q_norm_g, k_norm_g, q, k, vt, *later_weights)


def _mix_out_kernel(gates_ref, zb_ref, zc_ref, zv_ref, zc_prev_ref, zv_prev_ref, o_ref, x_ref,
                    b_gate_ref, conv_w_ref, w_conv_out_ref, w_mla_out_ref, w_o_ref, ln2_g_ref, w_up_f32_ref,
                    h_ref, u2_ref, w_up_bf16_ref, *, tm, tiles_per_seq):
    for tile in range(D_FF // TF_FFN):
        for part in range(2):
            src = part * D_FF + tile * TF_FFN
            dst = (2 * tile + part) * TF_FFN
            w_up_bf16_ref[:, dst:dst + TF_FFN] = w_up_f32_ref[:, src:src + TF_FFN].astype(BF16)

    seq_start = (pl.program_id(0) % tiles_per_seq) == 0
    prev = zc_prev_ref[...].astype(F32) * zv_prev_ref[...].astype(F32)
    cv = jnp.concatenate([jnp.where(seq_start, 0.0, prev), zc_ref[...].astype(F32) * zv_ref[...].astype(F32)], axis=0)
    conv = _causal_conv3(cv, conv_w_ref[...], tm)
    conv_in = (zb_ref[...].astype(F32) * conv).astype(BF16)
    y_conv = _dot(conv_in, w_conv_out_ref[...])
    y_mla = _dot(o_ref[...], w_mla_out_ref[...])
    gates = jax.nn.sigmoid(gates_ref[...].astype(F32) + b_gate_ref[...])
    mix = (gates[:, :D_MODEL] * y_conv + gates[:, D_MODEL:] * y_mla).astype(BF16)
    h = x_ref[...] + _dot(mix, w_o_ref[...])
    h_ref[...] = h
    u2_ref[...] = (h * _rms_scale(h, D_MODEL) * ln2_g_ref[...]).astype(BF16)


def _mix_out(z, o, x2, b_gate, conv_w, w_conv_out, w_mla_out, w_o, ln2_g, w_ffn_up, seq):
    t = x2.shape[0]
    tm = TM_MIX
    halo_per_tile = tm // HALO
    cast_in, cast_out, cast_shape = _cast_sidecar((w_ffn_up,), t // tm, lambda i: i)
    cblk = lambda c: (lambda i: (i, c))
    prev = lambda c: (lambda i: (jnp.maximum(i * halo_per_tile - 1, 0), c))
    gate_blocks = 2 * D_MODEL // CONV_DIM
    return pl.pallas_call(
        functools.partial(_mix_out_kernel, tm=tm, tiles_per_seq=seq // tm),
        grid=(t // tm,),
        in_specs=[
            pl.BlockSpec((tm, 2 * D_MODEL), cblk(0)),
            pl.BlockSpec((tm, CONV_DIM), cblk(gate_blocks)),
            pl.BlockSpec((tm, CONV_DIM), cblk(gate_blocks + 1)),
            pl.BlockSpec((tm, CONV_DIM), cblk(gate_blocks + 2)),
            pl.BlockSpec((HALO, CONV_DIM), prev(gate_blocks + 1)),
            pl.BlockSpec((HALO, CONV_DIM), prev(gate_blocks + 2)),
            pl.BlockSpec((tm, D_MODEL), cblk(0)),
            pl.BlockSpec((tm, D_MODEL), cblk(0)),
            _const_spec((1, 2 * D_MODEL)),
            _const_spec((CONV_WIDTH, CONV_DIM)),
            _const_spec((CONV_DIM, D_MODEL)),
            _const_spec((D_MODEL, D_MODEL)),
            _const_spec((D_MODEL, D_MODEL)),
            _const_spec((1, D_MODEL)),
        ] + cast_in,
        out_specs=[pl.BlockSpec((tm, D_MODEL), cblk(0)), pl.BlockSpec((tm, D_MODEL), cblk(0))] + cast_out,
        out_shape=[jax.ShapeDtypeStruct((t, D_MODEL), F32), jax.ShapeDtypeStruct((t, D_MODEL), BF16)] + cast_shape,
        compiler_params=_params("arbitrary"),
        name="mix_out",
    )(z, z, z, z, z, z, o, x2, b_gate, conv_w, w_conv_out, w_mla_out, w_o, ln2_g, w_ffn_up)


def _ffn_kernel(u_ref, u_prev_ref, h_ref, w_up_ref, cw_ref, b_ref, wd_ref, out_ref, lhs_ref, *p_refs,
                tm, tf, row_chunk, tiles_per_seq, n_tiles):
    group = len(p_refs)
    jj = pl.program_id(1)

    @pl.when(jj == 0)
    def _():
        seq_start = (pl.program_id(0) % tiles_per_seq) == 0
        lhs_ref[0:HALO, :] = jnp.where(seq_start, jnp.zeros_like(u_prev_ref), u_prev_ref[...])
        lhs_ref[HALO:, :] = u_ref[...]
        out_ref[...] = h_ref[...]

    def d_ff_tile(c):
        p_ref = p_refs[c]
        cols = slice(c * 2 * tf, (c + 1) * 2 * tf)
        bounds = [0] + [HALO + (r + 1) * row_chunk for r in range(tm // row_chunk)]
        for lo, hi in zip(bounds[:-1], bounds[1:]):
            p_ref[lo:hi, :] = _dot(lhs_ref[lo:hi, :], w_up_ref[:, cols])
        cw = cw_ref[:, cols]
        for r0 in range(0, tm, row_chunk):
            a = (cw[0:1, :] * p_ref[r0 + HALO - 2:r0 + HALO - 2 + row_chunk, :]
                 + cw[1:2, :] * p_ref[r0 + HALO - 1:r0 + HALO - 1 + row_chunk, :]
                 + cw[2:3, :] * p_ref[r0 + HALO:r0 + HALO + row_chunk, :]) + b_ref[:, cols]
            a_gate, a_up = a[:, :tf], a[:, tf:]
            act = (a_gate * jax.nn.sigmoid(a_gate) * a_up).astype(BF16)
            out_ref[r0:r0 + row_chunk, :] += _dot(act, wd_ref[c * tf:(c + 1) * tf, :])

    rest = n_tiles % group

    @pl.when(jj >= (1 if rest else 0))
    def _():
        for c in range(group):
            d_ff_tile(c)

    if rest:
        @pl.when(jj == 0)
        def _():
            for c in range(rest):
                d_ff_tile(c)


def _pair_tiles(v, tf):
    rows = v.shape[0]
    return v.reshape(rows, 2, D_FF // tf, tf).transpose(0, 2, 1, 3).reshape(rows, 2 * D_FF)


def _ffn(u2, h1, w_up_paired, conv_w, conv_b, w_down, seq):
    t = u2.shape[0]
    tm, tf, group = TM_FFN, TF_FFN, FFN_TILE_GROUP
    n_tiles = D_FF // tf
    n_groups = pl.cdiv(n_tiles, group)
    halo_per_tile = tm // HALO
    blk = (lambda j: (j + n_groups - 1) % n_groups) if n_tiles % group else (lambda j: j)
    return pl.pallas_call(
        functools.partial(_ffn_kernel, tm=tm, tf=tf, row_chunk=FFN_ROW_CHUNK, tiles_per_seq=seq // tm,
                          n_tiles=n_tiles),
        grid=(t // tm, n_groups),
        in_specs=[
            pl.BlockSpec((tm, D_MODEL), lambda i, j: (i, 0)),
            pl.BlockSpec((HALO, D_MODEL), lambda i, j: (jnp.maximum(i * halo_per_tile - 1, 0), 0)),
            pl.BlockSpec((tm, D_MODEL), lambda i, j: (i, 0)),
            pl.BlockSpec((D_MODEL, group * 2 * tf), lambda i, j: (0, blk(j))),
            pl.BlockSpec((CONV_WIDTH, group * 2 * tf), lambda i, j: (0, blk(j))),
            pl.BlockSpec((1, group * 2 * tf), lambda i, j: (0, blk(j))),
            pl.BlockSpec((group * tf, D_MODEL), lambda i, j: (blk(j), 0)),
        ],
        out_specs=pl.BlockSpec((tm, D_MODEL), lambda i, j: (i, 0)),
        out_shape=jax.ShapeDtypeStruct((t, D_MODEL), F32),
        scratch_shapes=[pltpu.VMEM((tm + HALO, D_MODEL), BF16)]
        + [pltpu.VMEM((tm + HALO, 2 * tf), F32) for _ in range(group)],
        compiler_params=_params("arbitrary", "arbitrary"),
        name="ffn",
    )(u2, u2, h1, w_up_paired, _pair_tiles(conv_w, tf), _pair_tiles(conv_b, tf), w_down)


def _row(v):
    return v.reshape(1, -1).astype(F32)


def _layer(h2, pos, inv_freq, batch, seq, ln1_g, w_in, b_gate, conv_w, w_conv_out, q_a_g, w_q_b, kv_a_g, w_kv_b,
           q_norm_g, k_norm_g, w_mla_out, w_o, ln2_g, w_ffn_up, ffn_conv_w, ffn_conv_b, w_ffn_down):
    w_in_t = w_in.T
    w_lat = w_in_t[LAT_OFFSET:LAT_OFFSET + LAT_WIDTH].astype(BF16)
    wq3 = w_q_b.reshape(Q_LORA_RANK, N_HEADS, QK_HEAD_DIM)
    w_qn = wq3[:, :, :QK_NOPE_DIM].reshape(Q_LORA_RANK, -1).astype(BF16)
    w_qr = wq3[:, :, QK_NOPE_DIM:].reshape(Q_LORA_RANK, -1).astype(BF16)
    w_kv = w_kv_b.astype(BF16)
    twice = lambda g: _row(jnp.concatenate([g, g]))

    q, k, vt, u = _mla_prep(
        h2, pos, _row(ln1_g), w_lat, _row(q_a_g), _row(kv_a_g), w_qn, w_qr, w_kv,
        _row(q_norm_g[:QK_NOPE_DIM]), twice(q_norm_g[QK_NOPE_DIM:]),
        _row(k_norm_g[:QK_NOPE_DIM]), twice(k_norm_g[QK_NOPE_DIM:]), inv_freq)
    z = _in_proj(u, w_in_t)
    o, w_conv_out, w_mla_out, w_o, w_down = _attn(q, k, vt, _row(q_norm_g), _row(k_norm_g),
                                                  (w_conv_out, w_mla_out, w_o, w_ffn_down), batch, seq)
    h1, u2, w_up = _mix_out(z, o, h2, _row(b_gate), conv_w.astype(F32), w_conv_out, w_mla_out, w_o, _row(ln2_g),
                            w_ffn_up, seq)
    return _ffn(u2, h1, w_up, ffn_conv_w.astype(F32), _row(ffn_conv_b), w_down, seq)


def kernel(x, positions, ln1_g, w_in, b_gate, conv_w, w_conv_out, q_a_g, w_q_b, kv_a_g, w_kv_b,
           q_norm_g, k_norm_g, w_mla_out, w_o, ln2_g, w_ffn_up, ffn_conv_w, ffn_conv_b, w_ffn_down):
    batch, seq, _ = x.shape
    depth = ln1_g.shape[0]
    h2 = x.reshape(batch * seq, D_MODEL)
    pos = positions.reshape(batch * seq, 1).astype(jnp.int32)
    freq = ROPE_THETA ** (-jnp.arange(0, QK_ROPE_DIM, 2, dtype=F32) / QK_ROPE_DIM)
    inv_freq = jnp.tile(freq, LANES // (QK_ROPE_DIM // 2)).reshape(1, LANES)
    for l in range(depth):
        h2 = _layer(h2, pos, inv_freq, batch, seq, ln1_g[l], w_in[l], b_gate[l], conv_w[l], w_conv_out[l],
                    q_a_g[l], w_q_b[l], kv_a_g[l], w_kv_b[l], q_norm_g[l], k_norm_g[l], w_mla_out[l],
                    w_o[l], ln2_g[l], w_ffn_up[l], ffn_conv_w[l], ffn_conv_b[l], w_ffn_down[l])
    return h2.reshape(batch, seq, D_MODEL)
```

```python
import functools
import math

import jax
import jax.numpy as jnp
from jax import lax
from jax.experimental import pallas as pl
from jax.experimental.pallas import tpu as pltpu

D_MODEL = 2048
CONV_DIM = 1024
CONV_WIDTH = 3
N_HEADS = 16
QK_NOPE_DIM = 128
QK_ROPE_DIM = 64
QK_HEAD_DIM = QK_NOPE_DIM + QK_ROPE_DIM
V_HEAD_DIM = 128
Q_LORA_RANK = 768
KV_LORA_RANK = 512
ROPE_THETA = 10000.0
D_FF = 5632
NORM_EPS = 1e-6
NEG_INF = -1e30

Q_SCALE = QK_HEAD_DIM ** -0.5 * math.log2(math.e)
FIXED_SHIFT_LIMIT = 48.0

LANES = 128
HEAD_SLOT = 2 * LANES
HALO = 16
LAT_OFFSET = 3 * CONV_DIM
LAT_WIDTH = Q_LORA_RANK + KV_LORA_RANK + LANES
GATE_OFFSET = LAT_OFFSET + Q_LORA_RANK + KV_LORA_RANK + QK_ROPE_DIM
Z_WIDTH = 2 * D_MODEL + 3 * CONV_DIM
VMEM_LIMIT = 56 * 1024 * 1024

TM_PREP = 256
TM_INPROJ = 2048
TN_INPROJ = 1024
TQ_ATTN = 1024
TK_ATTN = 512
HEADS_PER_STEP = 4
TM_MIX = 256
TM_FFN = 512
TF_FFN = 512
FFN_ROW_CHUNK = 256
FFN_TILE_GROUP = 2

F32 = jnp.float32
BF16 = jnp.bfloat16


def _dot(a, b):
    return jnp.dot(a, b, preferred_element_type=F32)


def _dot_nt(a, b):
    return lax.dot_general(a, b, (((1,), (1,)), ((), ())), preferred_element_type=F32)


def _rms_scale(x, width):
    return lax.rsqrt(jnp.sum(x * x, axis=-1, keepdims=True) * (1.0 / width) + NORM_EPS)


def _params(*sem):
    return pltpu.CompilerParams(dimension_semantics=sem, vmem_limit_bytes=VMEM_LIMIT)


def _const_spec(shape):
    return pl.BlockSpec(shape, lambda *_: (0,) * len(shape), pipeline_mode=pl.Buffered(1))


def _cast_sidecar(weights, n_steps, step_index):
    in_specs, out_specs, out_shape = [], [], []
    for w in weights:
        rows, cols = w.shape
        slab = rows // n_steps
        assert slab * n_steps == rows and slab % 16 == 0, (w.shape, n_steps)
        spec = pl.BlockSpec((slab, cols), lambda *ids: (step_index(*ids), 0))
        in_specs.append(spec)
        out_specs.append(spec)
        out_shape.append(jax.ShapeDtypeStruct(w.shape, BF16))
    return in_specs, out_specs, out_shape


def _cast_slabs(src_refs, dst_refs):
    for src, dst in zip(src_refs, dst_refs):
        dst[...] = src[...].astype(BF16)


def _causal_conv3(p, cw, rows):
    full = cw[2:3, :] * p + cw[1:2, :] * pltpu.roll(p, 1, axis=0) + cw[0:1, :] * pltpu.roll(p, 2, axis=0)
    return full[p.shape[0] - rows:, :]


def _mla_prep_kernel(x_ref, pos_ref, ln_g_ref, w_lat_ref, qa_g_ref, kva_g_ref, w_qn_ref, w_qr_ref,
                     w_kv_ref, qg_nope_ref, qg_rope_ref, kg_nope_ref, kg_rope_ref, inv_freq_ref,
                     q_ref, k_ref, vt_ref, u_ref, qn_s, qr_s, kn_s, v_s, kr_s, *, q_scale):
    @pl.when(pl.program_id(0) == 0)
    def _():
        for s in (qn_s, qr_s, kn_s, v_s, kr_s):
            s[...] = jnp.zeros_like(s)

    ang = pos_ref[...].astype(F32) * inv_freq_ref[...]
    cos = jnp.cos(ang)
    sin = jnp.sin(ang)
    lane = lax.broadcasted_iota(jnp.int32, (1, LANES), 1)
    first_half = (lane % QK_ROPE_DIM) < (QK_ROPE_DIM // 2)
    sin_signed = jnp.where(first_half, -sin, sin)

    def rope(t):
        rot = jnp.where(first_half, pltpu.roll(t, LANES - QK_ROPE_DIM // 2, axis=1),
                        pltpu.roll(t, QK_ROPE_DIM // 2, axis=1))
        return t * cos + rot * sin_signed

    k_rope = kr_s[...]
    k_rope = jnp.where(lane < QK_ROPE_DIM, k_rope, pltpu.roll(k_rope, QK_ROPE_DIM, axis=1))
    qg_nope = qg_nope_ref[...] * q_scale
    qg_rope = qg_rope_ref[...] * q_scale
    kg_nope = kg_nope_ref[...]
    k_tail = rope(k_rope * kg_rope_ref[...])
    k_rope_ss = 0.5 * jnp.sum(k_rope * k_rope, axis=-1, keepdims=True)

    for pair in range(N_HEADS // 2):
        q_pair = qr_s[:, pair * LANES:(pair + 1) * LANES]
        q_pair_roped = rope(q_pair * qg_rope)
        for sub in range(2):
            h = 2 * pair + sub
            mine = (lane // QK_ROPE_DIM) == sub
            q_nope = qn_s[:, h * QK_NOPE_DIM:(h + 1) * QK_NOPE_DIM]
            ss = jnp.sum(q_nope * q_nope + jnp.where(mine, q_pair * q_pair, 0.0), axis=-1, keepdims=True)
            r = lax.rsqrt(ss * (1.0 / QK_HEAD_DIM) + NORM_EPS)
            q_ref[:, h * HEAD_SLOT: h * HEAD_SLOT + LANES] = (q_nope * r * qg_nope).astype(BF16)
            q_ref[:, h * HEAD_SLOT + LANES:(h + 1) * HEAD_SLOT] = jnp.where(
                mine, q_pair_roped * r, 0.0).astype(BF16)

    for h in range(N_HEADS):
        k_nope = kn_s[:, h * QK_NOPE_DIM:(h + 1) * QK_NOPE_DIM]
        kss = jnp.sum(k_nope * k_nope, axis=-1, keepdims=True) + k_rope_ss
        kr = lax.rsqrt(kss * (1.0 / QK_HEAD_DIM) + NORM_EPS)
        k_ref[:, h * HEAD_SLOT: h * HEAD_SLOT + LANES] = (k_nope * kr * kg_nope).astype(BF16)
        k_ref[:, h * HEAD_SLOT + LANES:(h + 1) * HEAD_SLOT] = (k_tail * kr).astype(BF16)

    vt_ref[0] = v_s[...].T

    xf = x_ref[...]
    u = (xf * _rms_scale(xf, D_MODEL) * ln_g_ref[...]).astype(BF16)
    u_ref[...] = u
    q_lat = _dot_nt(u, w_lat_ref[:Q_LORA_RANK, :])
    kv_lat = _dot_nt(u, w_lat_ref[Q_LORA_RANK:Q_LORA_RANK + KV_LORA_RANK + LANES, :])
    kr_s[...] = kv_lat[:, KV_LORA_RANK:]
    kv_lat = kv_lat[:, :KV_LORA_RANK]
    qn = (q_lat * _rms_scale(q_lat, Q_LORA_RANK) * qa_g_ref[...]).astype(BF16)
    kvn = (kv_lat * _rms_scale(kv_lat, KV_LORA_RANK) * kva_g_ref[...]).astype(BF16)
    qn_s[...] = _dot(qn, w_qn_ref[...])
    qr_s[...] = _dot(qn, w_qr_ref[...])
    kv = _dot(kvn, w_kv_ref[...])
    for h in range(N_HEADS):
        k0 = h * (QK_NOPE_DIM + V_HEAD_DIM)
        v0 = k0 + QK_NOPE_DIM
        kn_s[:, h * QK_NOPE_DIM:(h + 1) * QK_NOPE_DIM] = kv[:, k0:v0]
        v_s[:, h * V_HEAD_DIM:(h + 1) * V_HEAD_DIM] = kv[:, v0:v0 + V_HEAD_DIM].astype(BF16)


def _mla_prep(x2, pos, ln_g, w_lat, qa_g, kva_g, w_qn, w_qr, w_kv, qg_nope, qg_rope, kg_nope, kg_rope, inv_freq):
    t = x2.shape[0]
    tm = TM_PREP
    n = t // tm
    row = lambda i: (jnp.minimum(i, n - 1), 0)
    prev = lambda i: (jnp.maximum(i - 1, 0), 0)
    return pl.pallas_call(
        functools.partial(_mla_prep_kernel, q_scale=Q_SCALE),
        grid=(n + 1,),
        in_specs=[
            pl.BlockSpec((tm, D_MODEL), row),
            pl.BlockSpec((tm, 1), prev),
            _const_spec((1, D_MODEL)),
            _const_spec((LAT_WIDTH, D_MODEL)),
            _const_spec((1, Q_LORA_RANK)),
            _const_spec((1, KV_LORA_RANK)),
            _const_spec(w_qn.shape), _const_spec(w_qr.shape), _const_spec(w_kv.shape),
            _const_spec((1, LANES)), _const_spec((1, LANES)),
            _const_spec((1, LANES)), _const_spec((1, LANES)),
            _const_spec((1, LANES)),
        ],
        out_specs=[
            pl.BlockSpec((tm, N_HEADS * HEAD_SLOT), prev),
            pl.BlockSpec((tm, N_HEADS * HEAD_SLOT), prev),
            pl.BlockSpec((1, N_HEADS * V_HEAD_DIM, tm), lambda i: (jnp.maximum(i - 1, 0), 0, 0)),
            pl.BlockSpec((tm, D_MODEL), row),
        ],
        out_shape=[
            jax.ShapeDtypeStruct((t, N_HEADS * HEAD_SLOT), BF16),
            jax.ShapeDtypeStruct((t, N_HEADS * HEAD_SLOT), BF16),
            jax.ShapeDtypeStruct((n, N_HEADS * V_HEAD_DIM, tm), BF16),
            jax.ShapeDtypeStruct((t, D_MODEL), BF16),
        ],
        scratch_shapes=[
            pltpu.VMEM((tm, N_HEADS * QK_NOPE_DIM), F32),
            pltpu.VMEM((tm, N_HEADS * QK_ROPE_DIM), F32),
            pltpu.VMEM((tm, N_HEADS * QK_NOPE_DIM), F32),
            pltpu.VMEM((tm, N_HEADS * V_HEAD_DIM), BF16),
            pltpu.VMEM((tm, LANES), F32),
        ],
        compiler_params=_params("arbitrary"),
        name="mla_prep",
    )(x2, pos, ln_g, w_lat, qa_g, kva_g, w_qn, w_qr, w_kv, qg_nope, qg_rope, kg_nope, kg_rope, inv_freq)


def _in_proj_kernel(u_ref, w_ref, z_ref):
    z_ref[...] = _dot_nt(u_ref[...], w_ref[...].astype(BF16)).astype(BF16)


def _in_proj(u, w_in_t):
    t = u.shape[0]
    tm, tn = TM_INPROJ, TN_INPROJ
    gate_tiles = 2 * D_MODEL // tn

    def w_rows(i, j):
        rows = jnp.where(j < gate_tiles, GATE_OFFSET + j * tn, (j - gate_tiles) * tn)
        return pl.multiple_of(rows, math.gcd(GATE_OFFSET, tn)), 0

    return pl.pallas_call(
        _in_proj_kernel,
        grid=(t // tm, Z_WIDTH // tn),
        in_specs=[
            pl.BlockSpec((tm, D_MODEL), lambda i, j: (i, 0)),
            pl.BlockSpec((pl.Element(tn), pl.Element(D_MODEL)), w_rows),
        ],
        out_specs=pl.BlockSpec((tm, tn), lambda i, j: (i, j)),
        out_shape=jax.ShapeDtypeStruct((t, Z_WIDTH), BF16),
        compiler_params=_params("arbitrary", "arbitrary"),
        name="in_proj",
    )(u, w_in_t)


def _attn_kernel(qg_ref, kg_ref, q_ref, k_ref, vt_ref, wc_f32_ref, wm_f32_ref, wo_f32_ref, wd_f32_ref,
                 o_ref, wc_bf16_ref, wm_bf16_ref, wo_bf16_ref, wd_bf16_ref, m_ref, l_ref, acc_ref,
                 *, tq, tk, vt_cols, q_scale):
    i = pl.program_id(2)
    sub_blocks = tk // vt_cols
    kv_per_q = tq // tk
    gq = jnp.max(jnp.abs(qg_ref[...]), axis=-1, keepdims=True)
    gk = jnp.max(jnp.abs(kg_ref[...]), axis=-1, keepdims=True)
    bound = (gq * gk)[0, 0] * (q_scale * QK_HEAD_DIM)
    fixed_shift = bound <= FIXED_SHIFT_LIMIT
    _cast_slabs((wc_f32_ref, wm_f32_ref, wo_f32_ref, wd_f32_ref),
                (wc_bf16_ref, wm_bf16_ref, wo_bf16_ref, wd_bf16_ref))

    def score(j, hh, col0):
        start = pl.multiple_of(j * tk, tk)
        q = q_ref[col0:, hh * HEAD_SLOT:(hh + 1) * HEAD_SLOT]
        kb = k_ref[pl.ds(start, tk), hh * HEAD_SLOT:(hh + 1) * HEAD_SLOT]
        return _dot_nt(kb, q)

    def per_head(j, col0, fn):
        scores = [score(j, hh, col0) for hh in range(HEADS_PER_STEP)]
        for hh, s in enumerate(scores):
            fn(hh, s)

    def causal(s):
        kv_idx = lax.broadcasted_iota(jnp.int32, s.shape, 0)
        q_idx = lax.broadcasted_iota(jnp.int32, s.shape, 1)
        return jnp.where(kv_idx <= q_idx, s, NEG_INF)

    def p_times_v(j, hh, p):
        vt = jnp.concatenate([vt_ref[j * sub_blocks + c, hh * V_HEAD_DIM:(hh + 1) * V_HEAD_DIM, :]
                              for c in range(sub_blocks)], axis=1)
        return _dot(vt, p.astype(BF16))

    def sweep(step):
        def body(g, c):
            for d in range(kv_per_q):
                step(g * kv_per_q + d, 0, False)
            return c

        lax.fori_loop(0, i, body, 0)
        for d in range(kv_per_q):
            step(i * kv_per_q + d, d * tk, True)
        for hh in range(HEADS_PER_STEP):
            o_ref[:, hh * V_HEAD_DIM:(hh + 1) * V_HEAD_DIM] = (acc_ref[hh] / l_ref[hh]).T.astype(BF16)

    l_ref[...] = jnp.zeros_like(l_ref)
    acc_ref[...] = jnp.zeros_like(acc_ref)

    @pl.when(fixed_shift)
    def _():
        def step(j, col0, masked):
            def head(hh, s):
                p = jnp.exp2((causal(s) if masked else s) - bound)
                l_ref[hh, :, col0:] += jnp.sum(p, axis=0, keepdims=True)
                acc_ref[hh, :, col0:] += p_times_v(j, hh, p)

            per_head(j, col0, head)

        sweep(step)

    @pl.when(jnp.logical_not(fixed_shift))
    def _():
        m_ref[...] = jnp.full_like(m_ref, NEG_INF)

        def step(j, col0, masked):
            def head(hh, s):
                s = causal(s) if masked else s
                m = m_ref[hh, :, col0:]
                m_new = jnp.maximum(m, jnp.max(s, axis=0, keepdims=True))
                alpha = jnp.exp2(m - m_new)
                p = jnp.exp2(s - m_new)
                m_ref[hh, :, col0:] = m_new
                l_ref[hh, :, col0:] = alpha * l_ref[hh, :, col0:] + jnp.sum(p, axis=0, keepdims=True)
                acc_ref[hh, :, col0:] = alpha * acc_ref[hh, :, col0:] + p_times_v(j, hh, p)

            per_head(j, col0, head)

        sweep(step)


def _attn(q, k, vt, q_norm_g, k_norm_g, later_weights, batch, seq):
    tq, tk = TQ_ATTN, TK_ATTN
    nq = seq // tq
    vt_cols = vt.shape[-1]
    hs = HEADS_PER_STEP
    groups = N_HEADS // hs
    cast_in, cast_out, cast_shape = _cast_sidecar(later_weights, batch * groups * nq,
                                                  lambda b, h, i: (b * groups + h) * nq + i)
    return pl.pallas_call(
        functools.partial(_attn_kernel, tq=tq, tk=tk, vt_cols=vt_cols, q_scale=Q_SCALE),
        grid=(batch, groups, nq),
        in_specs=[
            _const_spec((1, QK_HEAD_DIM)),
            _const_spec((1, QK_HEAD_DIM)),
            pl.BlockSpec((tq, hs * HEAD_SLOT), lambda b, h, i: (b * nq + i, h)),
            pl.BlockSpec((seq, hs * HEAD_SLOT), lambda b, h, i: (b, h)),
            pl.BlockSpec((seq // vt_cols, hs * V_HEAD_DIM, vt_cols), lambda b, h, i: (b, h, 0)),
        ] + cast_in,
        out_specs=[pl.BlockSpec((tq, hs * V_HEAD_DIM), lambda b, h, i: (b * nq + i, h))] + cast_out,
        out_shape=[jax.ShapeDtypeStruct((batch * seq, N_HEADS * V_HEAD_DIM), BF16)] + cast_shape,
        scratch_shapes=[
            pltpu.VMEM((hs, 1, tq), F32),
            pltpu.VMEM((hs, 1, tq), F32),
            pltpu.VMEM((hs, V_HEAD_DIM, tq), F32),
        ],
        compiler_params=_params("arbitrary", "arbitrary", "arbitrary"),
        name="attn",
    )(q_norm_g, k_norm_g, q, k, vt, *later_weights)


def _mix_out_kernel(gates_ref, zb_ref, zc_ref, zv_ref, zc_prev_ref, zv_prev_ref, o_ref, x_ref,
                    b_gate_ref, conv_w_ref, w_conv_out_ref, w_mla_out_ref, w_o_ref, ln2_g_ref, w_up_f32_ref,
                    h_ref, u2_ref, w_up_bf16_ref, *, tm, tiles_per_seq):
    for tile in range(D_FF // TF_FFN):
        for part in range(2):
            src = part * D_FF + tile * TF_FFN
            dst = (2 * tile + part) * TF_FFN
            w_up_bf16_ref[:, dst:dst + TF_FFN] = w_up_f32_ref[:, src:src + TF_FFN].astype(BF16)

    seq_start = (pl.program_id(0) % tiles_per_seq) == 0
    prev = zc_prev_ref[...].astype(F32) * zv_prev_ref[...].astype(F32)
    cv = jnp.concatenate([jnp.where(seq_start, 0.0, prev), zc_ref[...].astype(F32) * zv_ref[...].astype(F32)], axis=0)
    conv = _causal_conv3(cv, conv_w_ref[...], tm)
    conv_in = (zb_ref[...].astype(F32) * conv).astype(BF16)
    y_conv = _dot(conv_in, w_conv_out_ref[...])
    y_mla = _dot(o_ref[...], w_mla_out_ref[...])
    gates = jax.nn.sigmoid(gates_ref[...].astype(F32) + b_gate_ref[...])
    mix = (gates[:, :D_MODEL] * y_conv + gates[:, D_MODEL:] * y_mla).astype(BF16)
    h = x_ref[...] + _dot(mix, w_o_ref[...])
    h_ref[...] = h
    u2_ref[...] = (h * _rms_scale(h, D_MODEL) * ln2_g_ref[...]).astype(BF16)


def _mix_out(z, o, x2, b_gate, conv_w, w_conv_out, w_mla_out, w_o, ln2_g, w_ffn_up, seq):
    t = x2.shape[0]
    tm = TM_MIX
    halo_per_tile = tm // HALO
    cast_in, cast_out, cast_shape = _cast_sidecar((w_ffn_up,), t // tm, lambda i: i)
    cblk = lambda c: (lambda i: (i, c))
    prev = lambda c: (lambda i: (jnp.maximum(i * halo_per_tile - 1, 0), c))
    gate_blocks = 2 * D_MODEL // CONV_DIM
    return pl.pallas_call(
        functools.partial(_mix_out_kernel, tm=tm, tiles_per_seq=seq // tm),
        grid=(t // tm,),
        in_specs=[
            pl.BlockSpec((tm, 2 * D_MODEL), cblk(0)),
            pl.BlockSpec((tm, CONV_DIM), cblk(gate_blocks)),
            pl.BlockSpec((tm, CONV_DIM), cblk(gate_blocks + 1)),
            pl.BlockSpec((tm, CONV_DIM), cblk(gate_blocks + 2)),
            pl.BlockSpec((HALO, CONV_DIM), prev(gate_blocks + 1)),
            pl.BlockSpec((HALO, CONV_DIM), prev(gate_blocks + 2)),
            pl.BlockSpec((tm, D_MODEL), cblk(0)),
            pl.BlockSpec((tm, D_MODEL), cblk(0)),
            _const_spec((1, 2 * D_MODEL)),
            _const_spec((CONV_WIDTH, CONV_DIM)),
            _const_spec((CONV_DIM, D_MODEL)),
            _const_spec((D_MODEL, D_MODEL)),
            _const_spec((D_MODEL, D_MODEL)),
            _const_spec((1, D_MODEL)),
        ] + cast_in,
        out_specs=[pl.BlockSpec((tm, D_MODEL), cblk(0)), pl.BlockSpec((tm, D_MODEL), cblk(0))] + cast_out,
        out_shape=[jax.ShapeDtypeStruct((t, D_MODEL), F32), jax.ShapeDtypeStruct((t, D_MODEL), BF16)] + cast_shape,
        compiler_params=_params("arbitrary"),
        name="mix_out",
    )(z, z, z, z, z, z, o, x2, b_gate, conv_w, w_conv_out, w_mla_out, w_o, ln2_g, w_ffn_up)


def _ffn_kernel(u_ref, u_prev_ref, h_ref, w_up_ref, cw_ref, b_ref, wd_ref, out_ref, lhs_ref, *p_refs,
                tm, tf, row_chunk, tiles_per_seq, n_tiles):
    group = len(p_refs)
    jj = pl.program_id(1)

    @pl.when(jj == 0)
    def _():
        seq_start = (pl.program_id(0) % tiles_per_seq) == 0
        lhs_ref[0:HALO, :] = jnp.where(seq_start, jnp.zeros_like(u_prev_ref), u_prev_ref[...])
        lhs_ref[HALO:, :] = u_ref[...]
        out_ref[...] = h_ref[...]

    def d_ff_tile(c):
        p_ref = p_refs[c]
        cols = slice(c * 2 * tf, (c + 1) * 2 * tf)
        bounds = [0] + [HALO + (r + 1) * row_chunk for r in range(tm // row_chunk)]
        for lo, hi in zip(bounds[:-1], bounds[1:]):
            p_ref[lo:hi, :] = _dot(lhs_ref[lo:hi, :], w_up_ref[:, cols])
        cw = cw_ref[:, cols]
        for r0 in range(0, tm, row_chunk):
            a = (cw[0:1, :] * p_ref[r0 + HALO - 2:r0 + HALO - 2 + row_chunk, :]
                 + cw[1:2, :] * p_ref[r0 + HALO - 1:r0 + HALO - 1 + row_chunk, :]
                 + cw[2:3, :] * p_ref[r0 + HALO:r0 + HALO + row_chunk, :]) + b_ref[:, cols]
            a_gate, a_up = a[:, :tf], a[:, tf:]
            act = (a_gate * jax.nn.sigmoid(a_gate) * a_up).astype(BF16)
            out_ref[r0:r0 + row_chunk, :] += _dot(act, wd_ref[c * tf:(c + 1) * tf, :])

    rest = n_tiles % group

    @pl.when(jj >= (1 if rest else 0))
    def _():
        for c in range(group):
            d_ff_tile(c)

    if rest:
        @pl.when(jj == 0)
        def _():
            for c in range(rest):
                d_ff_tile(c)


def _pair_tiles(v, tf):
    rows = v.shape[0]
    return v.reshape(rows, 2, D_FF // tf, tf).transpose(0, 2, 1, 3).reshape(rows, 2 * D_FF)


def _ffn(u2, h1, w_up_paired, conv_w, conv_b, w_down, seq):
    t = u2.shape[0]
    tm, tf, group = TM_FFN, TF_FFN, FFN_TILE_GROUP
    n_tiles = D_FF // tf
    n_groups = pl.cdiv(n_tiles, group)
    halo_per_tile = tm // HALO
    blk = (lambda j: (j + n_groups - 1) % n_groups) if n_tiles % group else (lambda j: j)
    return pl.pallas_call(
        functools.partial(_ffn_kernel, tm=tm, tf=tf, row_chunk=FFN_ROW_CHUNK, tiles_per_seq=seq // tm,
                          n_tiles=n_tiles),
        grid=(t // tm, n_groups),
        in_specs=[
            pl.BlockSpec((tm, D_MODEL), lambda i, j: (i, 0)),
            pl.BlockSpec((HALO, D_MODEL), lambda i, j: (jnp.maximum(i * halo_per_tile - 1, 0), 0)),
            pl.BlockSpec((tm, D_MODEL), lambda i, j: (i, 0)),
            pl.BlockSpec((D_MODEL, group * 2 * tf), lambda i, j: (0, blk(j))),
            pl.BlockSpec((CONV_WIDTH, group * 2 * tf), lambda i, j: (0, blk(j))),
            pl.BlockSpec((1, group * 2 * tf), lambda i, j: (0, blk(j))),
            pl.BlockSpec((group * tf, D_MODEL), lambda i, j: (blk(j), 0)),
        ],
        out_specs=pl.BlockSpec((tm, D_MODEL), lambda i, j: (i, 0)),
        out_shape=jax.ShapeDtypeStruct((t, D_MODEL), F32),
        scratch_shapes=[pltpu.VMEM((tm + HALO, D_MODEL), BF16)]
        + [pltpu.VMEM((tm + HALO, 2 * tf), F32) for _ in range(group)],
        compiler_params=_params("arbitrary", "arbitrary"),
        name="ffn",
    )(u2, u2, h1, w_up_paired, _pair_tiles(conv_w, tf), _pair_tiles(conv_b, tf), w_down)


def _row(v):
    return v.reshape(1, -1).astype(F32)


def _layer(h2, pos, inv_freq, batch, seq, ln1_g, w_in, b_gate, conv_w, w_conv_out, q_a_g, w_q_b, kv_a_g, w_kv_b,
           q_norm_g, k_norm_g, w_mla_out, w_o, ln2_g, w_ffn_up, ffn_conv_w, ffn_conv_b, w_ffn_down):
    w_in_t = w_in.T
    w_lat = w_in_t[LAT_OFFSET:LAT_OFFSET + LAT_WIDTH].astype(BF16)
    wq3 = w_q_b.reshape(Q_LORA_RANK, N_HEADS, QK_HEAD_DIM)
    w_qn = wq3[:, :, :QK_NOPE_DIM].reshape(Q_LORA_RANK, -1).astype(BF16)
    w_qr = wq3[:, :, QK_NOPE_DIM:].reshape(Q_LORA_RANK, -1).astype(BF16)
    w_kv = w_kv_b.astype(BF16)
    twice = lambda g: _row(jnp.concatenate([g, g]))

    q, k, vt, u = _mla_prep(
        h2, pos, _row(ln1_g), w_lat, _row(q_a_g), _row(kv_a_g), w_qn, w_qr, w_kv,
        _row(q_norm_g[:QK_NOPE_DIM]), twice(q_norm_g[QK_NOPE_DIM:]),
        _row(k_norm_g[:QK_NOPE_DIM]), twice(k_norm_g[QK_NOPE_DIM:]), inv_freq)
    z = _in_proj(u, w_in_t)
    o, w_conv_out, w_mla_out, w_o, w_down = _attn(q, k, vt, _row(q_norm_g), _row(k_norm_g),
                                                  (w_conv_out, w_mla_out, w_o, w_ffn_down), batch, seq)
    h1, u2, w_up = _mix_out(z, o, h2, _row(b_gate), conv_w.astype(F32), w_conv_out, w_mla_out, w_o, _row(ln2_g),
                            w_ffn_up, seq)
    return _ffn(u2, h1, w_up, ffn_conv_w.astype(F32), _row(ffn_conv_b), w_down, seq)


def kernel(x, positions, ln1_g, w_in, b_gate, conv_w, w_conv_out, q_a_g, w_q_b, kv_a_g, w_kv_b,
           q_norm_g, k_norm_g, w_mla_out, w_o, ln2_g, w_ffn_up, ffn_conv_w, ffn_conv_b, w_ffn_down):
    batch, seq, _ = x.shape
    depth = ln1_g.shape[0]
    h2 = x.reshape(batch * seq, D_MODEL)
    pos = positions.reshape(batch * seq, 1).astype(jnp.int32)
    freq = ROPE_THETA ** (-jnp.arange(0, QK_ROPE_DIM, 2, dtype=F32) / QK_ROPE_DIM)
    inv_freq = jnp.tile(freq, LANES // (QK_ROPE_DIM // 2)).reshape(1, LANES)
    for l in range(depth):
        h2 = _layer(h2, pos, inv_freq, batch, seq, ln1_g[l], w_in[l], b_gate[l], conv_w[l], w_conv_out[l],
                    q_a_g[l], w_q_b[l], kv_a_g[l], w_kv_b[l], q_norm_g[l], k_norm_g[l], w_mla_out[l],
                    w_o[l], ln2_g[l], w_ffn_up[l], ffn_conv_w[l], ffn_conv_b[l], w_ffn_down[l])
    return h2.reshape(batch, seq, D_MODEL)
```
